```python
import math, functools
import jax, jax.numpy as jnp
from jax import lax
import numpy as np

D_MODEL = 1024
BATCH = 8
SEQ = 8192
DEPTH = 1
DEC_BATCH = 16
DEC_SEQ = 64
PAST_LEN = 4096

CHUNK = 64
EPS = 1e-6
SSM_INNER = 2 * D_MODEL
SSM_HEAD_DIM = 64
SSM_HEADS = SSM_INNER // SSM_HEAD_DIM
SSM_GROUPS = 4
SSM_STATE = 128
CONV_W = 4
CONV_CH = SSM_INNER + 2 * SSM_GROUPS * SSM_STATE
HG_HEADS = 8
HG_K = D_MODEL // HG_HEADS
HG_V = D_MODEL // HG_HEADS
HG_QK = HG_HEADS * HG_K
HG_VW = HG_HEADS * HG_V
N_MEM = 256
X_HEADS = 4
X_HEAD_DIM = D_MODEL // X_HEADS
N_EXPERTS = 32
TOP_K = 4
D_EXPERT = D_MODEL
SWIGLU_LIMIT = 7.0
SWIGLU_ALPHA = 1.702
MOE_BLOCK = 128
SPLITS = (SSM_INNER, CONV_CH, SSM_HEADS, HG_QK, HG_QK, HG_VW, HG_VW, D_MODEL, D_MODEL)
IN_COLS = sum(SPLITS)

kernel_name = 'streaming_ssd_hgrn2_moe_xattn'


def rmsnorm(x, g):
    xf = x.astype(jnp.float32)
    y = xf * lax.rsqrt(jnp.mean(xf * xf, axis=-1, keepdims=True) + EPS)
    return (y * g.astype(jnp.float32)).astype(x.dtype)


def causal_conv(x, buf, w, b):
    t = x.shape[1]
    xp = jnp.concatenate([buf.astype(x.dtype), x], axis=1)
    y = b
    for k in range(CONV_W):
        y = y + xp[:, k:k + t] * w[k]
    return y, xp[:, -(CONV_W - 1):]


def chunked_scan(step, s0, seqs):
    b, t = seqs[0].shape[:2]
    l = min(t, CHUNK)
    nc = t // l
    xs = tuple(jnp.swapaxes(a.reshape(b, nc, l, *a.shape[2:]), 0, 1) for a in seqs)
    s, ys = lax.scan(lambda c, inp: step(c, *inp), s0, xs)
    return jnp.swapaxes(ys, 0, 1).reshape(b, t, *ys.shape[3:]), s


def ssd_chunk(s, x, dt, bm, cm, a):
    b, l = x.shape[:2]
    hg = SSM_HEADS // SSM_GROUPS
    acum = jnp.cumsum(dt * a, axis=1)
    acum_h = jnp.swapaxes(acum, 1, 2)
    mask = jnp.tril(jnp.ones((l, l), dtype=bool))
    decay = jnp.exp(jnp.where(mask, acum_h[..., :, None] - acum_h[..., None, :], -jnp.inf))
    decay = decay.reshape(b, SSM_GROUPS, hg, l, l)
    cb = jnp.einsum('btgn,bsgn->bgts', cm, bm)
    dt_s = dt.reshape(b, l, SSM_GROUPS, hg).transpose(0, 2, 3, 1)
    wts = cb[:, :, None] * decay * dt_s[..., None, :]
    xg = x.reshape(b, l, SSM_GROUPS, hg, SSM_HEAD_DIM)
    sg = s.reshape(b, SSM_GROUPS, hg, SSM_HEAD_DIM, SSM_STATE)
    y = jnp.einsum('bghts,bsghp->btghp', wts, xg)
    y = y + jnp.einsum('btgn,bghpn->btghp', cm, sg) * jnp.exp(acum).reshape(b, l, SSM_GROUPS, hg)[..., None]
    tail = (jnp.exp(acum[:, -1:] - acum) * dt).reshape(b, l, SSM_GROUPS, hg)
    s_new = (jnp.exp(acum[:, -1]).reshape(b, SSM_GROUPS, hg)[..., None, None] * sg
             + jnp.einsum('bsgh,bsghp,bsgn->bghpn', tail, xg, bm))
    return (s_new.reshape(b, SSM_HEADS, SSM_HEAD_DIM, SSM_STATE),
            y.reshape(b, l, SSM_HEADS, SSM_HEAD_DIM))


def hgrn_chunk(s, q, logf, k, v):
    l = q.shape[1]
    gc = jnp.cumsum(logf, axis=1)
    mask = jnp.tril(jnp.ones((l, l), dtype=bool))[None, :, :, None, None]
    dec = jnp.exp(jnp.where(mask, gc[:, :, None] - gc[:, None, :], -jnp.inf))
    att = jnp.einsum('bthk,btshk,bshk->bhts', q, dec, k)
    o = jnp.einsum('bhts,bshv->bthv', att, v) + jnp.einsum('bthk,bhkv->bthv', q * jnp.exp(gc), s)
    s_new = (jnp.exp(gc[:, -1])[..., None] * s
             + jnp.einsum('bshk,bshv->bhkv', k * jnp.exp(gc[:, -1:] - gc), v))
    return s_new, o


def mixer(xn, conv_s, ssm_s, hg_s, lb, w_in, conv_w, conv_b, dt_bias, a_log, d_skip, g_ssm,
          w_ssm_branch, g_hgrn, w_hgrn_branch, w_out):
    f32 = jnp.float32
    b, t, _ = xn.shape
    proj = xn @ w_in
    idx = [int(c) for c in np.cumsum(SPLITS)[:-1]]
    z, xbc, dt_raw, q, f_raw, i_raw, g_o, gate_a, gate_b = jnp.split(proj, idx, axis=-1)

    xbc, conv_new = causal_conv(xbc, conv_s, conv_w, conv_b)
    xbc = jax.nn.silu(xbc)
    xs, bm, cm = jnp.split(xbc, [SSM_INNER, SSM_INNER + SSM_GROUPS * SSM_STATE], axis=-1)
    xs = xs.reshape(b, t, SSM_HEADS, SSM_HEAD_DIM).astype(f32)
    bm = bm.reshape(b, t, SSM_GROUPS, SSM_STATE).astype(f32)
    cm = cm.reshape(b, t, SSM_GROUPS, SSM_STATE).astype(f32)
    dt = jax.nn.softplus(dt_raw.astype(f32) + dt_bias.astype(f32))
    a = -jnp.exp(a_log.astype(f32))
    y, ssm_new = chunked_scan(lambda c, xc, dtc, bc, cc: ssd_chunk(c, xc, dtc, bc, cc, a),
                              ssm_s.astype(f32), (xs, dt, bm, cm))
    y = y + d_skip.astype(f32)[:, None] * xs
    yz = (y.reshape(b, t, SSM_INNER) * jax.nn.silu(z.astype(f32))).reshape(b, t, SSM_GROUPS, -1)
    yz = yz * lax.rsqrt(jnp.mean(yz * yz, axis=-1, keepdims=True) + EPS)
    y_a = (yz.reshape(b, t, SSM_INNER) * g_ssm.astype(f32)).astype(xn.dtype) @ w_ssm_branch

    lbv = lb.astype(f32).reshape(HG_HEADS, HG_K)
    fr = f_raw.astype(f32).reshape(b, t, HG_HEADS, HG_K)
    fg = lbv + (1.0 - lbv) * jax.nn.sigmoid(fr)
    logf = jnp.log(fg)
    kk = (1.0 - lbv) * jax.nn.sigmoid(-fr)
    qq = q.astype(f32).reshape(b, t, HG_HEADS, HG_K)
    vv = jax.nn.silu(i_raw.astype(f32)).reshape(b, t, HG_HEADS, HG_V)
    o, hg_new = chunked_scan(hgrn_chunk, hg_s.astype(f32), (qq, logf, kk, vv))
    o = o * lax.rsqrt(jnp.mean(o * o, axis=-1, keepdims=True) + EPS)
    o = o.reshape(b, t, HG_VW) * g_hgrn.astype(f32) * jax.nn.silu(g_o.astype(f32))
    y_b = o.astype(xn.dtype) @ w_hgrn_branch

    m = jax.nn.sigmoid(gate_a) * y_a + jax.nn.sigmoid(gate_b) * y_b
    return m @ w_out, conv_new, ssm_new, hg_new


def memory_kv(mem, g_mem, w_xk, w_xv):
    b, n, _ = mem.shape
    mn = rmsnorm(mem, g_mem)
    return ((mn @ w_xk).reshape(b, n, X_HEADS, X_HEAD_DIM),
            (mn @ w_xv).reshape(b, n, X_HEADS, X_HEAD_DIM))


def cross_attn(hn, mk, mv, w_xq, w_xo):
    b, t, _ = hn.shape
    q = (hn @ w_xq).reshape(b, t, X_HEADS, X_HEAD_DIM)
    s = jnp.einsum('bthd,bmhd->bhtm', q, mk.astype(hn.dtype)).astype(jnp.float32) * (X_HEAD_DIM ** -0.5)
    p = jax.nn.softmax(s, axis=-1).astype(hn.dtype)
    o = jnp.einsum('bhtm,bmhd->bthd', p, mv.astype(hn.dtype)).reshape(b, t, D_MODEL)
    return o @ w_xo


def moe(xn, w_router, b_router, w_gate_up, b_gate_up, w_down, b_down):
    f32 = jnp.float32
    b, t, d = xn.shape
    n_tok = b * t
    x2 = xn.reshape(n_tok, d)
    logits = x2.astype(f32) @ w_router.astype(f32) + b_router.astype(f32)
    top_val, top_idx = lax.top_k(logits, TOP_K)
    gates = jax.nn.softmax(top_val, axis=-1)
    n_rows = n_tok * TOP_K
    flat_e = top_idx.reshape(-1)
    flat_g = gates.reshape(-1)
    flat_tok = jnp.arange(n_rows, dtype=jnp.int32) // TOP_K
    order = jnp.argsort(flat_e)
    se = flat_e[order]
    counts = jnp.zeros((N_EXPERTS,), jnp.int32).at[flat_e].add(1)
    padded = (counts + MOE_BLOCK - 1) // MOE_BLOCK * MOE_BLOCK
    pad_end = jnp.cumsum(padded)
    pad_start = pad_end - padded
    start = jnp.cumsum(counts) - counts
    dest = pad_start[se] + jnp.arange(n_rows, dtype=jnp.int32) - start[se]
    n_blocks = -(-n_rows // MOE_BLOCK) + N_EXPERTS
    n_slots = n_blocks * MOE_BLOCK
    slot_tok = jnp.full((n_slots,), n_tok, jnp.int32).at[dest].set(flat_tok[order])
    slot_w = jnp.zeros((n_slots,), f32).at[dest].set(flat_g[order])
    block_e = jnp.minimum(jnp.searchsorted(pad_end, jnp.arange(n_blocks, dtype=jnp.int32) * MOE_BLOCK,
                                           side='right'), N_EXPERTS - 1)
    x_pad = jnp.concatenate([x2, jnp.zeros((1, d), x2.dtype)], axis=0)

    def step(acc, blk):
        tok, wgt, e = blk
        xb = x_pad[tok]
        gu = xb @ w_gate_up[e] + b_gate_up[e]
        gate = jnp.minimum(gu[:, :D_EXPERT], SWIGLU_LIMIT)
        up = jnp.clip(gu[:, D_EXPERT:], -SWIGLU_LIMIT, SWIGLU_LIMIT)
        hmid = (up + 1.0) * gate * jax.nn.sigmoid(SWIGLU_ALPHA * gate)
        yb = hmid @ w_down[e] + b_down[e]
        return acc.at[tok].add(yb.astype(f32) * wgt[:, None]), None

    acc0 = jnp.zeros((n_tok + 1, d), f32)
    acc, _ = lax.scan(step, acc0, (slot_tok.reshape(n_blocks, MOE_BLOCK),
                                   slot_w.reshape(n_blocks, MOE_BLOCK), block_e))
    return acc[:n_tok].astype(xn.dtype).reshape(b, t, d)


def block(h, mem_k, mem_v, conv_s, ssm_s, hg_s, *, lb, g_mix, w_in, conv_w, conv_b, dt_bias, a_log,
          d_skip, g_ssm, w_ssm_branch, g_hgrn, w_hgrn_branch, w_out, g_xattn, w_xq, w_xo, g_moe,
          w_router, b_router, w_gate_up, b_gate_up, w_down, b_down):
    mix, conv_new, ssm_new, hg_new = mixer(rmsnorm(h, g_mix), conv_s, ssm_s, hg_s, lb, w_in, conv_w, conv_b,
                                           dt_bias, a_log, d_skip, g_ssm, w_ssm_branch, g_hgrn,
                                           w_hgrn_branch, w_out)
    h = h + mix
    h = h + cross_attn(rmsnorm(h, g_xattn), mem_k, mem_v, w_xq, w_xo)
    h = h + moe(rmsnorm(h, g_moe), w_router, b_router, w_gate_up, b_gate_up, w_down, b_down)
    return h, conv_new.astype(h.dtype), ssm_new.astype(h.dtype), hg_new.astype(h.dtype)


def setup_inputs(seed: int = 0) -> dict:
    key = jax.random.key(seed)
    ks = iter(jax.random.split(key, 48))
    f32 = jnp.float32

    def nrm(shape, scale=1.0):
        return jax.random.normal(next(ks), shape, f32) * scale

    def gain(shape):
        return 1.0 + nrm(shape, 0.02)

    D = D_MODEL
    x_prompt = nrm((BATCH, SEQ, D))
    x_sample = nrm((DEC_BATCH, DEC_SEQ, D))
    mem_prompt = nrm((BATCH, N_MEM, D))
    state_conv = nrm((DEPTH, DEC_BATCH, CONV_W - 1, CONV_CH))
    state_ssm = nrm((DEPTH, DEC_BATCH, SSM_HEADS, SSM_HEAD_DIM, SSM_STATE), 0.5)
    state_hgrn = nrm((DEPTH, DEC_BATCH, HG_HEADS, HG_K, HG_V), 0.5)
    cache_mem_k = nrm((DEPTH, DEC_BATCH, N_MEM, X_HEADS, X_HEAD_DIM))
    cache_mem_v = nrm((DEPTH, DEC_BATCH, N_MEM, X_HEADS, X_HEAD_DIM))
    dt0 = jnp.exp(jax.random.uniform(next(ks), (DEPTH, SSM_HEADS), f32, math.log(1e-3), math.log(1e-1)))
    dt_bias = dt0 + jnp.log(-jnp.expm1(-dt0))
    a_log = jnp.log(jax.random.uniform(next(ks), (DEPTH, SSM_HEADS), f32, 1.0, 16.0))
    return {
        'x_prompt': x_prompt,
        'x_sample': x_sample,
        'mem_prompt': mem_prompt,
        'state_conv': state_conv,
        'state_ssm': state_ssm,
        'state_hgrn': state_hgrn,
        'cache_mem_k': cache_mem_k,
        'cache_mem_v': cache_mem_v,
        'g_mix': gain((DEPTH, D)),
        'w_in': nrm((DEPTH, D, IN_COLS), D ** -0.5),
        'conv_w': nrm((DEPTH, CONV_W, CONV_CH), CONV_W ** -0.5),
        'conv_b': nrm((DEPTH, CONV_CH), 0.02),
        'dt_bias': dt_bias,
        'a_log': a_log,
        'd_skip': gain((DEPTH, SSM_HEADS)),
        'g_ssm': gain((DEPTH, SSM_INNER)),
        'w_ssm_branch': nrm((DEPTH, SSM_INNER, D), SSM_INNER ** -0.5),
        'lb_logits': nrm((DEPTH + 1, HG_QK), 0.1),
        'g_hgrn': gain((DEPTH, HG_VW)),
        'w_hgrn_branch': nrm((DEPTH, HG_VW, D), HG_VW ** -0.5),
        'w_out': nrm((DEPTH, D, D), D ** -0.5),
        'g_mem': gain((DEPTH, D)),
        'w_xk': nrm((DEPTH, D, D), D ** -0.5),
        'w_xv': nrm((DEPTH, D, D), D ** -0.5),
        'g_xattn': gain((DEPTH, D)),
        'w_xq': nrm((DEPTH, D, D), D ** -0.5),
        'w_xo': nrm((DEPTH, D, D), D ** -0.5),
        'g_moe': gain((DEPTH, D)),
        'w_router': nrm((DEPTH, D, N_EXPERTS), D ** -0.5),
        'b_router': nrm((DEPTH, N_EXPERTS), 0.01),
        'w_gate_up': nrm((DEPTH, N_EXPERTS, D, 2 * D_EXPERT), D ** -0.5),
        'b_gate_up': nrm((DEPTH, N_EXPERTS, 2 * D_EXPERT), 0.02),
        'w_down': nrm((DEPTH, N_EXPERTS, D_EXPERT, D), D_EXPERT ** -0.5),
        'b_down': nrm((DEPTH, N_EXPERTS, D), 0.02),
        'g_final': gain((D,)),
    }


def reference(x_prompt, x_sample, mem_prompt, state_conv, state_ssm, state_hgrn, cache_mem_k, cache_mem_v,
              g_mix, w_in, conv_w, conv_b, dt_bias, a_log, d_skip, g_ssm, w_ssm_branch, lb_logits, g_hgrn,
              w_hgrn_branch, w_out, g_mem, w_xk, w_xv, g_xattn, w_xq, w_xo, g_moe, w_router, b_router,
              w_gate_up, b_gate_up, w_down, b_down, g_final):
    f32 = jnp.float32
    lb_all = jnp.cumsum(jax.nn.softmax(lb_logits.astype(f32), axis=0), axis=0)
    bp = x_prompt.shape[0]
    hp, hs = x_prompt, x_sample
    conv_p, ssm_p, hg_p, mk_p, mv_p = [], [], [], [], []
    conv_s, ssm_s, hg_s = [], [], []
    for l in range(DEPTH):
        blk = functools.partial(
            block, lb=lb_all[l], g_mix=g_mix[l], w_in=w_in[l], conv_w=conv_w[l], conv_b=conv_b[l],
            dt_bias=dt_bias[l], a_log=a_log[l], d_skip=d_skip[l], g_ssm=g_ssm[l],
            w_ssm_branch=w_ssm_branch[l], g_hgrn=g_hgrn[l], w_hgrn_branch=w_hgrn_branch[l], w_out=w_out[l],
            g_xattn=g_xattn[l], w_xq=w_xq[l], w_xo=w_xo[l], g_moe=g_moe[l], w_router=w_router[l],
            b_router=b_router[l], w_gate_up=w_gate_up[l], b_gate_up=b_gate_up[l], w_down=w_down[l],
            b_down=b_down[l])
        mk, mv = memory_kv(mem_prompt, g_mem[l], w_xk[l], w_xv[l])
        hp, c1, s1, g1 = blk(hp, mk, mv,
                             jnp.zeros((bp, CONV_W - 1, CONV_CH), hp.dtype),
                             jnp.zeros((bp, SSM_HEADS, SSM_HEAD_DIM, SSM_STATE), f32),
                             jnp.zeros((bp, HG_HEADS, HG_K, HG_V), f32))
        conv_p.append(c1); ssm_p.append(s1); hg_p.append(g1); mk_p.append(mk); mv_p.append(mv)
        hs, c2, s2, g2 = blk(hs, cache_mem_k[l], cache_mem_v[l], state_conv[l], state_ssm[l], state_hgrn[l])
        conv_s.append(c2); ssm_s.append(s2); hg_s.append(g2)
    y_prompt = rmsnorm(hp, g_final)
    y_sample = rmsnorm(hs, g_final)
    return (y_prompt, y_sample,
            jnp.stack(conv_p), jnp.stack(ssm_p), jnp.stack(hg_p), jnp.stack(mk_p), jnp.stack(mv_p),
            jnp.stack(conv_s), jnp.stack(ssm_s), jnp.stack(hg_s))
```

```python
import functools
import math

import jax
import jax.numpy as jnp
import numpy as np
from jax import lax
from jax.experimental import pallas as pl
from jax.experimental.pallas import tpu as pltpu

F32 = jnp.float32
BF16 = jnp.bfloat16

D_MODEL = 1024
CHUNK = 64
EPS = 1e-6
SSM_INNER = 2 * D_MODEL
SSM_HEAD_DIM = 64
SSM_HEADS = SSM_INNER // SSM_HEAD_DIM
SSM_GROUPS = 4
SSM_STATE = 128
GROUP_COLS = SSM_INNER // SSM_GROUPS
CONV_W = 4
BC_COLS = SSM_GROUPS * SSM_STATE
CONV_CH = SSM_INNER + 2 * BC_COLS
HG_HEADS = 8
HG_K = D_MODEL // HG_HEADS
N_MEM = 256
X_HEADS = 4
X_HEAD_DIM = D_MODEL // X_HEADS
N_EXPERTS = 32
TOP_K = 4
D_EXPERT = D_MODEL
SWIGLU_LIMIT = 7.0
SWIGLU_ALPHA = 1.702

LANES = 128
CONV_PAD = 8
MOE_ROWS = 256
VMEM_LIMIT = 56 * 1024 * 1024


def _const_spec(shape):
    nd = len(shape)
    return pl.BlockSpec(shape, lambda *_: (0,) * nd, pipeline_mode=pl.Buffered(1))


def _dot(a, b):
    return jnp.dot(a, b, preferred_element_type=F32)


def _dot_nt(a, b):
    return lax.dot_general(a, b, (((1,), (1,)), ((), ())), preferred_element_type=F32)


def _dot_tn(a, b):
    return lax.dot_general(a, b, (((0,), (0,)), ((), ())), preferred_element_type=F32)


def _split3(a):
    hi = a.astype(BF16)
    r1 = a - hi.astype(F32)
    mid = r1.astype(BF16)
    lo = (r1 - mid.astype(F32)).astype(BF16)
    return hi, mid, lo


def _exact_dot_rhs01(a, sel):
    hi, mid, lo = _split3(a)
    return _dot(hi, sel) + _dot(mid, sel) + _dot(lo, sel)


def _exact_dot_lhs01(sel, a):
    hi, mid, lo = _split3(a)
    return _dot(sel, hi) + _dot(sel, mid) + _dot(sel, lo)


def _rmsnorm(x, g):
    return x * lax.rsqrt(jnp.mean(x * x, axis=-1, keepdims=True) + EPS) * g


def _sigmoid(x):
    return jax.nn.sigmoid(x)


def _silu(x):
    return x * jax.nn.sigmoid(x)


def _softplus(x):
    return jnp.maximum(x, 0.0) + jnp.log1p(jnp.exp(-jnp.abs(x)))


def _ssd_kernel(x_ref, conv0_ref, ssm0_ref, g_ref, wxbc_ref, wz_ref, wdt_ref, wga_ref, convw_ref,
                convb_ref, dtb_ref, alog_ref, dskip_ref, gssm_ref, wbr_ref, tri_ref, expand_ref,
                eye_ref,
                ma_ref, convn_ref, ssmn_ref,
                xpad, xact, dts, ysc, st):
    tt = x_ref.shape[0]
    ti = pl.program_id(1)
    nt = pl.num_programs(1)

    @pl.when(ti == 0)
    def _():
        xpad[CONV_PAD - (CONV_W - 1):CONV_PAD, :] = conv0_ref[...]
        st[...] = ssm0_ref[...].reshape(SSM_INNER, SSM_STATE).T

    xn = _rmsnorm(x_ref[...], g_ref[...]).astype(BF16)
    xpad[CONV_PAD:CONV_PAD + tt, :] = _dot(xn, wxbc_ref[...])

    conv = convb_ref[...] + xpad[CONV_PAD:CONV_PAD + tt, :] * convw_ref[CONV_W - 1:CONV_W, :]
    for k in range(1, CONV_W):
        conv = conv + xpad[CONV_PAD - k:CONV_PAD - k + tt, :] * convw_ref[CONV_W - 1 - k:CONV_W - k, :]
    xact[...] = _silu(conv)
    tail = xpad[CONV_PAD + tt - (CONV_W - 1):CONV_PAD + tt, :]
    xpad[CONV_PAD - (CONV_W - 1):CONV_PAD, :] = tail

    dts[...] = _softplus(_dot(xn, wdt_ref[...]) + dtb_ref[...])
    a_neg = -jnp.exp(alog_ref[...])

    lane = lax.broadcasted_iota(jnp.int32, (CHUNK, LANES), 1)
    row = lax.broadcasted_iota(jnp.int32, (CHUNK, LANES), 0)
    causal2 = row >= (lane % CHUNK)
    left = lane < CHUNK

    def chunk_body(c, carry):
        r0 = pl.multiple_of(c * CHUNK, CHUNK)
        rows = pl.ds(r0, CHUNK)
        dt = dts[rows, :]
        acum = _exact_dot_lhs01(tri_ref[...], dt * a_neg)
        acum_x = _exact_dot_rhs01(acum, expand_ref[...])
        dt_x = _exact_dot_rhs01(dt, expand_ref[...])
        acum_r = jnp.sum(acum_x * eye_ref[...], axis=0, keepdims=True)
        dt_r = jnp.sum(dt_x * eye_ref[...], axis=0, keepdims=True)
        ea_x = jnp.exp(acum_x)
        tail_x = jnp.exp(acum_x[CHUNK - 1:CHUNK, :] - acum_x) * dt_x

        for g in range(SSM_GROUPS):
            gs = slice(g * GROUP_COLS, (g + 1) * GROUP_COLS)
            b_g = xact[rows, SSM_INNER + g * SSM_STATE:SSM_INNER + (g + 1) * SSM_STATE].astype(BF16)
            c_g = xact[rows, SSM_INNER + BC_COLS + g * SSM_STATE:
                       SSM_INNER + BC_COLS + (g + 1) * SSM_STATE].astype(BF16)
            cb2 = _dot_nt(c_g, jnp.concatenate([b_g, b_g], axis=0))
            for j in range(GROUP_COLS // LANES):
                ps = slice(g * GROUP_COLS + j * LANES, g * GROUP_COLS + (j + 1) * LANES)
                diff = acum_x[:, ps] - acum_r[:, ps]
                dec = jnp.where(causal2, jnp.exp(jnp.minimum(diff, 0.0)), 0.0)
                wts = (cb2 * dec * dt_r[:, ps]).astype(BF16)
                xp = xact[rows, ps]
                xbd = jnp.concatenate([jnp.where(left, xp, 0.0), jnp.where(left, 0.0, xp)],
                                      axis=0).astype(BF16)
                ysc[rows, ps] = _dot(wts, xbd)
            st_g = st[:, gs]
            y_inter = _dot(c_g, st_g.astype(BF16)) * ea_x[:, gs]
            ysc[rows, gs] = ysc[rows, gs] + y_inter
            xw = (tail_x[:, gs] * xact[rows, gs]).astype(BF16)
            st[:, gs] = ea_x[CHUNK - 1:CHUNK, gs] * st_g + _dot_tn(b_g, xw)
        return carry

    lax.fori_loop(0, tt // CHUNK, chunk_body, 0)

    xs = xact[:, :SSM_INNER]
    y = ysc[...] + dskip_ref[...] * xs
    yz = y * _silu(_dot(xn, wz_ref[...]))
    parts = []
    for g in range(SSM_GROUPS):
        blk = yz[:, g * GROUP_COLS:(g + 1) * GROUP_COLS]
        parts.append(blk * lax.rsqrt(jnp.mean(blk * blk, axis=-1, keepdims=True) + EPS))
    yn = (jnp.concatenate(parts, axis=1) * gssm_ref[...]).astype(BF16)
    ya = _dot(yn, wbr_ref[...])
    ma_ref[...] = _sigmoid(_dot(xn, wga_ref[...])) * ya

    @pl.when(ti == nt - 1)
    def _():
        convn_ref[...] = tail
        ssmn_ref[...] = st[...].T.reshape(SSM_HEADS, SSM_HEAD_DIM, SSM_STATE)


def _ssd_mixer(x, conv0, ssm0, p, tt):
    b, t, d = x.shape
    grid = (b, t // tt)
    row_spec = pl.BlockSpec((None, tt, d), lambda i, j: (i, j, 0))
    consts = [p['g_mix'], p['w_xbc'], p['w_z'], p['w_dt'], p['w_ga'], p['conv_w'], p['conv_b'], p['dt_bias'],
              p['a_log'], p['d_skip'], p['g_ssm'], p['w_ssm_branch'], p['tri_incl'], p['expand'], p['eye_tiled']]
    return pl.pallas_call(
        _ssd_kernel,
        grid=grid,
        in_specs=[row_spec,
                  pl.BlockSpec((None, CONV_W - 1, CONV_CH), lambda i, j: (i, 0, 0)),
                  pl.BlockSpec((None, SSM_HEADS, SSM_HEAD_DIM, SSM_STATE), lambda i, j: (i, 0, 0, 0))]
                 + [_const_spec(c.shape) for c in consts],
        out_specs=[row_spec,
                   pl.BlockSpec((None, CONV_W - 1, CONV_CH), lambda i, j: (i, 0, 0)),
                   pl.BlockSpec((None, SSM_HEADS, SSM_HEAD_DIM, SSM_STATE), lambda i, j: (i, 0, 0, 0))],
        out_shape=[jax.ShapeDtypeStruct((b, t, d), F32),
                   jax.ShapeDtypeStruct((b, CONV_W - 1, CONV_CH), F32),
                   jax.ShapeDtypeStruct((b, SSM_HEADS, SSM_HEAD_DIM, SSM_STATE), F32)],
        scratch_shapes=[pltpu.VMEM((CONV_PAD + tt, CONV_CH), F32),
                        pltpu.VMEM((tt, CONV_CH), F32),
                        pltpu.VMEM((tt, LANES), F32),
                        pltpu.VMEM((tt, SSM_INNER), F32),
                        pltpu.VMEM((SSM_STATE, SSM_INNER), F32)],
        compiler_params=pltpu.CompilerParams(dimension_semantics=("arbitrary", "arbitrary"),
                                             vmem_limit_bytes=VMEM_LIMIT),
        name="ssd_mixer",
    )(x, conv0, ssm0, *consts)


def _hgrn_kernel(x_ref, hg0_ref, g_ref, wh_ref, wgb_ref, lbl_ref, ghg_ref, wbr_ref, tri_ref,
                 mb_ref, hgn_ref,
                 qs, lfs, ks, vs, osc, st):
    tt = x_ref.shape[0]
    ti = pl.program_id(1)
    nt = pl.num_programs(1)

    @pl.when(ti == 0)
    def _():
        for h in range(HG_HEADS):
            st[h] = hg0_ref[h].T

    xn = _rmsnorm(x_ref[...], g_ref[...]).astype(BF16)
    proj = _dot(xn, wh_ref[...])
    l0 = lbl_ref[0:1, :]
    l1 = lbl_ref[1:2, :]
    lmax = jnp.maximum(l0, l1)
    e0 = jnp.exp(l0 - lmax)
    lb = e0 / (e0 + jnp.exp(l1 - lmax))
    fr = proj[:, D_MODEL:2 * D_MODEL]
    qs[...] = proj[:, :D_MODEL]
    lfs[...] = jnp.log(lb + (1.0 - lb) * _sigmoid(fr))
    ks[...] = (1.0 - lb) * _sigmoid(-fr)
    vs[...] = _silu(proj[:, 2 * D_MODEL:3 * D_MODEL])

    row = lax.broadcasted_iota(jnp.int32, (CHUNK, CHUNK), 0)
    col = lax.broadcasted_iota(jnp.int32, (CHUNK, CHUNK), 1)
    causal = row >= col

    def chunk_body(c, carry):
        r0 = pl.multiple_of(c * CHUNK, CHUNK)
        rows = pl.ds(r0, CHUNK)
        gc = _exact_dot_lhs01(tri_ref[...], lfs[rows, :])
        for h in range(HG_HEADS):
            hs = slice(h * HG_K, (h + 1) * HG_K)
            g = gc[:, hs]
            glast = g[CHUNK - 1:CHUNK, :]
            qh = (qs[rows, hs] * jnp.exp(g)).astype(BF16)
            kh = (ks[rows, hs] * jnp.exp(-g)).astype(BF16)
            att = jnp.where(causal, _dot_nt(qh, kh), 0.0).astype(BF16)
            v = vs[rows, hs].astype(BF16)
            st_h = st[h]
            osc[rows, hs] = _dot(att, v) + _dot_nt(qh, st_h.astype(BF16))
            kt = (ks[rows, hs] * jnp.exp(glast - g)).astype(BF16)
            st[h] = jnp.exp(glast) * st_h + _dot_tn(v, kt)
        return carry

    lax.fori_loop(0, tt // CHUNK, chunk_body, 0)

    parts = []
    for h in range(HG_HEADS):
        blk = osc[:, h * HG_K:(h + 1) * HG_K]
        parts.append(blk * lax.rsqrt(jnp.mean(blk * blk, axis=-1, keepdims=True) + EPS))
    on = (jnp.concatenate(parts, axis=1) * ghg_ref[...] * _silu(proj[:, 3 * D_MODEL:])).astype(BF16)
    yb = _dot(on, wbr_ref[...])
    mb_ref[...] = _sigmoid(_dot(xn, wgb_ref[...])) * yb

    @pl.when(ti == nt - 1)
    def _():
        for h in range(HG_HEADS):
            hgn_ref[h] = st[h].T


def _hgrn_mixer(x, hg0, p, tt):
    b, t, d = x.shape
    grid = (b, t // tt)
    row_spec = pl.BlockSpec((None, tt, d), lambda i, j: (i, j, 0))
    st_spec = pl.BlockSpec((None, HG_HEADS, HG_K, HG_K), lambda i, j: (i, 0, 0, 0))
    consts = [p['g_mix'], p['w_hg'], p['w_gb'], p['lb_logits'], p['g_hgrn'], p['w_hgrn_branch'], p['tri_incl']]
    return pl.pallas_call(
        _hgrn_kernel,
        grid=grid,
        in_specs=[row_spec, st_spec] + [_const_spec(c.shape) for c in consts],
        out_specs=[row_spec, st_spec],
        out_shape=[jax.ShapeDtypeStruct((b, t, d), F32),
                   jax.ShapeDtypeStruct((b, HG_HEADS, HG_K, HG_K), F32)],
        scratch_shapes=[pltpu.VMEM((tt, d), F32)] * 5 + [pltpu.VMEM((HG_HEADS, HG_K, HG_K), F32)],
        compiler_params=pltpu.CompilerParams(dimension_semantics=("arbitrary", "arbitrary"),
                                             vmem_limit_bytes=VMEM_LIMIT),
        name="hgrn_mixer",
    )(x, hg0, *consts)


def _memkv_kernel(m_ref, g_ref, wk_ref, wv_ref, k_ref, v_ref):
    mn = _rmsnorm(m_ref[...], g_ref[...]).astype(BF16)
    k_ref[...] = _dot(mn, wk_ref[...])
    v_ref[...] = _dot(mn, wv_ref[...])


def _memory_kv(mem, p):
    b, n, d = mem.shape
    spec = pl.BlockSpec((None, n, d), lambda i: (i, 0, 0))
    consts = [p['g_mem'], p['w_xk'], p['w_xv']]
    return pl.pallas_call(
        _memkv_kernel,
        grid=(b,),
        in_specs=[spec] + [_const_spec(c.shape) for c in consts],
        out_specs=[spec, spec],
        out_shape=[jax.ShapeDtypeStruct((b, n, d), F32)] * 2,
        compiler_params=pltpu.CompilerParams(dimension_semantics=("arbitrary",),
                                             vmem_limit_bytes=VMEM_LIMIT),
        name="memory_kv",
    )(mem, *consts)


def _post_kernel(x_ref, ma_ref, mb_ref, mk_ref, mv_ref, cnt0_ref, wo_ref, gx_ref, wq_ref, wxo_ref,
                 gmoe_ref, wr_ref, br_ref, tril_ref,
                 h_ref, xn_ref, meta_ref, gate_ref, cnt_ref,
                 base):
    tm = x_ref.shape[0]
    first = jnp.logical_and(pl.program_id(0) == 0, pl.program_id(1) == 0)

    @pl.when(first)
    def _():
        base[...] = cnt0_ref[...]

    m = (ma_ref[...] + mb_ref[...]).astype(BF16)
    h1 = x_ref[...] + _dot(m, wo_ref[...])

    hn = _rmsnorm(h1, gx_ref[...]).astype(BF16)
    q = _dot(hn, wq_ref[...])
    heads = []
    for hh in range(X_HEADS):
        hs = slice(hh * X_HEAD_DIM, (hh + 1) * X_HEAD_DIM)
        s = _dot_nt(q[:, hs].astype(BF16), mk_ref[:, hs].astype(BF16)) * (X_HEAD_DIM ** -0.5)
        s = s - jnp.max(s, axis=-1, keepdims=True)
        e = jnp.exp(s)
        pr = e / jnp.sum(e, axis=-1, keepdims=True)
        heads.append(_dot(pr.astype(BF16), mv_ref[:, hs].astype(BF16)))
    o = jnp.concatenate(heads, axis=1).astype(BF16)
    h2 = h1 + _dot(o, wxo_ref[...])
    h_ref[...] = h2

    xn3 = _rmsnorm(h2, gmoe_ref[...])
    xn_ref[...] = xn3
    logits = _dot(xn3.astype(BF16), wr_ref[...]) + br_ref[...]

    lane = lax.broadcasted_iota(jnp.int32, (tm, LANES), 1)
    lane_f = lane.astype(F32)
    run = logits
    vals, hots = [], []
    for _ in range(TOP_K):
        mx = jnp.max(run, axis=-1, keepdims=True)
        idx = jnp.min(jnp.where(run == mx, lane_f, float(LANES)), axis=-1, keepdims=True)
        hot = lane_f == idx
        run = jnp.where(hot, -jnp.inf, run)
        vals.append(mx)
        hots.append(hot)
    es = [jnp.exp(v - vals[0]) for v in vals]
    den = es[0] + es[1] + es[2] + es[3]

    tot = jnp.zeros((tm, LANES), F32)
    for hot in hots:
        tot = tot + hot.astype(F32)
    before = base[...] + _dot(tril_ref[...], tot.astype(BF16))
    base[...] = base[...] + jnp.sum(tot, axis=0, keepdims=True)
    cnt_ref[...] = base[...]

    meta = jnp.zeros((tm, LANES), F32)
    gates = jnp.zeros((tm, LANES), F32)
    for j in range(TOP_K):
        e_idx = jnp.sum(jnp.where(hots[j], lane_f, 0.0), axis=-1, keepdims=True)
        rank = jnp.sum(jnp.where(hots[j], before, 0.0), axis=-1, keepdims=True)
        meta = jnp.where(lane == j, e_idx, meta)
        meta = jnp.where(lane == TOP_K + j, rank, meta)
        gates = jnp.where(lane == j, es[j] / den, gates)
    meta_ref[...] = meta.astype(jnp.int32)
    gate_ref[...] = gates


def _post_mixer(x, ma, mb, mk, mv, cnt0, p, tm):
    b, t, d = x.shape
    grid = (b, t // tm)
    row_spec = pl.BlockSpec((None, tm, d), lambda i, j: (i, j, 0))
    small_spec = pl.BlockSpec((None, tm, LANES), lambda i, j: (i, j, 0))
    mem_spec = pl.BlockSpec((None, N_MEM, d), lambda i, j: (i, 0, 0))
    cnt_spec = pl.BlockSpec((1, LANES), lambda i, j: (0, 0))
    tril = jnp.tril(jnp.ones((tm, tm), BF16), -1)
    consts = [p['w_out'], p['g_xattn'], p['w_xq'], p['w_xo'], p['g_moe'], p['w_router'], p['b_router'], tril]
    return pl.pallas_call(
        _post_kernel,
        grid=grid,
        in_specs=[row_spec, row_spec, row_spec, mem_spec, mem_spec, cnt_spec]
                 + [_const_spec(c.shape) for c in consts],
        out_specs=[row_spec, row_spec, small_spec, small_spec, cnt_spec],
        out_shape=[jax.ShapeDtypeStruct((b, t, d), F32),
                   jax.ShapeDtypeStruct((b, t, d), F32),
                   jax.ShapeDtypeStruct((b, t, LANES), jnp.int32),
                   jax.ShapeDtypeStruct((b, t, LANES), F32),
                   jax.ShapeDtypeStruct((1, LANES), F32)],
        scratch_shapes=[pltpu.VMEM((1, LANES), F32)],
        compiler_params=pltpu.CompilerParams(dimension_semantics=("arbitrary", "arbitrary"),
                                             vmem_limit_bytes=VMEM_LIMIT),
        name="post_mixer",
    )(x, ma, mb, mk, mv, cnt0, *consts)


def _dispatch_kernel(dest_ref, x_ref, slots_in_ref, slots_ref, sem):
    del slots_in_ref
    tm = x_ref.shape[0]

    def row_copy(t, j):
        return pltpu.make_async_copy(x_ref.at[pl.ds(t, 1), :],
                                     slots_ref.at[pl.ds(dest_ref[t * TOP_K + j], 1), :], sem)

    def issue(t, carry):
        for j in range(TOP_K):
            row_copy(t, j).start()
        return carry

    lax.fori_loop(0, tm, issue, 0)

    def drain(t, carry):
        for j in range(TOP_K):
            row_copy(t, j).wait()
        return carry

    lax.fori_loop(0, tm, drain, 0)


def _dispatch(xn, dest, n_slots, tm):
    n, d = xn.shape
    slots0 = jnp.zeros((n_slots, d), F32)
    return pl.pallas_call(
        _dispatch_kernel,
        grid=(n // tm,),
        in_specs=[pl.BlockSpec((tm * TOP_K,), lambda i: (i,), memory_space=pltpu.SMEM),
                  pl.BlockSpec((tm, d), lambda i: (i, 0)),
                  pl.BlockSpec(memory_space=pl.ANY)],
        out_specs=pl.BlockSpec(memory_space=pl.ANY),
        out_shape=jax.ShapeDtypeStruct((n_slots, d), F32),
        scratch_shapes=[pltpu.SemaphoreType.DMA(())],
        input_output_aliases={2: 0},
        compiler_params=pltpu.CompilerParams(dimension_semantics=("arbitrary",),
                                             has_side_effects=True),
        name="moe_dispatch",
    )(dest, xn, slots0)


def _moe_kernel(be_ref, x_ref, wgu_ref, bgu_ref, wd_ref, bd_ref, y_ref):
    del be_ref
    xb = x_ref[...].astype(BF16)
    gu = _dot(xb, wgu_ref[...]) + bgu_ref[...]
    gate = jnp.minimum(gu[:, :D_EXPERT], SWIGLU_LIMIT)
    up = jnp.clip(gu[:, D_EXPERT:], -SWIGLU_LIMIT, SWIGLU_LIMIT)
    hmid = ((up + 1.0) * gate * _sigmoid(SWIGLU_ALPHA * gate)).astype(BF16)
    y_ref[...] = _dot(hmid, wd_ref[...]) + bd_ref[...]


def _moe_experts(slots, block_e, p):
    n_slots, d = slots.shape
    n_blocks = n_slots // MOE_ROWS
    grid_spec = pltpu.PrefetchScalarGridSpec(
        num_scalar_prefetch=1,
        grid=(n_blocks,),
        in_specs=[pl.BlockSpec((MOE_ROWS, d), lambda i, be: (i, 0)),
                  pl.BlockSpec((None, d, 2 * D_EXPERT), lambda i, be: (be[i], 0, 0)),
                  pl.BlockSpec((None, 1, 2 * D_EXPERT), lambda i, be: (be[i], 0, 0)),
                  pl.BlockSpec((None, D_EXPERT, d), lambda i, be: (be[i], 0, 0)),
                  pl.BlockSpec((None, 1, d), lambda i, be: (be[i], 0, 0))],
        out_specs=pl.BlockSpec((MOE_ROWS, d), lambda i, be: (i, 0)),
    )
    return pl.pallas_call(
        _moe_kernel,
        grid_spec=grid_spec,
        out_shape=jax.ShapeDtypeStruct((n_slots, d), F32),
        compiler_params=pltpu.CompilerParams(dimension_semantics=("arbitrary",),
                                             vmem_limit_bytes=VMEM_LIMIT),
        name="moe_experts",
    )(block_e, slots, p['w_gate_up'], p['b_gate_up'], p['w_down'], p['b_down'])


def _combine_kernel(dest_ref, h_ref, gate_ref, gfin_ref, y_hbm, out_ref, buf, sem):
    tm = h_ref.shape[0]

    def row_copy(t, j):
        return pltpu.make_async_copy(y_hbm.at[pl.ds(dest_ref[t * TOP_K + j], 1), :],
                                     buf.at[j, pl.ds(t, 1), :], sem)

    def issue(t, carry):
        for j in range(TOP_K):
            row_copy(t, j).start()
        return carry

    lax.fori_loop(0, tm, issue, 0)

    def drain(t, carry):
        for j in range(TOP_K):
            row_copy(t, j).wait()
        return carry

    lax.fori_loop(0, tm, drain, 0)

    acc = h_ref[...]
    gates = gate_ref[...]
    for j in range(TOP_K):
        acc = acc + gates[:, j:j + 1] * buf[j]
    out_ref[...] = _rmsnorm(acc, gfin_ref[...])


def _combine(h, gates, dest, y_slots, g_final, tm):
    n, d = h.shape
    return pl.pallas_call(
        _combine_kernel,
        grid=(n // tm,),
        in_specs=[pl.BlockSpec((tm * TOP_K,), lambda i: (i,), memory_space=pltpu.SMEM),
                  pl.BlockSpec((tm, d), lambda i: (i, 0)),
                  pl.BlockSpec((tm, LANES), lambda i: (i, 0)),
                  _const_spec(g_final.shape),
                  pl.BlockSpec(memory_space=pl.ANY)],
        out_specs=pl.BlockSpec((tm, d), lambda i: (i, 0)),
        out_shape=jax.ShapeDtypeStruct((n, d), F32),
        scratch_shapes=[pltpu.VMEM((TOP_K, tm, d), F32), pltpu.SemaphoreType.DMA(())],
        compiler_params=pltpu.CompilerParams(dimension_semantics=("arbitrary",),
                                             vmem_limit_bytes=VMEM_LIMIT),
        name="moe_combine",
    )(dest, h, gates, g_final, y_slots)


def _moe(h, xn, meta, gates, counts, p, g_final):
    n, d = h.shape
    n_rows = n * TOP_K
    n_blocks = n_rows // MOE_ROWS + N_EXPERTS
    n_slots = n_blocks * MOE_ROWS
    cnt = counts[0, :N_EXPERTS].astype(jnp.int32)
    padded = (cnt + MOE_ROWS - 1) // MOE_ROWS * MOE_ROWS
    pad_end = jnp.cumsum(padded)
    pad_start = pad_end - padded
    block_e = jnp.minimum(jnp.searchsorted(pad_end, jnp.arange(n_blocks, dtype=jnp.int32) * MOE_ROWS,
                                           side='right'), N_EXPERTS - 1).astype(jnp.int32)
    dest = (pad_start[meta[:, :TOP_K]] + meta[:, TOP_K:2 * TOP_K]).reshape(-1).astype(jnp.int32)
    tm = min(512, n)
    slots = _dispatch(xn, dest, n_slots, tm)
    y_slots = _moe_experts(slots, block_e, p)
    tmc = min(256, n)
    return _combine(h, gates, dest, y_slots, g_final, tmc)


def _prepare(g_mix, w_in, conv_w, conv_b, dt_bias, a_log, d_skip, g_ssm, w_ssm_branch, lb_logits, g_hgrn,
             w_hgrn_branch, w_out, g_mem, w_xk, w_xv, g_xattn, w_xq, w_xo, g_moe, w_router, b_router,
             w_gate_up, b_gate_up, w_down, b_down):
    d = D_MODEL
    w = w_in[0]
    o_z, o_xbc = 0, SSM_INNER
    o_dt = o_xbc + CONV_CH
    o_q = o_dt + SSM_HEADS
    o_ga = o_q + 4 * d
    o_gb = o_ga + d

    def row(v):
        return v.reshape(1, -1).astype(F32)

    def lane_pad(v, fill=0.0):
        return jnp.pad(v.reshape(1, -1).astype(F32), ((0, 0), (0, LANES - v.shape[-1])), constant_values=fill)

    head_of_col = np.arange(SSM_INNER) // SSM_HEAD_DIM
    expand = (np.arange(LANES)[:, None] == head_of_col[None, :])
    eye_tiled = (np.arange(CHUNK)[:, None] == (np.arange(SSM_INNER) % CHUNK)[None, :])
    return dict(
        g_mix=row(g_mix[0]),
        w_z=w[:, o_z:o_z + SSM_INNER].astype(BF16),
        w_xbc=w[:, o_xbc:o_xbc + CONV_CH].astype(BF16),
        w_dt=jnp.pad(w[:, o_dt:o_dt + SSM_HEADS], ((0, 0), (0, LANES - SSM_HEADS))).astype(BF16),
        w_hg=w[:, o_q:o_q + 4 * d].astype(BF16),
        w_ga=w[:, o_ga:o_ga + d].astype(BF16),
        w_gb=w[:, o_gb:o_gb + d].astype(BF16),
        conv_w=conv_w[0].astype(F32),
        conv_b=row(conv_b[0]),
        dt_bias=lane_pad(dt_bias[0]),
        a_log=lane_pad(a_log[0]),
        d_skip=row(jnp.repeat(d_skip[0], SSM_HEAD_DIM)),
        g_ssm=row(g_ssm[0]),
        w_ssm_branch=w_ssm_branch[0].astype(BF16),
        lb_logits=lb_logits.astype(F32),
        g_hgrn=row(g_hgrn[0]),
        w_hgrn_branch=w_hgrn_branch[0].astype(BF16),
        w_out=w_out[0].astype(BF16),
        g_mem=row(g_mem[0]),
        w_xk=w_xk[0].astype(BF16),
        w_xv=w_xv[0].astype(BF16),
        g_xattn=row(g_xattn[0]),
        w_xq=w_xq[0].astype(BF16),
        w_xo=w_xo[0].astype(BF16),
        g_moe=row(g_moe[0]),
        w_router=jnp.pad(w_router[0], ((0, 0), (0, LANES - N_EXPERTS))).astype(BF16),
        b_router=lane_pad(b_router[0], fill=-jnp.inf),
        w_gate_up=w_gate_up[0].astype(BF16),
        b_gate_up=b_gate_up[0].reshape(N_EXPERTS, 1, 2 * D_EXPERT).astype(F32),
        w_down=w_down[0].astype(BF16),
        b_down=b_down[0].reshape(N_EXPERTS, 1, d).astype(F32),
        tri_incl=jnp.asarray(np.tril(np.ones((CHUNK, CHUNK))), BF16),
        expand=jnp.asarray(expand, BF16),
        eye_tiled=jnp.asarray(eye_tiled, F32),
    )


def _group_step(x, conv0, ssm0, hg0, mk, mv, p, g_final):
    b, t, d = x.shape
    tt = min(256, t)
    ma, conv_n, ssm_n = _ssd_mixer(x, conv0, ssm0, p, tt)
    mb, hg_n = _hgrn_mixer(x, hg0, p, tt)
    tm = min(512, t)
    cnt0 = jnp.zeros((1, LANES), F32)
    h2, xn3, meta, gates, counts = _post_mixer(x, ma, mb, mk, mv, cnt0, p, tm)
    n = b * t
    y = _moe(h2.reshape(n, d), xn3.reshape(n, d), meta.reshape(n, LANES), gates.reshape(n, LANES),
             counts, p, g_final)
    return y.reshape(b, t, d), conv_n, ssm_n, hg_n


def kernel(x_prompt, x_sample, mem_prompt, state_conv, state_ssm, state_hgrn, cache_mem_k, cache_mem_v, g_mix, w_in, conv_w, conv_b, dt_bias, a_log, d_skip, g_ssm, w_ssm_branch, lb_logits, g_hgrn, w_hgrn_branch, w_out, g_mem, w_xk, w_xv, g_xattn, w_xq, w_xo, g_moe, w_router, b_router, w_gate_up, b_gate_up, w_down, b_down, g_final):
    p = _prepare(g_mix, w_in, conv_w, conv_b, dt_bias, a_log, d_skip, g_ssm, w_ssm_branch, lb_logits, g_hgrn,
                 w_hgrn_branch, w_out, g_mem, w_xk, w_xv, g_xattn, w_xq, w_xo, g_moe, w_router, b_router,
                 w_gate_up, b_gate_up, w_down, b_down)
    g_fin = g_final.reshape(1, -1).astype(F32)
    bp = x_prompt.shape[0]
    bs = x_sample.shape[0]
    d = D_MODEL

    mk_p, mv_p = _memory_kv(mem_prompt, p)
    y_p, conv_p, ssm_p, hg_p = _group_step(
        x_prompt,
        jnp.zeros((bp, CONV_W - 1, CONV_CH), F32),
        jnp.zeros((bp, SSM_HEADS, SSM_HEAD_DIM, SSM_STATE), F32),
        jnp.zeros((bp, HG_HEADS, HG_K, HG_K), F32),
        mk_p, mv_p, p, g_fin)
    y_s, conv_s, ssm_s, hg_s = _group_step(
        x_sample, state_conv[0], state_ssm[0], state_hgrn[0],
        cache_mem_k[0].reshape(bs, N_MEM, d), cache_mem_v[0].reshape(bs, N_MEM, d), p, g_fin)

    kv_shape = (1, bp, N_MEM, X_HEADS, X_HEAD_DIM)
    return (y_p, y_s,
            conv_p[None], ssm_p[None], hg_p[None], mk_p.reshape(kv_shape), mv_p.reshape(kv_shape),
            conv_s[None], ssm_s[None], hg_s[None])
```

```python
import jax
import jax.numpy as jnp
import numpy as np
from jax import lax
from jax.experimental import pallas as pl
from jax.experimental.pallas import tpu as pltpu

F32 = jnp.float32
BF16 = jnp.bfloat16

D_MODEL = 1024
CHUNK = 64
EPS = 1e-6
SSM_INNER = 2 * D_MODEL
SSM_HEAD_DIM = 64
SSM_HEADS = SSM_INNER // SSM_HEAD_DIM
SSM_GROUPS = 4
SSM_STATE = 128
GROUP_COLS = SSM_INNER // SSM_GROUPS
CONV_W = 4
BC_COLS = SSM_GROUPS * SSM_STATE
CONV_CH = SSM_INNER + 2 * BC_COLS
HG_HEADS = 8
HG_K = D_MODEL // HG_HEADS
N_MEM = 256
X_HEADS = 4
X_HEAD_DIM = D_MODEL // X_HEADS
N_EXPERTS = 32
TOP_K = 4
D_EXPERT = D_MODEL
SWIGLU_LIMIT = 7.0
SWIGLU_ALPHA = 1.702

LANES = 128
CONV_PAD = 8
MOE_ROWS = 256
MIXER_TILE = 256
POST_TILE = 512
DISPATCH_TILE = 512
COMBINE_TILE = 256
VMEM_LIMIT = 56 * 1024 * 1024


def _const_spec(shape):
    nd = len(shape)
    return pl.BlockSpec(shape, lambda *_: (0,) * nd, pipeline_mode=pl.Buffered(1))


def _dot(a, b):
    return jnp.dot(a, b, preferred_element_type=F32)


def _dot_nt(a, b):
    return lax.dot_general(a, b, (((1,), (1,)), ((), ())), preferred_element_type=F32)


def _dot_tn(a, b):
    return lax.dot_general(a, b, (((0,), (0,)), ((), ())), preferred_element_type=F32)


def _split3(a):
    hi = a.astype(BF16)
    r1 = a - hi.astype(F32)
    mid = r1.astype(BF16)
    lo = (r1 - mid.astype(F32)).astype(BF16)
    return hi, mid, lo


def _exact_dot_lhs01(sel, a):
    hi, mid, lo = _split3(a)
    return _dot(sel, hi) + _dot(sel, mid) + _dot(sel, lo)


def _rmsnorm(x, g):
    return x * lax.rsqrt(jnp.mean(x * x, axis=-1, keepdims=True) + EPS) * g


def _sigmoid(x):
    return jax.nn.sigmoid(x)


def _silu(x):
    return x * jax.nn.sigmoid(x)


def _softplus(x):
    return jnp.maximum(x, 0.0) + jnp.log1p(jnp.exp(-jnp.abs(x)))


def _ssd_kernel(x_ref, conv0_ref, ssm0_ref, g_ref, wxbc_ref, wz_ref, wdt_ref, wga_ref, convw_ref,
                convb_ref, dtb_ref, alog_ref, dskip_ref, gssm_ref, wbr_ref, tri_ref, expand_ref,
                ma_ref, convn_ref, ssmn_ref,
                xpad, xact, xdt, ysc, st):
    tt = x_ref.shape[0]
    ti = pl.program_id(1)
    nt = pl.num_programs(1)

    @pl.when(ti == 0)
    def _():
        xpad[CONV_PAD - (CONV_W - 1):CONV_PAD, :] = conv0_ref[...]
        st[...] = ssm0_ref[...].reshape(SSM_INNER, SSM_STATE).T

    xn = _rmsnorm(x_ref[...], g_ref[...]).astype(BF16)
    xpad[CONV_PAD:CONV_PAD + tt, :] = _dot(xn, wxbc_ref[...])

    conv = convb_ref[...] + xpad[CONV_PAD:CONV_PAD + tt, :] * convw_ref[CONV_W - 1:CONV_W, :]
    for k in range(1, CONV_W):
        conv = conv + xpad[CONV_PAD - k:CONV_PAD - k + tt, :] * convw_ref[CONV_W - 1 - k:CONV_W - k, :]
    xact[...] = _silu(conv)
    tail = xpad[CONV_PAD + tt - (CONV_W - 1):CONV_PAD + tt, :]
    xpad[CONV_PAD - (CONV_W - 1):CONV_PAD, :] = tail

    dt = _softplus(_dot(xn, wdt_ref[...]) + dtb_ref[...])
    acum_all = _exact_dot_lhs01(tri_ref[...], dt * -jnp.exp(alog_ref[...]))
    dt_hi = dt.astype(BF16)
    dt_lo = (dt - dt_hi.astype(F32)).astype(BF16)
    dt_x = _dot(dt_hi, expand_ref[...]) + _dot(dt_lo, expand_ref[...])
    xdt[...] = xact[:, :SSM_INNER] * dt_x

    lane = lax.broadcasted_iota(jnp.int32, (CHUNK, LANES), 1)
    row = lax.broadcasted_iota(jnp.int32, (CHUNK, LANES), 0)
    causal2 = row >= (lane % CHUNK)
    diag2 = row == (lane % CHUNK)
    left = lane < CHUNK
    pair_of_lane = lane // CHUNK

    for c in range(tt // CHUNK):
        rows = slice(c * CHUNK, (c + 1) * CHUNK)
        acum = acum_all[rows, :]
        for g in range(SSM_GROUPS):
            gs = slice(g * GROUP_COLS, (g + 1) * GROUP_COLS)
            b_g = xact[rows, SSM_INNER + g * SSM_STATE:SSM_INNER + (g + 1) * SSM_STATE].astype(BF16)
            c_g = xact[rows, SSM_INNER + BC_COLS + g * SSM_STATE:
                       SSM_INNER + BC_COLS + (g + 1) * SSM_STATE].astype(BF16)
            cb2 = _dot_nt(c_g, jnp.concatenate([b_g, b_g], axis=0))
            st_g = st[:, gs]
            y_inter = _dot(c_g, st_g.astype(BF16))
            xw, decay = [], []
            for j in range(GROUP_COLS // LANES):
                pair = g * (GROUP_COLS // LANES) + j
                ps = slice(pair * LANES, (pair + 1) * LANES)
                a_col = jnp.take_along_axis(acum, pair_of_lane + 2 * pair, axis=1)
                a_row = jnp.sum(jnp.where(diag2, a_col, 0.0), axis=0, keepdims=True)
                a_last = a_col[CHUNK - 1:CHUNK, :]
                dec = jnp.where(causal2, jnp.exp(jnp.minimum(a_col - a_row, 0.0)), 0.0)
                wts = (cb2 * dec).astype(BF16)
                xp = xdt[rows, ps]
                xbd = jnp.concatenate([jnp.where(left, xp, 0.0), jnp.where(left, 0.0, xp)],
                                      axis=0).astype(BF16)
                ysc[rows, ps] = _dot(wts, xbd) + y_inter[:, j * LANES:(j + 1) * LANES] * jnp.exp(a_col)
                xw.append((jnp.exp(a_last - a_col) * xp).astype(BF16))
                decay.append(jnp.exp(a_last))
            st[:, gs] = jnp.concatenate(decay, axis=1) * st_g + _dot_tn(b_g, jnp.concatenate(xw, axis=1))

    xs = xact[:, :SSM_INNER]
    y = ysc[...] + dskip_ref[...] * xs
    yz = y * _silu(_dot(xn, wz_ref[...]))
    parts = []
    for g in range(SSM_GROUPS):
        blk = yz[:, g * GROUP_COLS:(g + 1) * GROUP_COLS]
        parts.append(blk * lax.rsqrt(jnp.mean(blk * blk, axis=-1, keepdims=True) + EPS))
    yn = (jnp.concatenate(parts, axis=1) * gssm_ref[...]).astype(BF16)
    ya = _dot(yn, wbr_ref[...])
    ma_ref[...] = _sigmoid(_dot(xn, wga_ref[...])) * ya

    @pl.when(ti == nt - 1)
    def _():
        convn_ref[...] = tail
        ssmn_ref[...] = st[...].T.reshape(SSM_HEADS, SSM_HEAD_DIM, SSM_STATE)


def _chunk_tri(tt):
    return jnp.asarray(np.kron(np.eye(tt // CHUNK), np.tril(np.ones((CHUNK, CHUNK)))), BF16)


def _ssd_mixer(x, conv0, ssm0, p, tt):
    b, t, d = x.shape
    grid = (b, t // tt)
    row_spec = pl.BlockSpec((None, tt, d), lambda i, j: (i, j, 0))
    consts = [p['g_mix'], p['w_xbc'], p['w_z'], p['w_dt'], p['w_ga'], p['conv_w'], p['conv_b'], p['dt_bias'],
              p['a_log'], p['d_skip'], p['g_ssm'], p['w_ssm_branch'], _chunk_tri(tt), p['expand']]
    return pl.pallas_call(
        _ssd_kernel,
        grid=grid,
        in_specs=[row_spec,
                  pl.BlockSpec((None, CONV_W - 1, CONV_CH), lambda i, j: (i, 0, 0)),
                  pl.BlockSpec((None, SSM_HEADS, SSM_HEAD_DIM, SSM_STATE), lambda i, j: (i, 0, 0, 0))]
                 + [_const_spec(c.shape) for c in consts],
        out_specs=[row_spec,
                   pl.BlockSpec((None, CONV_W - 1, CONV_CH), lambda i, j: (i, 0, 0)),
                   pl.BlockSpec((None, SSM_HEADS, SSM_HEAD_DIM, SSM_STATE), lambda i, j: (i, 0, 0, 0))],
        out_shape=[jax.ShapeDtypeStruct((b, t, d), F32),
                   jax.ShapeDtypeStruct((b, CONV_W - 1, CONV_CH), F32),
                   jax.ShapeDtypeStruct((b, SSM_HEADS, SSM_HEAD_DIM, SSM_STATE), F32)],
        scratch_shapes=[pltpu.VMEM((CONV_PAD + tt, CONV_CH), F32),
                        pltpu.VMEM((tt, CONV_CH), F32),
                        pltpu.VMEM((tt, SSM_INNER), F32),
                        pltpu.VMEM((tt, SSM_INNER), F32),
                        pltpu.VMEM((SSM_STATE, SSM_INNER), F32)],
        compiler_params=pltpu.CompilerParams(dimension_semantics=("arbitrary", "arbitrary"),
                                             vmem_limit_bytes=VMEM_LIMIT),
        name="ssd_mixer",
    )(x, conv0, ssm0, *consts)


def _hgrn_kernel(x_ref, hg0_ref, g_ref, wh_ref, wgb_ref, lbl_ref, ghg_ref, wbr_ref, tri_ref,
                 mb_ref, hgn_ref,
                 osc, st):
    tt = x_ref.shape[0]
    n_chunks = tt // CHUNK
    ti = pl.program_id(1)
    nt = pl.num_programs(1)

    @pl.when(ti == 0)
    def _():
        for h in range(HG_HEADS):
            st[h] = hg0_ref[h].T

    xn = _rmsnorm(x_ref[...], g_ref[...]).astype(BF16)
    proj = _dot(xn, wh_ref[...])
    l0 = lbl_ref[0:1, :]
    l1 = lbl_ref[1:2, :]
    lmax = jnp.maximum(l0, l1)
    e0 = jnp.exp(l0 - lmax)
    lb = e0 / (e0 + jnp.exp(l1 - lmax))
    fr = proj[:, D_MODEL:2 * D_MODEL]
    kk = (1.0 - lb) * _sigmoid(-fr)
    v = _silu(proj[:, 2 * D_MODEL:3 * D_MODEL]).astype(BF16)
    gc = _exact_dot_lhs01(tri_ref[...], jnp.log(lb + (1.0 - lb) * _sigmoid(fr)))
    qh = (proj[:, :D_MODEL] * jnp.exp(gc)).astype(BF16)
    kh = (kk * jnp.exp(-gc)).astype(BF16)
    kt, decay = [], []
    for c in range(n_chunks):
        rows = slice(c * CHUNK, (c + 1) * CHUNK)
        glast = gc[(c + 1) * CHUNK - 1:(c + 1) * CHUNK, :]
        kt.append((kk[rows, :] * jnp.exp(glast - gc[rows, :])).astype(BF16))
        decay.append(jnp.exp(glast))

    row = lax.broadcasted_iota(jnp.int32, (tt, tt), 0)
    col = lax.broadcasted_iota(jnp.int32, (tt, tt), 1)
    causal = jnp.logical_and(row >= col, row // CHUNK == col // CHUNK)

    for h in range(HG_HEADS):
        hs = slice(h * HG_K, (h + 1) * HG_K)
        att = jnp.where(causal, _dot_nt(qh[:, hs], kh[:, hs]), 0.0).astype(BF16)
        o_intra = _dot(att, v[:, hs])
        s = st[h]
        o_inter = []
        for c in range(n_chunks):
            rows = slice(c * CHUNK, (c + 1) * CHUNK)
            o_inter.append(_dot_nt(qh[rows, hs], s.astype(BF16)))
            s = decay[c][:, hs] * s + _dot_tn(v[rows, hs], kt[c][:, hs])
        st[h] = s
        osc[:, hs] = o_intra + jnp.concatenate(o_inter, axis=0)

    parts = []
    for h in range(HG_HEADS):
        blk = osc[:, h * HG_K:(h + 1) * HG_K]
        parts.append(blk * lax.rsqrt(jnp.mean(blk * blk, axis=-1, keepdims=True) + EPS))
    on = (jnp.concatenate(parts, axis=1) * ghg_ref[...] * _silu(proj[:, 3 * D_MODEL:])).astype(BF16)
    yb = _dot(on, wbr_ref[...])
    mb_ref[...] = _sigmoid(_dot(xn, wgb_ref[...])) * yb

    @pl.when(ti == nt - 1)
    def _():
        for h in range(HG_HEADS):
            hgn_ref[h] = st[h].T


def _hgrn_mixer(x, hg0, p, tt):
    b, t, d = x.shape
    grid = (b, t // tt)
    row_spec = pl.BlockSpec((None, tt, d), lambda i, j: (i, j, 0))
    st_spec = pl.BlockSpec((None, HG_HEADS, HG_K, HG_K), lambda i, j: (i, 0, 0, 0))
    consts = [p['g_mix'], p['w_hg'], p['w_gb'], p['lb_logits'], p['g_hgrn'], p['w_hgrn_branch'], _chunk_tri(tt)]
    return pl.pallas_call(
        _hgrn_kernel,
        grid=grid,
        in_specs=[row_spec, st_spec] + [_const_spec(c.shape) for c in consts],
        out_specs=[row_spec, st_spec],
        out_shape=[jax.ShapeDtypeStruct((b, t, d), F32),
                   jax.ShapeDtypeStruct((b, HG_HEADS, HG_K, HG_K), F32)],
        scratch_shapes=[pltpu.VMEM((tt, d), F32), pltpu.VMEM((HG_HEADS, HG_K, HG_K), F32)],
        compiler_params=pltpu.CompilerParams(dimension_semantics=("arbitrary", "arbitrary"),
                                             vmem_limit_bytes=VMEM_LIMIT),
        name="hgrn_mixer",
    )(x, hg0, *consts)


def _memkv_kernel(m_ref, g_ref, wk_ref, wv_ref, k_ref, v_ref):
    mn = _rmsnorm(m_ref[...], g_ref[...]).astype(BF16)
    k_ref[...] = _dot(mn, wk_ref[...])
    v_ref[...] = _dot(mn, wv_ref[...])


def _memory_kv(mem, p):
    b, n, d = mem.shape
    spec = pl.BlockSpec((None, n, d), lambda i: (i, 0, 0))
    consts = [p['g_mem'], p['w_xk'], p['w_xv']]
    return pl.pallas_call(
        _memkv_kernel,
        grid=(b,),
        in_specs=[spec] + [_const_spec(c.shape) for c in consts],
        out_specs=[spec, spec],
        out_shape=[jax.ShapeDtypeStruct((b, n, d), F32)] * 2,
        compiler_params=pltpu.CompilerParams(dimension_semantics=("arbitrary",),
                                             vmem_limit_bytes=VMEM_LIMIT),
        name="memory_kv",
    )(mem, *consts)


def _post_kernel(x_ref, ma_ref, mb_ref, mk_ref, mv_ref, cnt0_ref, wo_ref, gx_ref, wq_ref, wxo_ref,
                 gmoe_ref, wr_ref, br_ref, tril_ref,
                 h_ref, xn_ref, meta_ref, gate_ref, cnt_ref,
                 base):
    tm = x_ref.shape[0]
    first = jnp.logical_and(pl.program_id(0) == 0, pl.program_id(1) == 0)

    @pl.when(first)
    def _():
        base[...] = cnt0_ref[...]

    m = (ma_ref[...] + mb_ref[...]).astype(BF16)
    h1 = x_ref[...] + _dot(m, wo_ref[...])

    hn = _rmsnorm(h1, gx_ref[...]).astype(BF16)
    q = _dot(hn, wq_ref[...])
    heads = []
    for hh in range(X_HEADS):
        hs = slice(hh * X_HEAD_DIM, (hh + 1) * X_HEAD_DIM)
        s = _dot_nt(q[:, hs].astype(BF16), mk_ref[:, hs].astype(BF16)) * (X_HEAD_DIM ** -0.5)
        s = s - jnp.max(s, axis=-1, keepdims=True)
        e = jnp.exp(s)
        pr = e / jnp.sum(e, axis=-1, keepdims=True)
        heads.append(_dot(pr.astype(BF16), mv_ref[:, hs].astype(BF16)))
    o = jnp.concatenate(heads, axis=1).astype(BF16)
    h2 = h1 + _dot(o, wxo_ref[...])
    h_ref[...] = h2

    xn3 = _rmsnorm(h2, gmoe_ref[...])
    xn_ref[...] = xn3
    logits = _dot(xn3.astype(BF16), wr_ref[...]) + br_ref[...]

    lane = lax.broadcasted_iota(jnp.int32, (tm, LANES), 1)
    lane_f = lane.astype(F32)
    run = logits
    vals, hots = [], []
    for _ in range(TOP_K):
        mx = jnp.max(run, axis=-1, keepdims=True)
        idx = jnp.min(jnp.where(run == mx, lane_f, float(LANES)), axis=-1, keepdims=True)
        hot = lane_f == idx
        run = jnp.where(hot, -jnp.inf, run)
        vals.append(mx)
        hots.append(hot)
    es = [jnp.exp(v - vals[0]) for v in vals]
    den = es[0] + es[1] + es[2] + es[3]

    tot = jnp.zeros((tm, LANES), F32)
    for hot in hots:
        tot = tot + hot.astype(F32)
    before = base[...] + _dot(tril_ref[...], tot.astype(BF16))
    base[...] = base[...] + jnp.sum(tot, axis=0, keepdims=True)
    cnt_ref[...] = base[...]

    meta = jnp.zeros((tm, LANES), F32)
    gates = jnp.zeros((tm, LANES), F32)
    for j in range(TOP_K):
        e_idx = jnp.sum(jnp.where(hots[j], lane_f, 0.0), axis=-1, keepdims=True)
        rank = jnp.sum(jnp.where(hots[j], before, 0.0), axis=-1, keepdims=True)
        meta = jnp.where(lane == j, e_idx, meta)
        meta = jnp.where(lane == TOP_K + j, rank, meta)
        gates = jnp.where(lane == j, es[j] / den, gates)
    meta_ref[...] = meta.astype(jnp.int32)
    gate_ref[...] = gates


def _post_mixer(x, ma, mb, mk, mv, cnt0, p, tm):
    b, t, d = x.shape
    grid = (b, t // tm)
    row_spec = pl.BlockSpec((None, tm, d), lambda i, j: (i, j, 0))
    small_spec = pl.BlockSpec((None, tm, LANES), lambda i, j: (i, j, 0))
    mem_spec = pl.BlockSpec((None, N_MEM, d), lambda i, j: (i, 0, 0))
    cnt_spec = pl.BlockSpec((1, LANES), lambda i, j: (0, 0))
    tril = jnp.tril(jnp.ones((tm, tm), BF16), -1)
    consts = [p['w_out'], p['g_xattn'], p['w_xq'], p['w_xo'], p['g_moe'], p['w_router'], p['b_router'], tril]
    return pl.pallas_call(
        _post_kernel,
        grid=grid,
        in_specs=[row_spec, row_spec, row_spec, mem_spec, mem_spec, cnt_spec]
                 + [_const_spec(c.shape) for c in consts],
        out_specs=[row_spec, row_spec, small_spec, small_spec, cnt_spec],
        out_shape=[jax.ShapeDtypeStruct((b, t, d), F32),
                   jax.ShapeDtypeStruct((b, t, d), F32),
                   jax.ShapeDtypeStruct((b, t, LANES), jnp.int32),
                   jax.ShapeDtypeStruct((b, t, LANES), F32),
                   jax.ShapeDtypeStruct((1, LANES), F32)],
        scratch_shapes=[pltpu.VMEM((1, LANES), F32)],
        compiler_params=pltpu.CompilerParams(dimension_semantics=("arbitrary", "arbitrary"),
                                             vmem_limit_bytes=VMEM_LIMIT),
        name="post_mixer",
    )(x, ma, mb, mk, mv, cnt0, *consts)


def _dispatch_kernel(dest_ref, x_ref, slots_ref, sem):
    tm = x_ref.shape[0]

    def row_copy(t, j):
        return pltpu.make_async_copy(x_ref.at[pl.ds(t, 1), :],
                                     slots_ref.at[pl.ds(dest_ref[t * TOP_K + j], 1), :], sem)

    def issue(t, carry):
        for j in range(TOP_K):
            row_copy(t, j).start()
        return carry

    lax.fori_loop(0, tm, issue, 0)

    def drain(t, carry):
        for j in range(TOP_K):
            row_copy(t, j).wait()
        return carry

    lax.fori_loop(0, tm, drain, 0)


def _dispatch(xn, dest, n_slots, tm):
    n, d = xn.shape
    return pl.pallas_call(
        _dispatch_kernel,
        grid=(n // tm,),
        in_specs=[pl.BlockSpec((tm * TOP_K,), lambda i: (i,), memory_space=pltpu.SMEM),
                  pl.BlockSpec((tm, d), lambda i: (i, 0))],
        out_specs=pl.BlockSpec(memory_space=pl.ANY),
        out_shape=jax.ShapeDtypeStruct((n_slots, d), F32),
        scratch_shapes=[pltpu.SemaphoreType.DMA(())],
        compiler_params=pltpu.CompilerParams(dimension_semantics=("arbitrary",),
                                             has_side_effects=True),
        name="moe_dispatch",
    )(dest, xn)


def _moe_kernel(be_ref, valid_ref, x_ref, wgu_ref, bgu_ref, wd_ref, bd_ref, y_ref):
    del be_ref
    row = lax.broadcasted_iota(jnp.int32, x_ref.shape, 0)
    xb = jnp.where(row < valid_ref[pl.program_id(0)], x_ref[...], 0.0).astype(BF16)
    gu = _dot(xb, wgu_ref[...]) + bgu_ref[...]
    gate = jnp.minimum(gu[:, :D_EXPERT], SWIGLU_LIMIT)
    up = jnp.clip(gu[:, D_EXPERT:], -SWIGLU_LIMIT, SWIGLU_LIMIT)
    hmid = ((up + 1.0) * gate * _sigmoid(SWIGLU_ALPHA * gate)).astype(BF16)
    y_ref[...] = _dot(hmid, wd_ref[...]) + bd_ref[...]


def _moe_experts(slots, block_e, block_valid, p):
    n_slots, d = slots.shape
    n_blocks = n_slots // MOE_ROWS
    grid_spec = pltpu.PrefetchScalarGridSpec(
        num_scalar_prefetch=2,
        grid=(n_blocks,),
        in_specs=[pl.BlockSpec((MOE_ROWS, d), lambda i, be, bv: (i, 0)),
                  pl.BlockSpec((None, d, 2 * D_EXPERT), lambda i, be, bv: (be[i], 0, 0)),
                  pl.BlockSpec((None, 1, 2 * D_EXPERT), lambda i, be, bv: (be[i], 0, 0)),
                  pl.BlockSpec((None, D_EXPERT, d), lambda i, be, bv: (be[i], 0, 0)),
                  pl.BlockSpec((None, 1, d), lambda i, be, bv: (be[i], 0, 0))],
        out_specs=pl.BlockSpec((MOE_ROWS, d), lambda i, be, bv: (i, 0)),
    )
    return pl.pallas_call(
        _moe_kernel,
        grid_spec=grid_spec,
        out_shape=jax.ShapeDtypeStruct((n_slots, d), F32),
        compiler_params=pltpu.CompilerParams(dimension_semantics=("arbitrary",),
                                             vmem_limit_bytes=VMEM_LIMIT),
        name="moe_experts",
    )(block_e, block_valid, slots, p['w_gate_up'], p['b_gate_up'], p['w_down'], p['b_down'])


def _combine_kernel(dest_ref, h_ref, gate_ref, gfin_ref, y_hbm, out_ref, buf, sem):
    tm = h_ref.shape[0]

    def row_copy(t, j):
        return pltpu.make_async_copy(y_hbm.at[pl.ds(dest_ref[t * TOP_K + j], 1), :],
                                     buf.at[j, pl.ds(t, 1), :], sem)

    def issue(t, carry):
        for j in range(TOP_K):
            row_copy(t, j).start()
        return carry

    lax.fori_loop(0, tm, issue, 0)

    def drain(t, carry):
        for j in range(TOP_K):
            row_copy(t, j).wait()
        return carry

    lax.fori_loop(0, tm, drain, 0)

    acc = h_ref[...]
    gates = gate_ref[...]
    for j in range(TOP_K):
        acc = acc + gates[:, j:j + 1] * buf[j]
    out_ref[...] = _rmsnorm(acc, gfin_ref[...])


def _combine(h, gates, dest, y_slots, g_final, tm):
    n, d = h.shape
    return pl.pallas_call(
        _combine_kernel,
        grid=(n // tm,),
        in_specs=[pl.BlockSpec((tm * TOP_K,), lambda i: (i,), memory_space=pltpu.SMEM),
                  pl.BlockSpec((tm, d), lambda i: (i, 0)),
                  pl.BlockSpec((tm, LANES), lambda i: (i, 0)),
                  _const_spec(g_final.shape),
                  pl.BlockSpec(memory_space=pl.ANY)],
        out_specs=pl.BlockSpec((tm, d), lambda i: (i, 0)),
        out_shape=jax.ShapeDtypeStruct((n, d), F32),
        scratch_shapes=[pltpu.VMEM((TOP_K, tm, d), F32), pltpu.SemaphoreType.DMA(())],
        compiler_params=pltpu.CompilerParams(dimension_semantics=("arbitrary",),
                                             vmem_limit_bytes=VMEM_LIMIT),
        name="moe_combine",
    )(dest, h, gates, g_final, y_slots)


def _moe(h, xn, meta, gates, counts, p, g_final):
    n, d = h.shape
    n_rows = n * TOP_K
    n_blocks = n_rows // MOE_ROWS + N_EXPERTS
    n_slots = n_blocks * MOE_ROWS
    cnt = counts[0, :N_EXPERTS].astype(jnp.int32)
    padded = (cnt + MOE_ROWS - 1) // MOE_ROWS * MOE_ROWS
    pad_end = jnp.cumsum(padded)
    pad_start = pad_end - padded
    block_row0 = jnp.arange(n_blocks, dtype=jnp.int32) * MOE_ROWS
    block_e = jnp.minimum(jnp.sum((pad_end[None, :] <= block_row0[:, None]).astype(jnp.int32), axis=1),
                          N_EXPERTS - 1)
    block_valid = jnp.clip((pad_start + cnt)[block_e] - block_row0, 0, MOE_ROWS)
    dest = (pad_start[meta[:, :TOP_K]] + meta[:, TOP_K:2 * TOP_K]).reshape(-1).astype(jnp.int32)
    slots = _dispatch(xn, dest, n_slots, min(DISPATCH_TILE, n))
    y_slots = _moe_experts(slots, block_e, block_valid, p)
    return _combine(h, gates, dest, y_slots, g_final, min(COMBINE_TILE, n))


def _prepare(g_mix, w_in, conv_w, conv_b, dt_bias, a_log, d_skip, g_ssm, w_ssm_branch, lb_logits, g_hgrn,
             w_hgrn_branch, w_out, g_mem, w_xk, w_xv, g_xattn, w_xq, w_xo, g_moe, w_router, b_router,
             w_gate_up, b_gate_up, w_down, b_down):
    d = D_MODEL
    w = w_in[0]
    o_z, o_xbc = 0, SSM_INNER
    o_dt = o_xbc + CONV_CH
    o_q = o_dt + SSM_HEADS
    o_ga = o_q + 4 * d
    o_gb = o_ga + d

    def row(v):
        return v.reshape(1, -1).astype(F32)

    def lane_pad(v, fill=0.0):
        return jnp.pad(v.reshape(1, -1).astype(F32), ((0, 0), (0, LANES - v.shape[-1])), constant_values=fill)

    head_of_col = np.arange(SSM_INNER) // SSM_HEAD_DIM
    expand = (np.arange(LANES)[:, None] == head_of_col[None, :])
    return dict(
        g_mix=row(g_mix[0]),
        w_z=w[:, o_z:o_z + SSM_INNER].astype(BF16),
        w_xbc=w[:, o_xbc:o_xbc + CONV_CH].astype(BF16),
        w_dt=jnp.pad(w[:, o_dt:o_dt + SSM_HEADS], ((0, 0), (0, LANES - SSM_HEADS))).astype(BF16),
        w_hg=w[:, o_q:o_q + 4 * d].astype(BF16),
        w_ga=w[:, o_ga:o_ga + d].astype(BF16),
        w_gb=w[:, o_gb:o_gb + d].astype(BF16),
        conv_w=conv_w[0].astype(F32),
        conv_b=row(conv_b[0]),
        dt_bias=lane_pad(dt_bias[0]),
        a_log=lane_pad(a_log[0]),
        d_skip=row(jnp.repeat(d_skip[0], SSM_HEAD_DIM)),
        g_ssm=row(g_ssm[0]),
        w_ssm_branch=w_ssm_branch[0].astype(BF16),
        lb_logits=lb_logits.astype(F32),
        g_hgrn=row(g_hgrn[0]),
        w_hgrn_branch=w_hgrn_branch[0].astype(BF16),
        w_out=w_out[0].astype(BF16),
        g_mem=row(g_mem[0]),
        w_xk=w_xk[0].astype(BF16),
        w_xv=w_xv[0].astype(BF16),
        g_xattn=row(g_xattn[0]),
        w_xq=w_xq[0].astype(BF16),
        w_xo=w_xo[0].astype(BF16),
        g_moe=row(g_moe[0]),
        w_router=jnp.pad(w_router[0], ((0, 0), (0, LANES - N_EXPERTS))).astype(BF16),
        b_router=lane_pad(b_router[0], fill=-jnp.inf),
        w_gate_up=w_gate_up[0].astype(BF16),
        b_gate_up=b_gate_up[0].reshape(N_EXPERTS, 1, 2 * D_EXPERT).astype(F32),
        w_down=w_down[0].astype(BF16),
        b_down=b_down[0].reshape(N_EXPERTS, 1, d).astype(F32),
        expand=jnp.asarray(expand, BF16),
    )


def _group_step(x, conv0, ssm0, hg0, mk, mv, p, g_final):
    b, t, d = x.shape
    tt = min(MIXER_TILE, t)
    ma, conv_n, ssm_n = _ssd_mixer(x, conv0, ssm0, p, tt)
    mb, hg_n = _hgrn_mixer(x, hg0, p, tt)
    tm = min(POST_TILE, t)
    cnt0 = jnp.zeros((1, LANES), F32)
    h2, xn3, meta, gates, counts = _post_mixer(x, ma, mb, mk, mv, cnt0, p, tm)
    n = b * t
    y = _moe(h2.reshape(n, d), xn3.reshape(n, d), meta.reshape(n, LANES), gates.reshape(n, LANES),
             counts, p, g_final)
    return y.reshape(b, t, d), conv_n, ssm_n, hg_n


def kernel(x_prompt, x_sample, mem_prompt, state_conv, state_ssm, state_hgrn, cache_mem_k, cache_mem_v, g_mix, w_in, conv_w, conv_b, dt_bias, a_log, d_skip, g_ssm, w_ssm_branch, lb_logits, g_hgrn, w_hgrn_branch, w_out, g_mem, w_xk, w_xv, g_xattn, w_xq, w_xo, g_moe, w_router, b_router, w_gate_up, b_gate_up, w_down, b_down, g_final):
    p = _prepare(g_mix, w_in, conv_w, conv_b, dt_bias, a_log, d_skip, g_ssm, w_ssm_branch, lb_logits, g_hgrn,
                 w_hgrn_branch, w_out, g_mem, w_xk, w_xv, g_xattn, w_xq, w_xo, g_moe, w_router, b_router,
                 w_gate_up, b_gate_up, w_down, b_down)
    g_fin = g_final.reshape(1, -1).astype(F32)
    bp = x_prompt.shape[0]
    bs = x_sample.shape[0]
    d = D_MODEL

    mk_p, mv_p = _memory_kv(mem_prompt, p)
    y_p, conv_p, ssm_p, hg_p = _group_step(
        x_prompt,
        jnp.zeros((bp, CONV_W - 1, CONV_CH), F32),
        jnp.zeros((bp, SSM_HEADS, SSM_HEAD_DIM, SSM_STATE), F32),
        jnp.zeros((bp, HG_HEADS, HG_K, HG_K), F32),
        mk_p, mv_p, p, g_fin)
    y_s, conv_s, ssm_s, hg_s = _group_step(
        x_sample, state_conv[0], state_ssm[0], state_hgrn[0],
        cache_mem_k[0].reshape(bs, N_MEM, d), cache_mem_v[0].reshape(bs, N_MEM, d), p, g_fin)

    kv_shape = (1, bp, N_MEM, X_HEADS, X_HEAD_DIM)
    return (y_p, y_s,
            conv_p[None], ssm_p[None], hg_p[None], mk_p.reshape(kv_shape), mv_p.reshape(kv_shape),
            conv_s[None], ssm_s[None], hg_s[None])
```

```python
import jax
import jax.numpy as jnp
import numpy as np
from jax import lax
from jax.experimental import pallas as pl
from jax.experimental.pallas import tpu as pltpu

F32 = jnp.float32
BF16 = jnp.bfloat16

D_MODEL = 1024
CHUNK = 64
EPS = 1e-6
SSM_INNER = 2 * D_MODEL
SSM_HEAD_DIM = 64
SSM_HEADS = SSM_INNER // SSM_HEAD_DIM
SSM_GROUPS = 4
SSM_STATE = 128
GROUP_COLS = SSM_INNER // SSM_GROUPS
CONV_W = 4
BC_COLS = SSM_GROUPS * SSM_STATE
CONV_CH = SSM_INNER + 2 * BC_COLS
HG_HEADS = 8
HG_K = D_MODEL // HG_HEADS
N_MEM = 256
X_HEADS = 4
X_HEAD_DIM = D_MODEL // X_HEADS
N_EXPERTS = 32
TOP_K = 4
D_EXPERT = D_MODEL
SWIGLU_LIMIT = 7.0
SWIGLU_ALPHA = 1.702

LANES = 128
CONV_PAD = 8
MOE_ROWS = 256
MIXER_TILE = 256
POST_TILE = 512
DISPATCH_TILE = 512
COMBINE_TILE = 256
VMEM_LIMIT = 56 * 1024 * 1024
DMA_PRIORITIES = 2


def _const_spec(shape):
    nd = len(shape)
    return pl.BlockSpec(shape, lambda *_: (0,) * nd, pipeline_mode=pl.Buffered(1))


def _dot(a, b):
    return jnp.dot(a, b, preferred_element_type=F32)


def _dot_nt(a, b):
    return lax.dot_general(a, b, (((1,), (1,)), ((), ())), preferred_element_type=F32)


def _dot_tn(a, b):
    return lax.dot_general(a, b, (((0,), (0,)), ((), ())), preferred_element_type=F32)


def _split3(a):
    hi = a.astype(BF16)
    r1 = a - hi.astype(F32)
    mid = r1.astype(BF16)
    lo = (r1 - mid.astype(F32)).astype(BF16)
    return hi, mid, lo


def _exact_dot_lhs01(sel, a):
    hi, mid, lo = _split3(a)
    return _dot(sel, hi) + _dot(sel, mid) + _dot(sel, lo)


def _rmsnorm(x, g):
    return x * lax.rsqrt(jnp.mean(x * x, axis=-1, keepdims=True) + EPS) * g


def _sigmoid(x):
    return jax.nn.sigmoid(x)


def _silu(x):
    return x * jax.nn.sigmoid(x)


def _softplus(x):
    return jnp.maximum(x, 0.0) + jnp.log1p(jnp.exp(-jnp.abs(x)))


def _ssd_kernel(x_ref, conv0_ref, ssm0_ref, g_ref, wxbc_ref, wz_ref, wdt_ref, wga_ref, convw_ref,
                convb_ref, dtb_ref, alog_ref, dskip_ref, gssm_ref, wbr_ref, tri_ref, expand_ref,
                ma_ref, convn_ref, ssmn_ref,
                xpad, xact, xdt, ysc, st):
    tt = x_ref.shape[0]
    ti = pl.program_id(1)
    nt = pl.num_programs(1)

    @pl.when(ti == 0)
    def _():
        xpad[CONV_PAD - (CONV_W - 1):CONV_PAD, :] = conv0_ref[...]
        st[...] = ssm0_ref[...].reshape(SSM_INNER, SSM_STATE).T

    xn = _rmsnorm(x_ref[...], g_ref[...]).astype(BF16)
    xpad[CONV_PAD:CONV_PAD + tt, :] = _dot(xn, wxbc_ref[...])

    conv = convb_ref[...] + xpad[CONV_PAD:CONV_PAD + tt, :] * convw_ref[CONV_W - 1:CONV_W, :]
    for k in range(1, CONV_W):
        conv = conv + xpad[CONV_PAD - k:CONV_PAD - k + tt, :] * convw_ref[CONV_W - 1 - k:CONV_W - k, :]
    xact[...] = _silu(conv)
    tail = xpad[CONV_PAD + tt - (CONV_W - 1):CONV_PAD + tt, :]
    xpad[CONV_PAD - (CONV_W - 1):CONV_PAD, :] = tail

    dt = _softplus(_dot(xn, wdt_ref[...]) + dtb_ref[...])
    acum_all = _exact_dot_lhs01(tri_ref[...], dt * -jnp.exp(alog_ref[...]))
    dt_hi = dt.astype(BF16)
    dt_lo = (dt - dt_hi.astype(F32)).astype(BF16)
    dt_x = _dot(dt_hi, expand_ref[...]) + _dot(dt_lo, expand_ref[...])
    xdt[...] = xact[:, :SSM_INNER] * dt_x

    lane = lax.broadcasted_iota(jnp.int32, (CHUNK, LANES), 1)
    row = lax.broadcasted_iota(jnp.int32, (CHUNK, LANES), 0)
    causal2 = row >= (lane % CHUNK)
    diag2 = row == (lane % CHUNK)
    left = lane < CHUNK
    pair_of_lane = lane // CHUNK

    for c in range(tt // CHUNK):
        rows = slice(c * CHUNK, (c + 1) * CHUNK)
        acum = acum_all[rows, :]
        for g in range(SSM_GROUPS):
            gs = slice(g * GROUP_COLS, (g + 1) * GROUP_COLS)
            b_g = xact[rows, SSM_INNER + g * SSM_STATE:SSM_INNER + (g + 1) * SSM_STATE].astype(BF16)
            c_g = xact[rows, SSM_INNER + BC_COLS + g * SSM_STATE:
                       SSM_INNER + BC_COLS + (g + 1) * SSM_STATE].astype(BF16)
            cb2 = _dot_nt(c_g, jnp.concatenate([b_g, b_g], axis=0))
            st_g = st[:, gs]
            y_inter = _dot(c_g, st_g.astype(BF16))
            xw, decay = [], []
            for j in range(GROUP_COLS // LANES):
                pair = g * (GROUP_COLS // LANES) + j
                ps = slice(pair * LANES, (pair + 1) * LANES)
                a_col = jnp.take_along_axis(acum, pair_of_lane + 2 * pair, axis=1)
                a_row = jnp.sum(jnp.where(diag2, a_col, 0.0), axis=0, keepdims=True)
                a_last = a_col[CHUNK - 1:CHUNK, :]
                dec = jnp.where(causal2, jnp.exp(jnp.minimum(a_col - a_row, 0.0)), 0.0)
                wts = (cb2 * dec).astype(BF16)
                xp = xdt[rows, ps]
                xbd = jnp.concatenate([jnp.where(left, xp, 0.0), jnp.where(left, 0.0, xp)],
                                      axis=0).astype(BF16)
                ysc[rows, ps] = _dot(wts, xbd) + y_inter[:, j * LANES:(j + 1) * LANES] * jnp.exp(a_col)
                xw.append((jnp.exp(a_last - a_col) * xp).astype(BF16))
                decay.append(jnp.exp(a_last))
            st[:, gs] = jnp.concatenate(decay, axis=1) * st_g + _dot_tn(b_g, jnp.concatenate(xw, axis=1))

    xs = xact[:, :SSM_INNER]
    y = ysc[...] + dskip_ref[...] * xs
    yz = y * _silu(_dot(xn, wz_ref[...]))
    parts = []
    for g in range(SSM_GROUPS):
        blk = yz[:, g * GROUP_COLS:(g + 1) * GROUP_COLS]
        parts.append(blk * lax.rsqrt(jnp.mean(blk * blk, axis=-1, keepdims=True) + EPS))
    yn = (jnp.concatenate(parts, axis=1) * gssm_ref[...]).astype(BF16)
    ya = _dot(yn, wbr_ref[...])
    ma_ref[...] = _sigmoid(_dot(xn, wga_ref[...])) * ya

    @pl.when(ti == nt - 1)
    def _():
        convn_ref[...] = tail
        ssmn_ref[...] = st[...].T.reshape(SSM_HEADS, SSM_HEAD_DIM, SSM_STATE)


def _chunk_tri(tt):
    return jnp.asarray(np.kron(np.eye(tt // CHUNK), np.tril(np.ones((CHUNK, CHUNK)))), BF16)


def _ssd_mixer(x, conv0, ssm0, p, tt):
    b, t, d = x.shape
    grid = (b, t // tt)
    row_spec = pl.BlockSpec((None, tt, d), lambda i, j: (i, j, 0))
    consts = [p['g_mix'], p['w_xbc'], p['w_z'], p['w_dt'], p['w_ga'], p['conv_w'], p['conv_b'], p['dt_bias'],
              p['a_log'], p['d_skip'], p['g_ssm'], p['w_ssm_branch'], _chunk_tri(tt), p['expand']]
    return pl.pallas_call(
        _ssd_kernel,
        grid=grid,
        in_specs=[row_spec,
                  pl.BlockSpec((None, CONV_W - 1, CONV_CH), lambda i, j: (i, 0, 0)),
                  pl.BlockSpec((None, SSM_HEADS, SSM_HEAD_DIM, SSM_STATE), lambda i, j: (i, 0, 0, 0))]
                 + [_const_spec(c.shape) for c in consts],
        out_specs=[row_spec,
                   pl.BlockSpec((None, CONV_W - 1, CONV_CH), lambda i, j: (i, 0, 0)),
                   pl.BlockSpec((None, SSM_HEADS, SSM_HEAD_DIM, SSM_STATE), lambda i, j: (i, 0, 0, 0))],
        out_shape=[jax.ShapeDtypeStruct((b, t, d), F32),
                   jax.ShapeDtypeStruct((b, CONV_W - 1, CONV_CH), F32),
                   jax.ShapeDtypeStruct((b, SSM_HEADS, SSM_HEAD_DIM, SSM_STATE), F32)],
        scratch_shapes=[pltpu.VMEM((CONV_PAD + tt, CONV_CH), F32),
                        pltpu.VMEM((tt, CONV_CH), F32),
                        pltpu.VMEM((tt, SSM_INNER), F32),
                        pltpu.VMEM((tt, SSM_INNER), F32),
                        pltpu.VMEM((SSM_STATE, SSM_INNER), F32)],
        compiler_params=pltpu.CompilerParams(dimension_semantics=("arbitrary", "arbitrary"),
                                             vmem_limit_bytes=VMEM_LIMIT),
        name="ssd_mixer",
    )(x, conv0, ssm0, *consts)


def _hgrn_kernel(x_ref, hg0_ref, g_ref, wh_ref, wgb_ref, lbl_ref, ghg_ref, wbr_ref, tri_ref,
                 mb_ref, hgn_ref,
                 osc, st):
    tt = x_ref.shape[0]
    n_chunks = tt // CHUNK
    ti = pl.program_id(1)
    nt = pl.num_programs(1)

    @pl.when(ti == 0)
    def _():
        for h in range(HG_HEADS):
            st[h] = hg0_ref[h].T

    xn = _rmsnorm(x_ref[...], g_ref[...]).astype(BF16)
    proj = _dot(xn, wh_ref[...])
    l0 = lbl_ref[0:1, :]
    l1 = lbl_ref[1:2, :]
    lmax = jnp.maximum(l0, l1)
    e0 = jnp.exp(l0 - lmax)
    lb = e0 / (e0 + jnp.exp(l1 - lmax))
    fr = proj[:, D_MODEL:2 * D_MODEL]
    kk = (1.0 - lb) * _sigmoid(-fr)
    v = _silu(proj[:, 2 * D_MODEL:3 * D_MODEL]).astype(BF16)
    gc = _exact_dot_lhs01(tri_ref[...], jnp.log(lb + (1.0 - lb) * _sigmoid(fr)))
    qh = (proj[:, :D_MODEL] * jnp.exp(gc)).astype(BF16)
    kh = (kk * jnp.exp(-gc)).astype(BF16)
    kt, decay = [], []
    for c in range(n_chunks):
        rows = slice(c * CHUNK, (c + 1) * CHUNK)
        glast = gc[(c + 1) * CHUNK - 1:(c + 1) * CHUNK, :]
        kt.append((kk[rows, :] * jnp.exp(glast - gc[rows, :])).astype(BF16))
        decay.append(jnp.exp(glast))

    row = lax.broadcasted_iota(jnp.int32, (tt, tt), 0)
    col = lax.broadcasted_iota(jnp.int32, (tt, tt), 1)
    causal = jnp.logical_and(row >= col, row // CHUNK == col // CHUNK)

    for h in range(HG_HEADS):
        hs = slice(h * HG_K, (h + 1) * HG_K)
        att = jnp.where(causal, _dot_nt(qh[:, hs], kh[:, hs]), 0.0).astype(BF16)
        o_intra = _dot(att, v[:, hs])
        s = st[h]
        o_inter = []
        for c in range(n_chunks):
            rows = slice(c * CHUNK, (c + 1) * CHUNK)
            o_inter.append(_dot_nt(qh[rows, hs], s.astype(BF16)))
            s = decay[c][:, hs] * s + _dot_tn(v[rows, hs], kt[c][:, hs])
        st[h] = s
        osc[:, hs] = o_intra + jnp.concatenate(o_inter, axis=0)

    parts = []
    for h in range(HG_HEADS):
        blk = osc[:, h * HG_K:(h + 1) * HG_K]
        parts.append(blk * lax.rsqrt(jnp.mean(blk * blk, axis=-1, keepdims=True) + EPS))
    on = (jnp.concatenate(parts, axis=1) * ghg_ref[...] * _silu(proj[:, 3 * D_MODEL:])).astype(BF16)
    yb = _dot(on, wbr_ref[...])
    mb_ref[...] = _sigmoid(_dot(xn, wgb_ref[...])) * yb

    @pl.when(ti == nt - 1)
    def _():
        for h in range(HG_HEADS):
            hgn_ref[h] = st[h].T


def _hgrn_mixer(x, hg0, p, tt):
    b, t, d = x.shape
    grid = (b, t // tt)
    row_spec = pl.BlockSpec((None, tt, d), lambda i, j: (i, j, 0))
    st_spec = pl.BlockSpec((None, HG_HEADS, HG_K, HG_K), lambda i, j: (i, 0, 0, 0))
    consts = [p['g_mix'], p['w_hg'], p['w_gb'], p['lb_logits'], p['g_hgrn'], p['w_hgrn_branch'], _chunk_tri(tt)]
    return pl.pallas_call(
        _hgrn_kernel,
        grid=grid,
        in_specs=[row_spec, st_spec] + [_const_spec(c.shape) for c in consts],
        out_specs=[row_spec, st_spec],
        out_shape=[jax.ShapeDtypeStruct((b, t, d), F32),
                   jax.ShapeDtypeStruct((b, HG_HEADS, HG_K, HG_K), F32)],
        scratch_shapes=[pltpu.VMEM((tt, d), F32), pltpu.VMEM((HG_HEADS, HG_K, HG_K), F32)],
        compiler_params=pltpu.CompilerParams(dimension_semantics=("arbitrary", "arbitrary"),
                                             vmem_limit_bytes=VMEM_LIMIT),
        name="hgrn_mixer",
    )(x, hg0, *consts)


def _memkv_kernel(m_ref, g_ref, wk_ref, wv_ref, k_ref, v_ref):
    mn = _rmsnorm(m_ref[...], g_ref[...]).astype(BF16)
    k_ref[...] = _dot(mn, wk_ref[...])
    v_ref[...] = _dot(mn, wv_ref[...])


def _memory_kv(mem, p):
    b, n, d = mem.shape
    spec = pl.BlockSpec((None, n, d), lambda i: (i, 0, 0))
    consts = [p['g_mem'], p['w_xk'], p['w_xv']]
    return pl.pallas_call(
        _memkv_kernel,
        grid=(b,),
        in_specs=[spec] + [_const_spec(c.shape) for c in consts],
        out_specs=[spec, spec],
        out_shape=[jax.ShapeDtypeStruct((b, n, d), F32)] * 2,
        compiler_params=pltpu.CompilerParams(dimension_semantics=("arbitrary",),
                                             vmem_limit_bytes=VMEM_LIMIT),
        name="memory_kv",
    )(mem, *consts)


def _post_kernel(x_ref, ma_ref, mb_ref, mk_ref, mv_ref, cnt0_ref, wo_ref, gx_ref, wq_ref, wxo_ref,
                 gmoe_ref, wr_ref, br_ref, tril_ref,
                 h_ref, xn_ref, meta_ref, gate_ref, cnt_ref,
                 base):
    tm = x_ref.shape[0]
    first = jnp.logical_and(pl.program_id(0) == 0, pl.program_id(1) == 0)

    @pl.when(first)
    def _():
        base[...] = cnt0_ref[...]

    m = (ma_ref[...] + mb_ref[...]).astype(BF16)
    h1 = x_ref[...] + _dot(m, wo_ref[...])

    hn = _rmsnorm(h1, gx_ref[...]).astype(BF16)
    q = _dot(hn, wq_ref[...])
    heads = []
    for hh in range(X_HEADS):
        hs = slice(hh * X_HEAD_DIM, (hh + 1) * X_HEAD_DIM)
        s = _dot_nt(q[:, hs].astype(BF16), mk_ref[:, hs].astype(BF16)) * (X_HEAD_DIM ** -0.5)
        s = s - jnp.max(s, axis=-1, keepdims=True)
        e = jnp.exp(s)
        pr = e / jnp.sum(e, axis=-1, keepdims=True)
        heads.append(_dot(pr.astype(BF16), mv_ref[:, hs].astype(BF16)))
    o = jnp.concatenate(heads, axis=1).astype(BF16)
    h2 = h1 + _dot(o, wxo_ref[...])
    h_ref[...] = h2

    xn3 = _rmsnorm(h2, gmoe_ref[...])
    xn_ref[...] = xn3
    logits = _dot(xn3.astype(BF16), wr_ref[...]) + br_ref[...]

    lane = lax.broadcasted_iota(jnp.int32, (tm, LANES), 1)
    lane_f = lane.astype(F32)
    run = logits
    vals, hots = [], []
    for _ in range(TOP_K):
        mx = jnp.max(run, axis=-1, keepdims=True)
        idx = jnp.min(jnp.where(run == mx, lane_f, float(LANES)), axis=-1, keepdims=True)
        hot = lane_f == idx
        run = jnp.where(hot, -jnp.inf, run)
        vals.append(mx)
        hots.append(hot)
    es = [jnp.exp(v - vals[0]) for v in vals]
    den = es[0] + es[1] + es[2] + es[3]

    tot = jnp.zeros((tm, LANES), F32)
    for hot in hots:
        tot = tot + hot.astype(F32)
    before = base[...] + _dot(tril_ref[...], tot.astype(BF16))
    base[...] = base[...] + jnp.sum(tot, axis=0, keepdims=True)
    cnt_ref[...] = base[...]

    meta = jnp.zeros((tm, LANES), F32)
    gates = jnp.zeros((tm, LANES), F32)
    for j in range(TOP_K):
        e_idx = jnp.sum(jnp.where(hots[j], lane_f, 0.0), axis=-1, keepdims=True)
        rank = jnp.sum(jnp.where(hots[j], before, 0.0), axis=-1, keepdims=True)
        meta = jnp.where(lane == j, e_idx, meta)
        meta = jnp.where(lane == TOP_K + j, rank, meta)
        gates = jnp.where(lane == j, es[j] / den, gates)
    meta_ref[...] = meta.astype(jnp.int32)
    gate_ref[...] = gates


def _post_mixer(x, ma, mb, mk, mv, cnt0, p, tm):
    b, t, d = x.shape
    grid = (b, t // tm)
    row_spec = pl.BlockSpec((None, tm, d), lambda i, j: (i, j, 0))
    small_spec = pl.BlockSpec((None, tm, LANES), lambda i, j: (i, j, 0))
    mem_spec = pl.BlockSpec((None, N_MEM, d), lambda i, j: (i, 0, 0))
    cnt_spec = pl.BlockSpec((1, LANES), lambda i, j: (0, 0))
    tril = jnp.tril(jnp.ones((tm, tm), BF16), -1)
    consts = [p['w_out'], p['g_xattn'], p['w_xq'], p['w_xo'], p['g_moe'], p['w_router'], p['b_router'], tril]
    return pl.pallas_call(
        _post_kernel,
        grid=grid,
        in_specs=[row_spec, row_spec, row_spec, mem_spec, mem_spec, cnt_spec]
                 + [_const_spec(c.shape) for c in consts],
        out_specs=[row_spec, row_spec, small_spec, small_spec, cnt_spec],
        out_shape=[jax.ShapeDtypeStruct((b, t, d), F32),
                   jax.ShapeDtypeStruct((b, t, d), F32),
                   jax.ShapeDtypeStruct((b, t, LANES), jnp.int32),
                   jax.ShapeDtypeStruct((b, t, LANES), F32),
                   jax.ShapeDtypeStruct((1, LANES), F32)],
        scratch_shapes=[pltpu.VMEM((1, LANES), F32)],
        compiler_params=pltpu.CompilerParams(dimension_semantics=("arbitrary", "arbitrary"),
                                             vmem_limit_bytes=VMEM_LIMIT),
        name="post_mixer",
    )(x, ma, mb, mk, mv, cnt0, *consts)


def _dispatch_kernel(dest_ref, x_ref, slots_ref, sem):
    tm = x_ref.shape[0]

    def row_copy(t, j):
        return pltpu.make_async_copy(x_ref.at[pl.ds(t, 1), :],
                                     slots_ref.at[pl.ds(dest_ref[t * TOP_K + j], 1), :], sem)

    def issue(t, carry):
        for j in range(TOP_K):
            row_copy(t, j).start(priority=j % DMA_PRIORITIES)
        return carry

    lax.fori_loop(0, tm, issue, 0)

    def drain(t, carry):
        for j in range(TOP_K):
            row_copy(t, j).wait()
        return carry

    lax.fori_loop(0, tm, drain, 0)


def _dispatch(xn, dest, n_slots, tm):
    n, d = xn.shape
    return pl.pallas_call(
        _dispatch_kernel,
        grid=(n // tm,),
        in_specs=[pl.BlockSpec((tm * TOP_K,), lambda i: (i,), memory_space=pltpu.SMEM),
                  pl.BlockSpec((tm, d), lambda i: (i, 0))],
        out_specs=pl.BlockSpec(memory_space=pl.ANY),
        out_shape=jax.ShapeDtypeStruct((n_slots, d), F32),
        scratch_shapes=[pltpu.SemaphoreType.DMA(())],
        compiler_params=pltpu.CompilerParams(dimension_semantics=("arbitrary",),
                                             has_side_effects=True, disable_bounds_checks=True),
        name="moe_dispatch",
    )(dest, xn)


def _moe_kernel(be_ref, valid_ref, x_ref, wgu_ref, bgu_ref, wd_ref, bd_ref, y_ref):
    del be_ref
    row = lax.broadcasted_iota(jnp.int32, x_ref.shape, 0)
    xb = jnp.where(row < valid_ref[pl.program_id(0)], x_ref[...], 0.0).astype(BF16)
    gu = _dot(xb, wgu_ref[...]) + bgu_ref[...]
    gate = jnp.minimum(gu[:, :D_EXPERT], SWIGLU_LIMIT)
    up = jnp.clip(gu[:, D_EXPERT:], -SWIGLU_LIMIT, SWIGLU_LIMIT)
    hmid = ((up + 1.0) * gate * _sigmoid(SWIGLU_ALPHA * gate)).astype(BF16)
    y_ref[...] = _dot(hmid, wd_ref[...]) + bd_ref[...]


def _moe_experts(slots, block_e, block_valid, p):
    n_slots, d = slots.shape
    n_blocks = n_slots // MOE_ROWS
    grid_spec = pltpu.PrefetchScalarGridSpec(
        num_scalar_prefetch=2,
        grid=(n_blocks,),
        in_specs=[pl.BlockSpec((MOE_ROWS, d), lambda i, be, bv: (i, 0)),
                  pl.BlockSpec((None, d, 2 * D_EXPERT), lambda i, be, bv: (be[i], 0, 0)),
                  pl.BlockSpec((None, 1, 2 * D_EXPERT), lambda i, be, bv: (be[i], 0, 0)),
                  pl.BlockSpec((None, D_EXPERT, d), lambda i, be, bv: (be[i], 0, 0)),
                  pl.BlockSpec((None, 1, d), lambda i, be, bv: (be[i], 0, 0))],
        out_specs=pl.BlockSpec((MOE_ROWS, d), lambda i, be, bv: (i, 0)),
    )
    return pl.pallas_call(
        _moe_kernel,
        grid_spec=grid_spec,
        out_shape=jax.ShapeDtypeStruct((n_slots, d), F32),
        compiler_params=pltpu.CompilerParams(dimension_semantics=("arbitrary",),
                                             vmem_limit_bytes=VMEM_LIMIT),
        name="moe_experts",
    )(block_e, block_valid, slots, p['w_gate_up'], p['b_gate_up'], p['w_down'], p['b_down'])


def _combine_kernel(dest_ref, h_ref, gate_ref, gfin_ref, y_hbm, out_ref, buf, sem):
    tm = h_ref.shape[0]

    def row_copy(t, j):
        return pltpu.make_async_copy(y_hbm.at[pl.ds(dest_ref[t * TOP_K + j], 1), :],
                                     buf.at[j, pl.ds(t, 1), :], sem)

    def issue(t, carry):
        for j in range(TOP_K):
            row_copy(t, j).start(priority=j % DMA_PRIORITIES)
        return carry

    lax.fori_loop(0, tm, issue, 0)

    def drain(t, carry):
        for j in range(TOP_K):
            row_copy(t, j).wait()
        return carry

    lax.fori_loop(0, tm, drain, 0)

    acc = h_ref[...]
    gates = gate_ref[...]
    for j in range(TOP_K):
        acc = acc + gates[:, j:j + 1] * buf[j]
    out_ref[...] = _rmsnorm(acc, gfin_ref[...])


def _combine(h, gates, dest, y_slots, g_final, tm):
    n, d = h.shape
    return pl.pallas_call(
        _combine_kernel,
        grid=(n // tm,),
        in_specs=[pl.BlockSpec((tm * TOP_K,), lambda i: (i,), memory_space=pltpu.SMEM),
                  pl.BlockSpec((tm, d), lambda i: (i, 0)),
                  pl.BlockSpec((tm, LANES), lambda i: (i, 0)),
                  _const_spec(g_final.shape),
                  pl.BlockSpec(memory_space=pl.ANY)],
        out_specs=pl.BlockSpec((tm, d), lambda i: (i, 0)),
        out_shape=jax.ShapeDtypeStruct((n, d), F32),
        scratch_shapes=[pltpu.VMEM((TOP_K, tm, d), F32), pltpu.SemaphoreType.DMA(())],
        compiler_params=pltpu.CompilerParams(dimension_semantics=("arbitrary",),
                                             vmem_limit_bytes=VMEM_LIMIT, disable_bounds_checks=True),
        name="moe_combine",
    )(dest, h, gates, g_final, y_slots)


def _moe(h, xn, meta, gates, counts, p, g_final):
    n, d = h.shape
    n_rows = n * TOP_K
    n_blocks = n_rows // MOE_ROWS + N_EXPERTS
    n_slots = n_blocks * MOE_ROWS
    cnt = counts[0, :N_EXPERTS].astype(jnp.int32)
    padded = (cnt + MOE_ROWS - 1) // MOE_ROWS * MOE_ROWS
    pad_end = jnp.cumsum(padded)
    pad_start = pad_end - padded
    block_row0 = jnp.arange(n_blocks, dtype=jnp.int32) * MOE_ROWS
    block_e = jnp.minimum(jnp.sum((pad_end[None, :] <= block_row0[:, None]).astype(jnp.int32), axis=1),
                          N_EXPERTS - 1)
    block_valid = jnp.clip((pad_start + cnt)[block_e] - block_row0, 0, MOE_ROWS)
    dest = (pad_start[meta[:, :TOP_K]] + meta[:, TOP_K:2 * TOP_K]).reshape(-1).astype(jnp.int32)
    slots = _dispatch(xn, dest, n_slots, min(DISPATCH_TILE, n))
    y_slots = _moe_experts(slots, block_e, block_valid, p)
    return _combine(h, gates, dest, y_slots, g_final, min(COMBINE_TILE, n))


def _prepare(g_mix, w_in, conv_w, conv_b, dt_bias, a_log, d_skip, g_ssm, w_ssm_branch, lb_logits, g_hgrn,
             w_hgrn_branch, w_out, g_mem, w_xk, w_xv, g_xattn, w_xq, w_xo, g_moe, w_router, b_router,
             w_gate_up, b_gate_up, w_down, b_down):
    d = D_MODEL
    w = w_in[0]
    o_z, o_xbc = 0, SSM_INNER
    o_dt = o_xbc + CONV_CH
    o_q = o_dt + SSM_HEADS
    o_ga = o_q + 4 * d
    o_gb = o_ga + d

    def row(v):
        return v.reshape(1, -1).astype(F32)

    def lane_pad(v, fill=0.0):
        return jnp.pad(v.reshape(1, -1).astype(F32), ((0, 0), (0, LANES - v.shape[-1])), constant_values=fill)

    head_of_col = np.arange(SSM_INNER) // SSM_HEAD_DIM
    expand = (np.arange(LANES)[:, None] == head_of_col[None, :])
    return dict(
        g_mix=row(g_mix[0]),
        w_z=w[:, o_z:o_z + SSM_INNER].astype(BF16),
        w_xbc=w[:, o_xbc:o_xbc + CONV_CH].astype(BF16),
        w_dt=jnp.pad(w[:, o_dt:o_dt + SSM_HEADS], ((0, 0), (0, LANES - SSM_HEADS))).astype(BF16),
        w_hg=w[:, o_q:o_q + 4 * d].astype(BF16),
        w_ga=w[:, o_ga:o_ga + d].astype(BF16),
        w_gb=w[:, o_gb:o_gb + d].astype(BF16),
        conv_w=conv_w[0].astype(F32),
        conv_b=row(conv_b[0]),
        dt_bias=lane_pad(dt_bias[0]),
        a_log=lane_pad(a_log[0]),
        d_skip=row(jnp.repeat(d_skip[0], SSM_HEAD_DIM)),
        g_ssm=row(g_ssm[0]),
        w_ssm_branch=w_ssm_branch[0].astype(BF16),
        lb_logits=lb_logits.astype(F32),
        g_hgrn=row(g_hgrn[0]),
        w_hgrn_branch=w_hgrn_branch[0].astype(BF16),
        w_out=w_out[0].astype(BF16),
        g_mem=row(g_mem[0]),
        w_xk=w_xk[0].astype(BF16),
        w_xv=w_xv[0].astype(BF16),
        g_xattn=row(g_xattn[0]),
        w_xq=w_xq[0].astype(BF16),
        w_xo=w_xo[0].astype(BF16),
        g_moe=row(g_moe[0]),
        w_router=jnp.pad(w_router[0], ((0, 0), (0, LANES - N_EXPERTS))).astype(BF16),
        b_router=lane_pad(b_router[0], fill=-jnp.inf),
        w_gate_up=w_gate_up[0].astype(BF16),
        b_gate_up=b_gate_up[0].reshape(N_EXPERTS, 1, 2 * D_EXPERT).astype(F32),
        w_down=w_down[0].astype(BF16),
        b_down=b_down[0].reshape(N_EXPERTS, 1, d).astype(F32),
        expand=jnp.asarray(expand, BF16),
    )


def _group_step(x, conv0, ssm0, hg0, mk, mv, p, g_final):
    b, t, d = x.shape
    tt = min(MIXER_TILE, t)
    ma, conv_n, ssm_n = _ssd_mixer(x, conv0, ssm0, p, tt)
    mb, hg_n = _hgrn_mixer(x, hg0, p, tt)
    tm = min(POST_TILE, t)
    cnt0 = jnp.zeros((1, LANES), F32)
    h2, xn3, meta, gates, counts = _post_mixer(x, ma, mb, mk, mv, cnt0, p, tm)
    n = b * t
    y = _moe(h2.reshape(n, d), xn3.reshape(n, d), meta.reshape(n, LANES), gates.reshape(n, LANES),
             counts, p, g_final)
    return y.reshape(b, t, d), conv_n, ssm_n, hg_n


def kernel(x_prompt, x_sample, mem_prompt, state_conv, state_ssm, state_hgrn, cache_mem_k, cache_mem_v, g_mix, w_in, conv_w, conv_b, dt_bias, a_log, d_skip, g_ssm, w_ssm_branch, lb_logits, g_hgrn, w_hgrn_branch, w_out, g_mem, w_xk, w_xv, g_xattn, w_xq, w_xo, g_moe, w_router, b_router, w_gate_up, b_gate_up, w_down, b_down, g_final):
    p = _prepare(g_mix, w_in, conv_w, conv_b, dt_bias, a_log, d_skip, g_ssm, w_ssm_branch, lb_logits, g_hgrn,
                 w_hgrn_branch, w_out, g_mem, w_xk, w_xv, g_xattn, w_xq, w_xo, g_moe, w_router, b_router,
                 w_gate_up, b_gate_up, w_down, b_down)
    g_fin = g_final.reshape(1, -1).astype(F32)
    bp = x_prompt.shape[0]
    bs = x_sample.shape[0]
    d = D_MODEL

    mk_p, mv_p = _memory_kv(mem_prompt, p)
    y_p, conv_p, ssm_p, hg_p = _group_step(
        x_prompt,
        jnp.zeros((bp, CONV_W - 1, CONV_CH), F32),
        jnp.zeros((bp, SSM_HEADS, SSM_HEAD_DIM, SSM_STATE), F32),
        jnp.zeros((bp, HG_HEADS, HG_K, HG_K), F32),
        mk_p, mv_p, p, g_fin)
    y_s, conv_s, ssm_s, hg_s = _group_step(
        x_sample, state_conv[0], state_ssm[0], state_hgrn[0],
        cache_mem_k[0].reshape(bs, N_MEM, d), cache_mem_v[0].reshape(bs, N_MEM, d), p, g_fin)

    kv_shape = (1, bp, N_MEM, X_HEADS, X_HEAD_DIM)
    return (y_p, y_s,
            conv_p[None], ssm_p[None], hg_p[None], mk_p.reshape(kv_shape), mv_p.reshape(kv_shape),
            conv_s[None], ssm_s[None], hg_s[None])
```

```python
import jax
import jax.numpy as jnp
import numpy as np
from jax import lax
from jax.experimental import pallas as pl
from jax.experimental.pallas import tpu as pltpu
from jax.experimental.pallas import tpu_sc as plsc

F32 = jnp.float32
BF16 = jnp.bfloat16

D_MODEL = 1024
CHUNK = 64
EPS = 1e-6
SSM_INNER = 2 * D_MODEL
SSM_HEAD_DIM = 64
SSM_HEADS = SSM_INNER // SSM_HEAD_DIM
SSM_GROUPS = 4
SSM_STATE = 128
GROUP_COLS = SSM_INNER // SSM_GROUPS
CONV_W = 4
BC_COLS = SSM_GROUPS * SSM_STATE
CONV_CH = SSM_INNER + 2 * BC_COLS
HG_HEADS = 8
HG_K = D_MODEL // HG_HEADS
N_MEM = 256
X_HEADS = 4
X_HEAD_DIM = D_MODEL // X_HEADS
N_EXPERTS = 32
TOP_K = 4
D_EXPERT = D_MODEL
SWIGLU_LIMIT = 7.0
SWIGLU_ALPHA = 1.702

LANES = 128
CONV_PAD = 8
MOE_ROWS = 256
MIXER_TILE = 256
POST_TILE = 512
COMBINE_TILE = 512
VMEM_LIMIT = 56 * 1024 * 1024
SC_CORES = 2
SC_SUBCORES = 16
SC_WORKERS = SC_CORES * SC_SUBCORES
SC_CHUNK = 32
SC_BUFFERS = 2


def _const_spec(shape):
    nd = len(shape)
    return pl.BlockSpec(shape, lambda *_: (0,) * nd, pipeline_mode=pl.Buffered(1))


def _dot(a, b):
    return jnp.dot(a, b, preferred_element_type=F32)


def _dot_nt(a, b):
    return lax.dot_general(a, b, (((1,), (1,)), ((), ())), preferred_element_type=F32)


def _dot_tn(a, b):
    return lax.dot_general(a, b, (((0,), (0,)), ((), ())), preferred_element_type=F32)


def _split3(a):
    hi = a.astype(BF16)
    r1 = a - hi.astype(F32)
    mid = r1.astype(BF16)
    lo = (r1 - mid.astype(F32)).astype(BF16)
    return hi, mid, lo


def _exact_dot_lhs01(sel, a):
    hi, mid, lo = _split3(a)
    return _dot(sel, hi) + _dot(sel, mid) + _dot(sel, lo)


def _rmsnorm(x, g):
    return x * lax.rsqrt(jnp.mean(x * x, axis=-1, keepdims=True) + EPS) * g


def _sigmoid(x):
    return jax.nn.sigmoid(x)


def _silu(x):
    return x * jax.nn.sigmoid(x)


def _softplus(x):
    return jnp.maximum(x, 0.0) + jnp.log1p(jnp.exp(-jnp.abs(x)))


def _ssd_kernel(x_ref, conv0_ref, ssm0_ref, g_ref, wxbc_ref, wz_ref, wdt_ref, wga_ref, convw_ref,
                convb_ref, dtb_ref, alog_ref, dskip_ref, gssm_ref, wbr_ref, tri_ref, expand_ref,
                ma_ref, convn_ref, ssmn_ref,
                xpad, xact, xdt, ysc, st):
    tt = x_ref.shape[0]
    ti = pl.program_id(1)
    nt = pl.num_programs(1)

    @pl.when(ti == 0)
    def _():
        xpad[CONV_PAD - (CONV_W - 1):CONV_PAD, :] = conv0_ref[...]
        st[...] = ssm0_ref[...].reshape(SSM_INNER, SSM_STATE).T

    xn = _rmsnorm(x_ref[...], g_ref[...]).astype(BF16)
    xpad[CONV_PAD:CONV_PAD + tt, :] = _dot(xn, wxbc_ref[...])

    conv = convb_ref[...] + xpad[CONV_PAD:CONV_PAD + tt, :] * convw_ref[CONV_W - 1:CONV_W, :]
    for k in range(1, CONV_W):
        conv = conv + xpad[CONV_PAD - k:CONV_PAD - k + tt, :] * convw_ref[CONV_W - 1 - k:CONV_W - k, :]
    xact[...] = _silu(conv)
    tail = xpad[CONV_PAD + tt - (CONV_W - 1):CONV_PAD + tt, :]
    xpad[CONV_PAD - (CONV_W - 1):CONV_PAD, :] = tail

    dt = _softplus(_dot(xn, wdt_ref[...]) + dtb_ref[...])
    acum_all = _exact_dot_lhs01(tri_ref[...], dt * -jnp.exp(alog_ref[...]))
    dt_hi = dt.astype(BF16)
    dt_lo = (dt - dt_hi.astype(F32)).astype(BF16)
    dt_x = _dot(dt_hi, expand_ref[...]) + _dot(dt_lo, expand_ref[...])
    xdt[...] = xact[:, :SSM_INNER] * dt_x

    lane = lax.broadcasted_iota(jnp.int32, (CHUNK, LANES), 1)
    row = lax.broadcasted_iota(jnp.int32, (CHUNK, LANES), 0)
    causal2 = row >= (lane % CHUNK)
    diag2 = row == (lane % CHUNK)
    left = lane < CHUNK
    pair_of_lane = lane // CHUNK

    for c in range(tt // CHUNK):
        rows = slice(c * CHUNK, (c + 1) * CHUNK)
        acum = acum_all[rows, :]
        for g in range(SSM_GROUPS):
            gs = slice(g * GROUP_COLS, (g + 1) * GROUP_COLS)
            b_g = xact[rows, SSM_INNER + g * SSM_STATE:SSM_INNER + (g + 1) * SSM_STATE].astype(BF16)
            c_g = xact[rows, SSM_INNER + BC_COLS + g * SSM_STATE:
                       SSM_INNER + BC_COLS + (g + 1) * SSM_STATE].astype(BF16)
            cb2 = _dot_nt(c_g, jnp.concatenate([b_g, b_g], axis=0))
            st_g = st[:, gs]
            y_inter = _dot(c_g, st_g.astype(BF16))
            xw, decay = [], []
            for j in range(GROUP_COLS // LANES):
                pair = g * (GROUP_COLS // LANES) + j
                ps = slice(pair * LANES, (pair + 1) * LANES)
                a_col = jnp.take_along_axis(acum, pair_of_lane + 2 * pair, axis=1)
                a_row = jnp.sum(jnp.where(diag2, a_col, 0.0), axis=0, keepdims=True)
                a_last = a_col[CHUNK - 1:CHUNK, :]
                dec = jnp.where(causal2, jnp.exp(jnp.minimum(a_col - a_row, 0.0)), 0.0)
                wts = (cb2 * dec).astype(BF16)
                xp = xdt[rows, ps]
                xbd = jnp.concatenate([jnp.where(left, xp, 0.0), jnp.where(left, 0.0, xp)],
                                      axis=0).astype(BF16)
                ysc[rows, ps] = _dot(wts, xbd) + y_inter[:, j * LANES:(j + 1) * LANES] * jnp.exp(a_col)
                xw.append((jnp.exp(a_last - a_col) * xp).astype(BF16))
                decay.append(jnp.exp(a_last))
            st[:, gs] = jnp.concatenate(decay, axis=1) * st_g + _dot_tn(b_g, jnp.concatenate(xw, axis=1))

    xs = xact[:, :SSM_INNER]
    y = ysc[...] + dskip_ref[...] * xs
    yz = y * _silu(_dot(xn, wz_ref[...]))
    parts = []
    for g in range(SSM_GROUPS):
        blk = yz[:, g * GROUP_COLS:(g + 1) * GROUP_COLS]
        parts.append(blk * lax.rsqrt(jnp.mean(blk * blk, axis=-1, keepdims=True) + EPS))
    yn = (jnp.concatenate(parts, axis=1) * gssm_ref[...]).astype(BF16)
    ya = _dot(yn, wbr_ref[...])
    ma_ref[...] = _sigmoid(_dot(xn, wga_ref[...])) * ya

    @pl.when(ti == nt - 1)
    def _():
        convn_ref[...] = tail
        ssmn_ref[...] = st[...].T.reshape(SSM_HEADS, SSM_HEAD_DIM, SSM_STATE)


def _chunk_tri(tt):
    return jnp.asarray(np.kron(np.eye(tt // CHUNK), np.tril(np.ones((CHUNK, CHUNK)))), BF16)


def _ssd_mixer(x, conv0, ssm0, p, tt):
    b, t, d = x.shape
    grid = (b, t // tt)
    row_spec = pl.BlockSpec((None, tt, d), lambda i, j: (i, j, 0))
    consts = [p['g_mix'], p['w_xbc'], p['w_z'], p['w_dt'], p['w_ga'], p['conv_w'], p['conv_b'], p['dt_bias'],
              p['a_log'], p['d_skip'], p['g_ssm'], p['w_ssm_branch'], _chunk_tri(tt), p['expand']]
    return pl.pallas_call(
        _ssd_kernel,
        grid=grid,
        in_specs=[row_spec,
                  pl.BlockSpec((None, CONV_W - 1, CONV_CH), lambda i, j: (i, 0, 0)),
                  pl.BlockSpec((None, SSM_HEADS, SSM_HEAD_DIM, SSM_STATE), lambda i, j: (i, 0, 0, 0))]
                 + [_const_spec(c.shape) for c in consts],
        out_specs=[row_spec,
                   pl.BlockSpec((None, CONV_W - 1, CONV_CH), lambda i, j: (i, 0, 0)),
                   pl.BlockSpec((None, SSM_HEADS, SSM_HEAD_DIM, SSM_STATE), lambda i, j: (i, 0, 0, 0))],
        out_shape=[jax.ShapeDtypeStruct((b, t, d), F32),
                   jax.ShapeDtypeStruct((b, CONV_W - 1, CONV_CH), F32),
                   jax.ShapeDtypeStruct((b, SSM_HEADS, SSM_HEAD_DIM, SSM_STATE), F32)],
        scratch_shapes=[pltpu.VMEM((CONV_PAD + tt, CONV_CH), F32),
                        pltpu.VMEM((tt, CONV_CH), F32),
                        pltpu.VMEM((tt, SSM_INNER), F32),
                        pltpu.VMEM((tt, SSM_INNER), F32),
                        pltpu.VMEM((SSM_STATE, SSM_INNER), F32)],
        compiler_params=pltpu.CompilerParams(dimension_semantics=("arbitrary", "arbitrary"),
                                             vmem_limit_bytes=VMEM_LIMIT),
        name="ssd_mixer",
    )(x, conv0, ssm0, *consts)


def _hgrn_kernel(x_ref, hg0_ref, g_ref, wh_ref, wgb_ref, lbl_ref, ghg_ref, wbr_ref, tri_ref,
                 mb_ref, hgn_ref,
                 osc, st):
    tt = x_ref.shape[0]
    n_chunks = tt // CHUNK
    ti = pl.program_id(1)
    nt = pl.num_programs(1)

    @pl.when(ti == 0)
    def _():
        for h in range(HG_HEADS):
            st[h] = hg0_ref[h].T

    xn = _rmsnorm(x_ref[...], g_ref[...]).astype(BF16)
    proj = _dot(xn, wh_ref[...])
    l0 = lbl_ref[0:1, :]
    l1 = lbl_ref[1:2, :]
    lmax = jnp.maximum(l0, l1)
    e0 = jnp.exp(l0 - lmax)
    lb = e0 / (e0 + jnp.exp(l1 - lmax))
    fr = proj[:, D_MODEL:2 * D_MODEL]
    kk = (1.0 - lb) * _sigmoid(-fr)
    v = _silu(proj[:, 2 * D_MODEL:3 * D_MODEL]).astype(BF16)
    gc = _exact_dot_lhs01(tri_ref[...], jnp.log(lb + (1.0 - lb) * _sigmoid(fr)))
    qh = (proj[:, :D_MODEL] * jnp.exp(gc)).astype(BF16)
    kh = (kk * jnp.exp(-gc)).astype(BF16)
    kt, decay = [], []
    for c in range(n_chunks):
        rows = slice(c * CHUNK, (c + 1) * CHUNK)
        glast = gc[(c + 1) * CHUNK - 1:(c + 1) * CHUNK, :]
        kt.append((kk[rows, :] * jnp.exp(glast - gc[rows, :])).astype(BF16))
        decay.append(jnp.exp(glast))

    row = lax.broadcasted_iota(jnp.int32, (tt, tt), 0)
    col = lax.broadcasted_iota(jnp.int32, (tt, tt), 1)
    causal = jnp.logical_and(row >= col, row // CHUNK == col // CHUNK)

    for h in range(HG_HEADS):
        hs = slice(h * HG_K, (h + 1) * HG_K)
        att = jnp.where(causal, _dot_nt(qh[:, hs], kh[:, hs]), 0.0).astype(BF16)
        o_intra = _dot(att, v[:, hs])
        s = st[h]
        o_inter = []
        for c in range(n_chunks):
            rows = slice(c * CHUNK, (c + 1) * CHUNK)
            o_inter.append(_dot_nt(qh[rows, hs], s.astype(BF16)))
            s = decay[c][:, hs] * s + _dot_tn(v[rows, hs], kt[c][:, hs])
        st[h] = s
        osc[:, hs] = o_intra + jnp.concatenate(o_inter, axis=0)

    parts = []
    for h in range(HG_HEADS):
        blk = osc[:, h * HG_K:(h + 1) * HG_K]
        parts.append(blk * lax.rsqrt(jnp.mean(blk * blk, axis=-1, keepdims=True) + EPS))
    on = (jnp.concatenate(parts, axis=1) * ghg_ref[...] * _silu(proj[:, 3 * D_MODEL:])).astype(BF16)
    yb = _dot(on, wbr_ref[...])
    mb_ref[...] = _sigmoid(_dot(xn, wgb_ref[...])) * yb

    @pl.when(ti == nt - 1)
    def _():
        for h in range(HG_HEADS):
            hgn_ref[h] = st[h].T


def _hgrn_mixer(x, hg0, p, tt):
    b, t, d = x.shape
    grid = (b, t // tt)
    row_spec = pl.BlockSpec((None, tt, d), lambda i, j: (i, j, 0))
    st_spec = pl.BlockSpec((None, HG_HEADS, HG_K, HG_K), lambda i, j: (i, 0, 0, 0))
    consts = [p['g_mix'], p['w_hg'], p['w_gb'], p['lb_logits'], p['g_hgrn'], p['w_hgrn_branch'], _chunk_tri(tt)]
    return pl.pallas_call(
        _hgrn_kernel,
        grid=grid,
        in_specs=[row_spec, st_spec] + [_const_spec(c.shape) for c in consts],
        out_specs=[row_spec, st_spec],
        out_shape=[jax.ShapeDtypeStruct((b, t, d), F32),
                   jax.ShapeDtypeStruct((b, HG_HEADS, HG_K, HG_K), F32)],
        scratch_shapes=[pltpu.VMEM((tt, d), F32), pltpu.VMEM((HG_HEADS, HG_K, HG_K), F32)],
        compiler_params=pltpu.CompilerParams(dimension_semantics=("arbitrary", "arbitrary"),
                                             vmem_limit_bytes=VMEM_LIMIT),
        name="hgrn_mixer",
    )(x, hg0, *consts)


def _memkv_kernel(m_ref, g_ref, wk_ref, wv_ref, k_ref, v_ref):
    mn = _rmsnorm(m_ref[...], g_ref[...]).astype(BF16)
    k_ref[...] = _dot(mn, wk_ref[...])
    v_ref[...] = _dot(mn, wv_ref[...])


def _memory_kv(mem, p):
    b, n, d = mem.shape
    spec = pl.BlockSpec((None, n, d), lambda i: (i, 0, 0))
    consts = [p['g_mem'], p['w_xk'], p['w_xv']]
    return pl.pallas_call(
        _memkv_kernel,
        grid=(b,),
        in_specs=[spec] + [_const_spec(c.shape) for c in consts],
        out_specs=[spec, spec],
        out_shape=[jax.ShapeDtypeStruct((b, n, d), F32)] * 2,
        compiler_params=pltpu.CompilerParams(dimension_semantics=("arbitrary",),
                                             vmem_limit_bytes=VMEM_LIMIT),
        name="memory_kv",
    )(mem, *consts)


def _post_kernel(x_ref, ma_ref, mb_ref, mk_ref, mv_ref, cnt0_ref, wo_ref, gx_ref, wq_ref, wxo_ref,
                 gmoe_ref, wr_ref, br_ref, tril_ref,
                 h_ref, xn_ref, meta_ref, gate_ref, cnt_ref,
                 base):
    tm = x_ref.shape[0]
    first = jnp.logical_and(pl.program_id(0) == 0, pl.program_id(1) == 0)

    @pl.when(first)
    def _():
        base[...] = cnt0_ref[...]

    m = (ma_ref[...] + mb_ref[...]).astype(BF16)
    h1 = x_ref[...] + _dot(m, wo_ref[...])

    hn = _rmsnorm(h1, gx_ref[...]).astype(BF16)
    q = _dot(hn, wq_ref[...])
    heads = []
    for hh in range(X_HEADS):
        hs = slice(hh * X_HEAD_DIM, (hh + 1) * X_HEAD_DIM)
        s = _dot_nt(q[:, hs].astype(BF16), mk_ref[:, hs].astype(BF16)) * (X_HEAD_DIM ** -0.5)
        s = s - jnp.max(s, axis=-1, keepdims=True)
        e = jnp.exp(s)
        pr = e / jnp.sum(e, axis=-1, keepdims=True)
        heads.append(_dot(pr.astype(BF16), mv_ref[:, hs].astype(BF16)))
    o = jnp.concatenate(heads, axis=1).astype(BF16)
    h2 = h1 + _dot(o, wxo_ref[...])
    h_ref[...] = h2

    xn3 = _rmsnorm(h2, gmoe_ref[...])
    xn_ref[...] = xn3
    logits = _dot(xn3.astype(BF16), wr_ref[...]) + br_ref[...]

    lane = lax.broadcasted_iota(jnp.int32, (tm, LANES), 1)
    lane_f = lane.astype(F32)
    run = logits
    vals, hots = [], []
    for _ in range(TOP_K):
        mx = jnp.max(run, axis=-1, keepdims=True)
        idx = jnp.min(jnp.where(run == mx, lane_f, float(LANES)), axis=-1, keepdims=True)
        hot = lane_f == idx
        run = jnp.where(hot, -jnp.inf, run)
        vals.append(mx)
        hots.append(hot)
    es = [jnp.exp(v - vals[0]) for v in vals]
    den = es[0] + es[1] + es[2] + es[3]

    tot = jnp.zeros((tm, LANES), F32)
    for hot in hots:
        tot = tot + hot.astype(F32)
    before = base[...] + _dot(tril_ref[...], tot.astype(BF16))
    base[...] = base[...] + jnp.sum(tot, axis=0, keepdims=True)
    cnt_ref[...] = base[...]

    meta = jnp.zeros((tm, LANES), F32)
    gates = jnp.zeros((tm, LANES), F32)
    for j in range(TOP_K):
        e_idx = jnp.sum(jnp.where(hots[j], lane_f, 0.0), axis=-1, keepdims=True)
        rank = jnp.sum(jnp.where(hots[j], before, 0.0), axis=-1, keepdims=True)
        meta = jnp.where(lane == j, e_idx, meta)
        meta = jnp.where(lane == TOP_K + j, rank, meta)
        gates = jnp.where(lane == j, es[j] / den, gates)
    meta_ref[...] = meta.astype(jnp.int32)
    gate_ref[...] = gates


def _post_mixer(x, ma, mb, mk, mv, cnt0, p, tm):
    b, t, d = x.shape
    grid = (b, t // tm)
    row_spec = pl.BlockSpec((None, tm, d), lambda i, j: (i, j, 0))
    small_spec = pl.BlockSpec((None, tm, LANES), lambda i, j: (i, j, 0))
    mem_spec = pl.BlockSpec((None, N_MEM, d), lambda i, j: (i, 0, 0))
    cnt_spec = pl.BlockSpec((1, LANES), lambda i, j: (0, 0))
    tril = jnp.tril(jnp.ones((tm, tm), BF16), -1)
    consts = [p['w_out'], p['g_xattn'], p['w_xq'], p['w_xo'], p['g_moe'], p['w_router'], p['b_router'], tril]
    return pl.pallas_call(
        _post_kernel,
        grid=grid,
        in_specs=[row_spec, row_spec, row_spec, mem_spec, mem_spec, cnt_spec]
                 + [_const_spec(c.shape) for c in consts],
        out_specs=[row_spec, row_spec, small_spec, small_spec, cnt_spec],
        out_shape=[jax.ShapeDtypeStruct((b, t, d), F32),
                   jax.ShapeDtypeStruct((b, t, d), F32),
                   jax.ShapeDtypeStruct((b, t, LANES), jnp.int32),
                   jax.ShapeDtypeStruct((b, t, LANES), F32),
                   jax.ShapeDtypeStruct((1, LANES), F32)],
        scratch_shapes=[pltpu.VMEM((1, LANES), F32)],
        compiler_params=pltpu.CompilerParams(dimension_semantics=("arbitrary", "arbitrary"),
                                             vmem_limit_bytes=VMEM_LIMIT),
        name="post_mixer",
    )(x, ma, mb, mk, mv, cnt0, *consts)


def _sc_chunk(rows_per_worker):
    chunk = SC_CHUNK
    while rows_per_worker % (SC_BUFFERS * chunk):
        chunk //= 2
    assert chunk % 8 == 0, rows_per_worker
    return chunk


def _sc_mesh():
    return plsc.VectorSubcoreMesh(core_axis_name="c", subcore_axis_name="s")


def _sc_worker():
    return lax.axis_index("s") * SC_CORES + lax.axis_index("c")


def _dispatch(xn, dest, n_slots):
    n, d = xn.shape
    per_w = n // SC_WORKERS
    chunk = _sc_chunk(per_w)
    n_chunks = per_w // chunk
    idx = dest.reshape(n // chunk, chunk, TOP_K).transpose(0, 2, 1)

    def body(x_hbm, idx_hbm, out_hbm, idx_v, rows_v, rsem, wsem):
        wid = _sc_worker()

        def read(i, b):
            blk = wid * n_chunks + i
            return pltpu.make_async_copy(x_hbm.at[pl.ds(blk * chunk, chunk)], rows_v.at[b], rsem.at[b])

        def read_start(i, b):
            pltpu.sync_copy(idx_hbm.at[wid * n_chunks + i], idx_v.at[b])
            read(i, b).start()

        def write(b, j):
            return pltpu.make_async_copy(rows_v.at[b], out_hbm.at[idx_v.at[b, j]], wsem.at[b])

        for b in range(SC_BUFFERS):
            read_start(b, b)

        @pl.loop(0, n_chunks, step=SC_BUFFERS)
        def _(i0):
            for b in range(SC_BUFFERS):
                i = i0 + b
                read(i, b).wait()
                for j in range(TOP_K):
                    write(b, j).start()
                for j in range(TOP_K):
                    write(b, j).wait()

                @pl.when(i + SC_BUFFERS < n_chunks)
                def _():
                    read_start(i + SC_BUFFERS, b)

    return pl.kernel(
        body, mesh=_sc_mesh(),
        out_type=jax.ShapeDtypeStruct((n_slots, d), F32),
        scratch_types=[pltpu.VMEM((SC_BUFFERS, TOP_K, chunk), jnp.int32),
                       pltpu.VMEM((SC_BUFFERS, chunk, d), F32),
                       pltpu.SemaphoreType.DMA((SC_BUFFERS,)),
                       pltpu.SemaphoreType.DMA((SC_BUFFERS,))],
        name="moe_dispatch",
    )(xn, idx)


def _gather_rows(table, idx):
    n_out = idx.shape[0]
    d = table.shape[1]
    per_w = n_out // SC_WORKERS
    chunk = _sc_chunk(per_w)
    n_chunks = per_w // chunk

    def body(table_hbm, idx_hbm, out_hbm, idx_v, rows_v, gsem, wsem):
        base = _sc_worker() * per_w

        def gather(b):
            return pltpu.make_async_copy(table_hbm.at[idx_v.at[b]], rows_v.at[b], gsem.at[b])

        def gather_start(i, b):
            pltpu.sync_copy(idx_hbm.at[pl.ds(base + i * chunk, chunk)], idx_v.at[b])
            gather(b).start()

        def write(i, b):
            return pltpu.make_async_copy(rows_v.at[b], out_hbm.at[pl.ds(base + i * chunk, chunk)], wsem.at[b])

        for b in range(SC_BUFFERS):
            gather_start(b, b)

        @pl.loop(0, n_chunks, step=SC_BUFFERS)
        def _(i0):
            for b in range(SC_BUFFERS):
                i = i0 + b
                gather(b).wait()
                write(i, b).start()
                write(i, b).wait()

                @pl.when(i + SC_BUFFERS < n_chunks)
                def _():
                    gather_start(i + SC_BUFFERS, b)

    return pl.kernel(
        body, mesh=_sc_mesh(),
        out_type=jax.ShapeDtypeStruct((n_out, d), F32),
        scratch_types=[pltpu.VMEM((SC_BUFFERS, chunk), jnp.int32),
                       pltpu.VMEM((SC_BUFFERS, chunk, d), F32),
                       pltpu.SemaphoreType.DMA((SC_BUFFERS,)),
                       pltpu.SemaphoreType.DMA((SC_BUFFERS,))],
        name="moe_gather",
    )(table, idx)


def _moe_kernel(be_ref, valid_ref, x_ref, wgu_ref, bgu_ref, wd_ref, bd_ref, y_ref):
    del be_ref
    row = lax.broadcasted_iota(jnp.int32, x_ref.shape, 0)
    xb = jnp.where(row < valid_ref[pl.program_id(0)], x_ref[...], 0.0).astype(BF16)
    gu = _dot(xb, wgu_ref[...]) + bgu_ref[...]
    gate = jnp.minimum(gu[:, :D_EXPERT], SWIGLU_LIMIT)
    up = jnp.clip(gu[:, D_EXPERT:], -SWIGLU_LIMIT, SWIGLU_LIMIT)
    hmid = ((up + 1.0) * gate * _sigmoid(SWIGLU_ALPHA * gate)).astype(BF16)
    y_ref[...] = _dot(hmid, wd_ref[...]) + bd_ref[...]


def _moe_experts(slots, block_e, block_valid, p):
    n_slots, d = slots.shape
    n_blocks = n_slots // MOE_ROWS
    grid_spec = pltpu.PrefetchScalarGridSpec(
        num_scalar_prefetch=2,
        grid=(n_blocks,),
        in_specs=[pl.BlockSpec((MOE_ROWS, d), lambda i, be, bv: (i, 0)),
                  pl.BlockSpec((None, d, 2 * D_EXPERT), lambda i, be, bv: (be[i], 0, 0)),
                  pl.BlockSpec((None, 1, 2 * D_EXPERT), lambda i, be, bv: (be[i], 0, 0)),
                  pl.BlockSpec((None, D_EXPERT, d), lambda i, be, bv: (be[i], 0, 0)),
                  pl.BlockSpec((None, 1, d), lambda i, be, bv: (be[i], 0, 0))],
        out_specs=pl.BlockSpec((MOE_ROWS, d), lambda i, be, bv: (i, 0)),
    )
    return pl.pallas_call(
        _moe_kernel,
        grid_spec=grid_spec,
        out_shape=jax.ShapeDtypeStruct((n_slots, d), F32),
        compiler_params=pltpu.CompilerParams(dimension_semantics=("arbitrary",),
                                             vmem_limit_bytes=VMEM_LIMIT),
        name="moe_experts",
    )(block_e, block_valid, slots, p['w_gate_up'], p['b_gate_up'], p['w_down'], p['b_down'])


def _combine_kernel(h_ref, gate_ref, y_ref, gfin_ref, out_ref):
    acc = h_ref[...]
    gates = gate_ref[...]
    for j in range(TOP_K):
        acc = acc + gates[:, j:j + 1] * y_ref[j]
    out_ref[...] = _rmsnorm(acc, gfin_ref[...])


def _combine(h, gates, y_tok, g_final, tm):
    n, d = h.shape
    return pl.pallas_call(
        _combine_kernel,
        grid=(n // tm,),
        in_specs=[pl.BlockSpec((tm, d), lambda i: (i, 0)),
                  pl.BlockSpec((tm, LANES), lambda i: (i, 0)),
                  pl.BlockSpec((TOP_K, tm, d), lambda i: (0, i, 0)),
                  _const_spec(g_final.shape)],
        out_specs=pl.BlockSpec((tm, d), lambda i: (i, 0)),
        out_shape=jax.ShapeDtypeStruct((n, d), F32),
        compiler_params=pltpu.CompilerParams(dimension_semantics=("arbitrary",),
                                             vmem_limit_bytes=VMEM_LIMIT),
        name="moe_combine",
    )(h, gates, y_tok, g_final)


def _moe(h, xn, meta, gates, counts, p, g_final):
    n, d = h.shape
    n_rows = n * TOP_K
    n_blocks = n_rows // MOE_ROWS + N_EXPERTS
    n_slots = n_blocks * MOE_ROWS
    cnt = counts[0, :N_EXPERTS].astype(jnp.int32)
    padded = (cnt + MOE_ROWS - 1) // MOE_ROWS * MOE_ROWS
    pad_end = jnp.cumsum(padded)
    pad_start = pad_end - padded
    block_row0 = jnp.arange(n_blocks, dtype=jnp.int32) * MOE_ROWS
    block_e = jnp.minimum(jnp.sum((pad_end[None, :] <= block_row0[:, None]).astype(jnp.int32), axis=1),
                          N_EXPERTS - 1)
    block_valid = jnp.clip((pad_start + cnt)[block_e] - block_row0, 0, MOE_ROWS)
    dest = (pad_start[meta[:, :TOP_K]] + meta[:, TOP_K:2 * TOP_K]).astype(jnp.int32)
    slots = _dispatch(xn, dest, n_slots)
    y_slots = _moe_experts(slots, block_e, block_valid, p)
    y_tok = _gather_rows(y_slots, dest.T.reshape(-1)).reshape(TOP_K, n, d)
    return _combine(h, gates, y_tok, g_final, min(COMBINE_TILE, n))


def _prepare(g_mix, w_in, conv_w, conv_b, dt_bias, a_log, d_skip, g_ssm, w_ssm_branch, lb_logits, g_hgrn,
             w_hgrn_branch, w_out, g_mem, w_xk, w_xv, g_xattn, w_xq, w_xo, g_moe, w_router, b_router,
             w_gate_up, b_gate_up, w_down, b_down):
    d = D_MODEL
    w = w_in[0]
    o_z, o_xbc = 0, SSM_INNER
    o_dt = o_xbc + CONV_CH
    o_q = o_dt + SSM_HEADS
    o_ga = o_q + 4 * d
    o_gb = o_ga + d

    def row(v):
        return v.reshape(1, -1).astype(F32)

    def lane_pad(v, fill=0.0):
        return jnp.pad(v.reshape(1, -1).astype(F32), ((0, 0), (0, LANES - v.shape[-1])), constant_values=fill)

    head_of_col = np.arange(SSM_INNER) // SSM_HEAD_DIM
    expand = (np.arange(LANES)[:, None] == head_of_col[None, :])
    return dict(
        g_mix=row(g_mix[0]),
        w_z=w[:, o_z:o_z + SSM_INNER].astype(BF16),
        w_xbc=w[:, o_xbc:o_xbc + CONV_CH].astype(BF16),
        w_dt=jnp.pad(w[:, o_dt:o_dt + SSM_HEADS], ((0, 0), (0, LANES - SSM_HEADS))).astype(BF16),
        w_hg=w[:, o_q:o_q + 4 * d].astype(BF16),
        w_ga=w[:, o_ga:o_ga + d].astype(BF16),
        w_gb=w[:, o_gb:o_gb + d].astype(BF16),
        conv_w=conv_w[0].astype(F32),
        conv_b=row(conv_b[0]),
        dt_bias=lane_pad(dt_bias[0]),
        a_log=lane_pad(a_log[0]),
        d_skip=row(jnp.repeat(d_skip[0], SSM_HEAD_DIM)),
        g_ssm=row(g_ssm[0]),
        w_ssm_branch=w_ssm_branch[0].astype(BF16),
        lb_logits=lb_logits.astype(F32),
        g_hgrn=row(g_hgrn[0]),
        w_hgrn_branch=w_hgrn_branch[0].astype(BF16),
        w_out=w_out[0].astype(BF16),
        g_mem=row(g_mem[0]),
        w_xk=w_xk[0].astype(BF16),
        w_xv=w_xv[0].astype(BF16),
        g_xattn=row(g_xattn[0]),
        w_xq=w_xq[0].astype(BF16),
        w_xo=w_xo[0].astype(BF16),
        g_moe=row(g_moe[0]),
        w_router=jnp.pad(w_router[0], ((0, 0), (0, LANES - N_EXPERTS))).astype(BF16),
        b_router=lane_pad(b_router[0], fill=-jnp.inf),
        w_gate_up=w_gate_up[0].astype(BF16),
        b_gate_up=b_gate_up[0].reshape(N_EXPERTS, 1, 2 * D_EXPERT).astype(F32),
        w_down=w_down[0].astype(BF16),
        b_down=b_down[0].reshape(N_EXPERTS, 1, d).astype(F32),
        expand=jnp.asarray(expand, BF16),
    )


def _group_step(x, conv0, ssm0, hg0, mk, mv, p, g_final):
    b, t, d = x.shape
    tt = min(MIXER_TILE, t)
    ma, conv_n, ssm_n = _ssd_mixer(x, conv0, ssm0, p, tt)
    mb, hg_n = _hgrn_mixer(x, hg0, p, tt)
    tm = min(POST_TILE, t)
    cnt0 = jnp.zeros((1, LANES), F32)
    h2, xn3, meta, gates, counts = _post_mixer(x, ma, mb, mk, mv, cnt0, p, tm)
    n = b * t
    y = _moe(h2.reshape(n, d), xn3.reshape(n, d), meta.reshape(n, LANES), gates.reshape(n, LANES),
             counts, p, g_final)
    return y.reshape(b, t, d), conv_n, ssm_n, hg_n


def kernel(x_prompt, x_sample, mem_prompt, state_conv, state_ssm, state_hgrn, cache_mem_k, cache_mem_v, g_mix, w_in, conv_w, conv_b, dt_bias, a_log, d_skip, g_ssm, w_ssm_branch, lb_logits, g_hgrn, w_hgrn_branch, w_out, g_mem, w_xk, w_xv, g_xattn, w_xq, w_xo, g_moe, w_router, b_router, w_gate_up, b_gate_up, w_down, b_down, g_final):
    p = _prepare(g_mix, w_in, conv_w, conv_b, dt_bias, a_log, d_skip, g_ssm, w_ssm_branch, lb_logits, g_hgrn,
                 w_hgrn_branch, w_out, g_mem, w_xk, w_xv, g_xattn, w_xq, w_xo, g_moe, w_router, b_router,
                 w_gate_up, b_gate_up, w_down, b_down)
    g_fin = g_final.reshape(1, -1).astype(F32)
    bp = x_prompt.shape[0]
    bs = x_sample.shape[0]
    d = D_MODEL

    mk_p, mv_p = _memory_kv(mem_prompt, p)
    y_p, conv_p, ssm_p, hg_p = _group_step(
        x_prompt,
        jnp.zeros((bp, CONV_W - 1, CONV_CH), F32),
        jnp.zeros((bp, SSM_HEADS, SSM_HEAD_DIM, SSM_STATE), F32),
        jnp.zeros((bp, HG_HEADS, HG_K, HG_K), F32),
        mk_p, mv_p, p, g_fin)
    y_s, conv_s, ssm_s, hg_s = _group_step(
        x_sample, state_conv[0], state_ssm[0], state_hgrn[0],
        cache_mem_k[0].reshape(bs, N_MEM, d), cache_mem_v[0].reshape(bs, N_MEM, d), p, g_fin)

    kv_shape = (1, bp, N_MEM, X_HEADS, X_HEAD_DIM)
    return (y_p, y_s,
            conv_p[None], ssm_p[None], hg_p[None], mk_p.reshape(kv_shape), mv_p.reshape(kv_shape),
            conv_s[None], ssm_s[None], hg_s[None])
```

```python
import jax
import jax.numpy as jnp
import numpy as np
from jax import lax
from jax.experimental import pallas as pl
from jax.experimental.pallas import tpu as pltpu
from jax.experimental.pallas import tpu_sc as plsc

F32 = jnp.float32
BF16 = jnp.bfloat16

D_MODEL = 1024
CHUNK = 64
EPS = 1e-6
SSM_INNER = 2 * D_MODEL
SSM_HEAD_DIM = 64
SSM_HEADS = SSM_INNER // SSM_HEAD_DIM
SSM_GROUPS = 4
SSM_STATE = 128
GROUP_COLS = SSM_INNER // SSM_GROUPS
CONV_W = 4
BC_COLS = SSM_GROUPS * SSM_STATE
CONV_CH = SSM_INNER + 2 * BC_COLS
HG_HEADS = 8
HG_K = D_MODEL // HG_HEADS
HG_SUB = CHUNK // 2
N_MEM = 256
X_HEADS = 4
X_HEAD_DIM = D_MODEL // X_HEADS
N_EXPERTS = 32
TOP_K = 4
D_EXPERT = D_MODEL
SWIGLU_LIMIT = 7.0
SWIGLU_ALPHA = 1.702

LANES = 128
CONV_PAD = 8
MOE_ROWS = 256
MIXER_TILE = 256
POST_TILE = 512
COMBINE_TILE = 512
VMEM_LIMIT = 56 * 1024 * 1024
SC_CORES = 2
SC_SUBCORES = 16
SC_WORKERS = SC_CORES * SC_SUBCORES
SC_CHUNK = 64
SC_BUFFERS = 2


def _const_spec(shape):
    nd = len(shape)
    return pl.BlockSpec(shape, lambda *_: (0,) * nd, pipeline_mode=pl.Buffered(1))


def _dot(a, b):
    return jnp.dot(a, b, preferred_element_type=F32)


def _dot_nt(a, b):
    return lax.dot_general(a, b, (((1,), (1,)), ((), ())), preferred_element_type=F32)


def _dot_tn(a, b):
    return lax.dot_general(a, b, (((0,), (0,)), ((), ())), preferred_element_type=F32)


def _split3(a):
    hi = a.astype(BF16)
    r1 = a - hi.astype(F32)
    mid = r1.astype(BF16)
    lo = (r1 - mid.astype(F32)).astype(BF16)
    return hi, mid, lo


def _exact_dot_lhs01(sel, a):
    hi, mid, lo = _split3(a)
    return _dot(sel, hi) + _dot(sel, mid) + _dot(sel, lo)


def _rmsnorm(x, g):
    return x * lax.rsqrt(jnp.mean(x * x, axis=-1, keepdims=True) + EPS) * g


def _sigmoid(x):
    return jax.nn.sigmoid(x)


def _silu(x):
    return x * jax.nn.sigmoid(x)


def _softplus(x):
    return jnp.maximum(x, 0.0) + jnp.log1p(jnp.exp(-jnp.abs(x)))


def _pack_bf16_pairs(x):
    c = x.shape[1] // 2
    hi = lax.bitcast_convert_type(x[:, :c].astype(BF16).astype(F32), jnp.int32)
    lo = lax.bitcast_convert_type(x[:, c:].astype(BF16).astype(F32), jnp.int32)
    return hi | lax.shift_right_logical(lo, 16)


def _unpack_bf16_pairs(u):
    hi = lax.bitcast_convert_type(u & jnp.int32(-65536), F32)
    lo = lax.bitcast_convert_type(lax.shift_left(u, 16), F32)
    return hi, lo


def _ssd_kernel(x_ref, conv0_ref, ssm0_ref, g_ref, wxbc_ref, wz_ref, wdt_ref, wga_ref, convw_ref,
                convb_ref, dtb_ref, alog_ref, dskip_ref, gssm_ref, wbr_ref, tri_ref, expand_ref,
                ma_ref, convn_ref, ssmn_ref,
                xpad, xact, xdt, ysc, st):
    tt = x_ref.shape[0]
    ti = pl.program_id(1)
    nt = pl.num_programs(1)

    @pl.when(ti == 0)
    def _():
        xpad[CONV_PAD - (CONV_W - 1):CONV_PAD, :] = conv0_ref[...]
        st[...] = ssm0_ref[...].reshape(SSM_INNER, SSM_STATE).T

    xn = _rmsnorm(x_ref[...], g_ref[...]).astype(BF16)
    xpad[CONV_PAD:CONV_PAD + tt, :] = _dot(xn, wxbc_ref[...])

    conv = convb_ref[...] + xpad[CONV_PAD:CONV_PAD + tt, :] * convw_ref[CONV_W - 1:CONV_W, :]
    for k in range(1, CONV_W):
        conv = conv + xpad[CONV_PAD - k:CONV_PAD - k + tt, :] * convw_ref[CONV_W - 1 - k:CONV_W - k, :]
    xact[...] = _silu(conv)
    tail = xpad[CONV_PAD + tt - (CONV_W - 1):CONV_PAD + tt, :]
    xpad[CONV_PAD - (CONV_W - 1):CONV_PAD, :] = tail

    dt = _softplus(_dot(xn, wdt_ref[...]) + dtb_ref[...])
    acum_all = _exact_dot_lhs01(tri_ref[...], dt * -jnp.exp(alog_ref[...]))
    dt_hi = dt.astype(BF16)
    dt_lo = (dt - dt_hi.astype(F32)).astype(BF16)
    dt_x = _dot(dt_hi, expand_ref[...]) + _dot(dt_lo, expand_ref[...])
    xdt[...] = xact[:, :SSM_INNER] * dt_x

    lane = lax.broadcasted_iota(jnp.int32, (CHUNK, LANES), 1)
    row = lax.broadcasted_iota(jnp.int32, (CHUNK, LANES), 0)
    causal2 = row >= (lane % CHUNK)
    diag2 = row == (lane % CHUNK)
    left = lane < CHUNK
    pair_of_lane = lane // CHUNK

    for c in range(tt // CHUNK):
        rows = slice(c * CHUNK, (c + 1) * CHUNK)
        acum = acum_all[rows, :]
        for g in range(SSM_GROUPS):
            gs = slice(g * GROUP_COLS, (g + 1) * GROUP_COLS)
            b_g = xact[rows, SSM_INNER + g * SSM_STATE:SSM_INNER + (g + 1) * SSM_STATE].astype(BF16)
            c_g = xact[rows, SSM_INNER + BC_COLS + g * SSM_STATE:
                       SSM_INNER + BC_COLS + (g + 1) * SSM_STATE].astype(BF16)
            cb2 = _dot_nt(c_g, jnp.concatenate([b_g, b_g], axis=0))
            st_g = st[:, gs]
            y_inter = _dot(c_g, st_g.astype(BF16))
            xw, decay = [], []
            for j in range(GROUP_COLS // LANES):
                pair = g * (GROUP_COLS // LANES) + j
                ps = slice(pair * LANES, (pair + 1) * LANES)
                a_col = jnp.take_along_axis(acum, pair_of_lane + 2 * pair, axis=1)
                a_row = jnp.sum(jnp.where(diag2, a_col, 0.0), axis=0, keepdims=True)
                a_last = a_col[CHUNK - 1:CHUNK, :]
                dec = jnp.where(causal2, jnp.exp(jnp.minimum(a_col - a_row, 0.0)), 0.0)
                wts = (cb2 * dec).astype(BF16)
                xp = xdt[rows, ps]
                xbd = jnp.concatenate([jnp.where(left, xp, 0.0), jnp.where(left, 0.0, xp)],
                                      axis=0).astype(BF16)
                ysc[rows, ps] = _dot(wts, xbd) + y_inter[:, j * LANES:(j + 1) * LANES] * jnp.exp(a_col)
                xw.append((jnp.exp(a_last - a_col) * xp).astype(BF16))
                decay.append(jnp.exp(a_last))
            st[:, gs] = jnp.concatenate(decay, axis=1) * st_g + _dot_tn(b_g, jnp.concatenate(xw, axis=1))

    xs = xact[:, :SSM_INNER]
    y = ysc[...] + dskip_ref[...] * xs
    yz = y * _silu(_dot(xn, wz_ref[...]))
    parts = []
    for g in range(SSM_GROUPS):
        blk = yz[:, g * GROUP_COLS:(g + 1) * GROUP_COLS]
        parts.append(blk * lax.rsqrt(jnp.mean(blk * blk, axis=-1, keepdims=True) + EPS))
    yn = (jnp.concatenate(parts, axis=1) * gssm_ref[...]).astype(BF16)
    ya = _dot(yn, wbr_ref[...])
    ma_ref[...] = _sigmoid(_dot(xn, wga_ref[...])) * ya

    @pl.when(ti == nt - 1)
    def _():
        convn_ref[...] = tail
        ssmn_ref[...] = st[...].T.reshape(SSM_HEADS, SSM_HEAD_DIM, SSM_STATE)


def _chunk_tri(tt):
    return jnp.asarray(np.kron(np.eye(tt // CHUNK), np.tril(np.ones((CHUNK, CHUNK)))), BF16)


def _ssd_mixer(x, conv0, ssm0, p, tt):
    b, t, d = x.shape
    grid = (b, t // tt)
    row_spec = pl.BlockSpec((None, tt, d), lambda i, j: (i, j, 0))
    consts = [p['g_mix'], p['w_xbc'], p['w_z'], p['w_dt'], p['w_ga'], p['conv_w'], p['conv_b'], p['dt_bias'],
              p['a_log'], p['d_skip'], p['g_ssm'], p['w_ssm_branch'], _chunk_tri(tt), p['expand']]
    return pl.pallas_call(
        _ssd_kernel,
        grid=grid,
        in_specs=[row_spec,
                  pl.BlockSpec((None, CONV_W - 1, CONV_CH), lambda i, j: (i, 0, 0)),
                  pl.BlockSpec((None, SSM_HEADS, SSM_HEAD_DIM, SSM_STATE), lambda i, j: (i, 0, 0, 0))]
                 + [_const_spec(c.shape) for c in consts],
        out_specs=[row_spec,
                   pl.BlockSpec((None, CONV_W - 1, CONV_CH), lambda i, j: (i, 0, 0)),
                   pl.BlockSpec((None, SSM_HEADS, SSM_HEAD_DIM, SSM_STATE), lambda i, j: (i, 0, 0, 0))],
        out_shape=[jax.ShapeDtypeStruct((b, t, d), F32),
                   jax.ShapeDtypeStruct((b, CONV_W - 1, CONV_CH), F32),
                   jax.ShapeDtypeStruct((b, SSM_HEADS, SSM_HEAD_DIM, SSM_STATE), F32)],
        scratch_shapes=[pltpu.VMEM((CONV_PAD + tt, CONV_CH), F32),
                        pltpu.VMEM((tt, CONV_CH), F32),
                        pltpu.VMEM((tt, SSM_INNER), F32),
                        pltpu.VMEM((tt, SSM_INNER), F32),
                        pltpu.VMEM((SSM_STATE, SSM_INNER), F32)],
        compiler_params=pltpu.CompilerParams(dimension_semantics=("arbitrary", "arbitrary"),
                                             vmem_limit_bytes=VMEM_LIMIT),
        name="ssd_mixer",
    )(x, conv0, ssm0, *consts)


def _hgrn_kernel(x_ref, hg0_ref, g_ref, wh_ref, wgb_ref, lbl_ref, ghg_ref, wbr_ref, tri_ref,
                 mb_ref, hgn_ref,
                 osc, st):
    tt = x_ref.shape[0]
    n_chunks = tt // CHUNK
    ti = pl.program_id(1)
    nt = pl.num_programs(1)

    @pl.when(ti == 0)
    def _():
        for h in range(HG_HEADS):
            st[h] = hg0_ref[h].T

    xn = _rmsnorm(x_ref[...], g_ref[...]).astype(BF16)
    proj = _dot(xn, wh_ref[...])
    l0 = lbl_ref[0:1, :]
    l1 = lbl_ref[1:2, :]
    lmax = jnp.maximum(l0, l1)
    e0 = jnp.exp(l0 - lmax)
    lb = e0 / (e0 + jnp.exp(l1 - lmax))
    fr = proj[:, D_MODEL:2 * D_MODEL]
    kk = (1.0 - lb) * _sigmoid(-fr)
    v = _silu(proj[:, 2 * D_MODEL:3 * D_MODEL]).astype(BF16)
    gc = _exact_dot_lhs01(tri_ref[...], jnp.log(lb + (1.0 - lb) * _sigmoid(fr)))
    q = proj[:, :D_MODEL]
    qh = (q * jnp.exp(gc)).astype(BF16)
    mid, half = [], []
    for m in range(tt // HG_SUB):
        r = m * HG_SUB + HG_SUB // 2 - 1
        mid.append(jnp.broadcast_to(gc[r:r + 1, :], (HG_SUB, D_MODEL)))
    for c in range(n_chunks):
        r = c * CHUNK + HG_SUB - 1
        half.append(jnp.broadcast_to(gc[r:r + 1, :], (CHUNK, D_MODEL)))
    d_mid = gc - jnp.concatenate(mid, axis=0)
    d_half = gc - jnp.concatenate(half, axis=0)
    q_sub = (q * jnp.exp(d_mid)).astype(BF16)
    k_sub = (kk * jnp.exp(-d_mid)).astype(BF16)
    q_far = (q * jnp.exp(jnp.minimum(d_half, 0.0))).astype(BF16)
    k_far = (kk * jnp.exp(jnp.minimum(-d_half, 0.0))).astype(BF16)
    kt, decay = [], []
    for c in range(n_chunks):
        rows = slice(c * CHUNK, (c + 1) * CHUNK)
        glast = gc[(c + 1) * CHUNK - 1:(c + 1) * CHUNK, :]
        kt.append((kk[rows, :] * jnp.exp(glast - gc[rows, :])).astype(BF16))
        decay.append(jnp.exp(glast))

    row = lax.broadcasted_iota(jnp.int32, (tt, tt), 0)
    col = lax.broadcasted_iota(jnp.int32, (tt, tt), 1)
    near = jnp.logical_and(row >= col, row // HG_SUB == col // HG_SUB)
    far = jnp.logical_and(row // CHUNK == col // CHUNK, row // HG_SUB > col // HG_SUB)

    for h in range(HG_HEADS):
        hs = slice(h * HG_K, (h + 1) * HG_K)
        att = (jnp.where(near, _dot_nt(q_sub[:, hs], k_sub[:, hs]), 0.0)
               + jnp.where(far, _dot_nt(q_far[:, hs], k_far[:, hs]), 0.0)).astype(BF16)
        o_intra = _dot(att, v[:, hs])
        s = st[h]
        o_inter = []
        for c in range(n_chunks):
            rows = slice(c * CHUNK, (c + 1) * CHUNK)
            o_inter.append(_dot_nt(qh[rows, hs], s.astype(BF16)))
            s = decay[c][:, hs] * s + _dot_tn(v[rows, hs], kt[c][:, hs])
        st[h] = s
        osc[:, hs] = o_intra + jnp.concatenate(o_inter, axis=0)

    parts = []
    for h in range(HG_HEADS):
        blk = osc[:, h * HG_K:(h + 1) * HG_K]
        parts.append(blk * lax.rsqrt(jnp.mean(blk * blk, axis=-1, keepdims=True) + EPS))
    on = (jnp.concatenate(parts, axis=1) * ghg_ref[...] * _silu(proj[:, 3 * D_MODEL:])).astype(BF16)
    yb = _dot(on, wbr_ref[...])
    mb_ref[...] = _sigmoid(_dot(xn, wgb_ref[...])) * yb

    @pl.when(ti == nt - 1)
    def _():
        for h in range(HG_HEADS):
            hgn_ref[h] = st[h].T


def _hgrn_mixer(x, hg0, p, tt):
    b, t, d = x.shape
    grid = (b, t // tt)
    row_spec = pl.BlockSpec((None, tt, d), lambda i, j: (i, j, 0))
    st_spec = pl.BlockSpec((None, HG_HEADS, HG_K, HG_K), lambda i, j: (i, 0, 0, 0))
    consts = [p['g_mix'], p['w_hg'], p['w_gb'], p['lb_logits'], p['g_hgrn'], p['w_hgrn_branch'], _chunk_tri(tt)]
    return pl.pallas_call(
        _hgrn_kernel,
        grid=grid,
        in_specs=[row_spec, st_spec] + [_const_spec(c.shape) for c in consts],
        out_specs=[row_spec, st_spec],
        out_shape=[jax.ShapeDtypeStruct((b, t, d), F32),
                   jax.ShapeDtypeStruct((b, HG_HEADS, HG_K, HG_K), F32)],
        scratch_shapes=[pltpu.VMEM((tt, d), F32), pltpu.VMEM((HG_HEADS, HG_K, HG_K), F32)],
        compiler_params=pltpu.CompilerParams(dimension_semantics=("arbitrary", "arbitrary"),
                                             vmem_limit_bytes=VMEM_LIMIT),
        name="hgrn_mixer",
    )(x, hg0, *consts)


def _memkv_kernel(m_ref, g_ref, wk_ref, wv_ref, k_ref, v_ref):
    mn = _rmsnorm(m_ref[...], g_ref[...]).astype(BF16)
    k_ref[...] = _dot(mn, wk_ref[...])
    v_ref[...] = _dot(mn, wv_ref[...])


def _memory_kv(mem, p):
    b, n, d = mem.shape
    spec = pl.BlockSpec((None, n, d), lambda i: (i, 0, 0))
    consts = [p['g_mem'], p['w_xk'], p['w_xv']]
    return pl.pallas_call(
        _memkv_kernel,
        grid=(b,),
        in_specs=[spec] + [_const_spec(c.shape) for c in consts],
        out_specs=[spec, spec],
        out_shape=[jax.ShapeDtypeStruct((b, n, d), F32)] * 2,
        compiler_params=pltpu.CompilerParams(dimension_semantics=("arbitrary",),
                                             vmem_limit_bytes=VMEM_LIMIT),
        name="memory_kv",
    )(mem, *consts)


def _post_kernel(x_ref, ma_ref, mb_ref, mk_ref, mv_ref, cnt0_ref, wo_ref, gx_ref, wq_ref, wxo_ref,
                 gmoe_ref, wr_ref, br_ref, tril_ref,
                 h_ref, xn_ref, meta_ref, gate_ref, cnt_ref,
                 base):
    tm = x_ref.shape[0]
    first = jnp.logical_and(pl.program_id(0) == 0, pl.program_id(1) == 0)

    @pl.when(first)
    def _():
        base[...] = cnt0_ref[...]

    m = (ma_ref[...] + mb_ref[...]).astype(BF16)
    h1 = x_ref[...] + _dot(m, wo_ref[...])

    hn = _rmsnorm(h1, gx_ref[...]).astype(BF16)
    q = _dot(hn, wq_ref[...])
    heads = []
    for hh in range(X_HEADS):
        hs = slice(hh * X_HEAD_DIM, (hh + 1) * X_HEAD_DIM)
        s = _dot_nt(q[:, hs].astype(BF16), mk_ref[:, hs].astype(BF16)) * (X_HEAD_DIM ** -0.5)
        s = s - jnp.max(s, axis=-1, keepdims=True)
        e = jnp.exp(s)
        pr = e / jnp.sum(e, axis=-1, keepdims=True)
        heads.append(_dot(pr.astype(BF16), mv_ref[:, hs].astype(BF16)))
    o = jnp.concatenate(heads, axis=1).astype(BF16)
    h2 = h1 + _dot(o, wxo_ref[...])
    h_ref[...] = h2

    xn3 = _rmsnorm(h2, gmoe_ref[...])
    xn_ref[...] = _pack_bf16_pairs(xn3)
    logits = _dot(xn3.astype(BF16), wr_ref[...]) + br_ref[...]

    lane = lax.broadcasted_iota(jnp.int32, (tm, LANES), 1)
    lane_f = lane.astype(F32)
    run = logits
    vals, hots = [], []
    for _ in range(TOP_K):
        mx = jnp.max(run, axis=-1, keepdims=True)
        idx = jnp.min(jnp.where(run == mx, lane_f, float(LANES)), axis=-1, keepdims=True)
        hot = lane_f == idx
        run = jnp.where(hot, -jnp.inf, run)
        vals.append(mx)
        hots.append(hot)
    es = [jnp.exp(v - vals[0]) for v in vals]
    den = es[0] + es[1] + es[2] + es[3]

    tot = jnp.zeros((tm, LANES), F32)
    for hot in hots:
        tot = tot + hot.astype(F32)
    before = base[...] + _dot(tril_ref[...], tot.astype(BF16))
    base[...] = base[...] + jnp.sum(tot, axis=0, keepdims=True)
    cnt_ref[...] = base[...]

    meta = jnp.zeros((tm, LANES), F32)
    gates = jnp.zeros((tm, LANES), F32)
    for j in range(TOP_K):
        e_idx = jnp.sum(jnp.where(hots[j], lane_f, 0.0), axis=-1, keepdims=True)
        rank = jnp.sum(jnp.where(hots[j], before, 0.0), axis=-1, keepdims=True)
        meta = jnp.where(lane == j, e_idx, meta)
        meta = jnp.where(lane == TOP_K + j, rank, meta)
        gates = jnp.where(lane == j, es[j] / den, gates)
    meta_ref[...] = meta.astype(jnp.int32)
    gate_ref[...] = gates


def _post_mixer(x, ma, mb, mk, mv, cnt0, p, tm):
    b, t, d = x.shape
    grid = (b, t // tm)
    row_spec = pl.BlockSpec((None, tm, d), lambda i, j: (i, j, 0))
    small_spec = pl.BlockSpec((None, tm, LANES), lambda i, j: (i, j, 0))
    mem_spec = pl.BlockSpec((None, N_MEM, d), lambda i, j: (i, 0, 0))
    cnt_spec = pl.BlockSpec((1, LANES), lambda i, j: (0, 0))
    tril = jnp.tril(jnp.ones((tm, tm), BF16), -1)
    consts = [p['w_out'], p['g_xattn'], p['w_xq'], p['w_xo'], p['g_moe'], p['w_router'], p['b_router'], tril]
    return pl.pallas_call(
        _post_kernel,
        grid=grid,
        in_specs=[row_spec, row_spec, row_spec, mem_spec, mem_spec, cnt_spec]
                 + [_const_spec(c.shape) for c in consts],
        out_specs=[row_spec, pl.BlockSpec((None, tm, d // 2), lambda i, j: (i, j, 0)), small_spec, small_spec,
                   cnt_spec],
        out_shape=[jax.ShapeDtypeStruct((b, t, d), F32),
                   jax.ShapeDtypeStruct((b, t, d // 2), jnp.int32),
                   jax.ShapeDtypeStruct((b, t, LANES), jnp.int32),
                   jax.ShapeDtypeStruct((b, t, LANES), F32),
                   jax.ShapeDtypeStruct((1, LANES), F32)],
        scratch_shapes=[pltpu.VMEM((1, LANES), F32)],
        compiler_params=pltpu.CompilerParams(dimension_semantics=("arbitrary", "arbitrary"),
                                             vmem_limit_bytes=VMEM_LIMIT),
        name="post_mixer",
    )(x, ma, mb, mk, mv, cnt0, *consts)


def _sc_chunk(rows_per_worker):
    chunk = SC_CHUNK
    while rows_per_worker % (SC_BUFFERS * chunk):
        chunk //= 2
    assert chunk % 8 == 0, rows_per_worker
    return chunk


def _sc_mesh():
    return plsc.VectorSubcoreMesh(core_axis_name="c", subcore_axis_name="s")


def _sc_worker():
    return lax.axis_index("s") * SC_CORES + lax.axis_index("c")


def _dispatch(xn, dest, n_slots):
    n, d = xn.shape
    per_w = n // SC_WORKERS
    chunk = _sc_chunk(per_w)
    n_chunks = per_w // chunk
    idx = dest.reshape(n // chunk, chunk, TOP_K).transpose(0, 2, 1)

    def body(x_hbm, idx_hbm, out_hbm, idx_v, rows_v, rsem, wsem):
        wid = _sc_worker()

        def read(i, b):
            blk = wid * n_chunks + i
            return pltpu.make_async_copy(x_hbm.at[pl.ds(blk * chunk, chunk)], rows_v.at[b], rsem.at[b])

        def read_start(i, b):
            pltpu.sync_copy(idx_hbm.at[wid * n_chunks + i], idx_v.at[b])
            read(i, b).start()

        def write(b, j):
            return pltpu.make_async_copy(rows_v.at[b], out_hbm.at[idx_v.at[b, j]], wsem.at[b])

        for b in range(SC_BUFFERS):
            read_start(b, b)

        @pl.loop(0, n_chunks, step=SC_BUFFERS)
        def _(i0):
            for b in range(SC_BUFFERS):
                i = i0 + b
                read(i, b).wait()
                for j in range(TOP_K):
                    write(b, j).start()
                for j in range(TOP_K):
                    write(b, j).wait()

                @pl.when(i + SC_BUFFERS < n_chunks)
                def _():
                    read_start(i + SC_BUFFERS, b)

    return pl.kernel(
        body, mesh=_sc_mesh(),
        out_type=jax.ShapeDtypeStruct((n_slots, d), xn.dtype),
        scratch_types=[pltpu.VMEM((SC_BUFFERS, TOP_K, chunk), jnp.int32),
                       pltpu.VMEM((SC_BUFFERS, chunk, d), xn.dtype),
                       pltpu.SemaphoreType.DMA((SC_BUFFERS,)),
                       pltpu.SemaphoreType.DMA((SC_BUFFERS,))],
        name="moe_dispatch",
    )(xn, idx)


def _gather_rows(table, idx):
    n_out = idx.shape[0]
    d = table.shape[1]
    per_w = n_out // SC_WORKERS
    chunk = _sc_chunk(per_w)
    n_chunks = per_w // chunk

    def body(table_hbm, idx_hbm, out_hbm, idx_v, rows_v, gsem, wsem):
        base = _sc_worker() * per_w

        def gather(b):
            return pltpu.make_async_copy(table_hbm.at[idx_v.at[b]], rows_v.at[b], gsem.at[b])

        def gather_start(i, b):
            pltpu.sync_copy(idx_hbm.at[pl.ds(base + i * chunk, chunk)], idx_v.at[b])
            gather(b).start()

        def write(i, b):
            return pltpu.make_async_copy(rows_v.at[b], out_hbm.at[pl.ds(base + i * chunk, chunk)], wsem.at[b])

        for b in range(SC_BUFFERS):
            gather_start(b, b)

        @pl.loop(0, n_chunks, step=SC_BUFFERS)
        def _(i0):
            for b in range(SC_BUFFERS):
                i = i0 + b
                gather(b).wait()
                write(i, b).start()
                write(i, b).wait()

                @pl.when(i + SC_BUFFERS < n_chunks)
                def _():
                    gather_start(i + SC_BUFFERS, b)

    return pl.kernel(
        body, mesh=_sc_mesh(),
        out_type=jax.ShapeDtypeStruct((n_out, d), table.dtype),
        scratch_types=[pltpu.VMEM((SC_BUFFERS, chunk), jnp.int32),
                       pltpu.VMEM((SC_BUFFERS, chunk, d), table.dtype),
                       pltpu.SemaphoreType.DMA((SC_BUFFERS,)),
                       pltpu.SemaphoreType.DMA((SC_BUFFERS,))],
        name="moe_gather",
    )(table, idx)


def _moe_kernel(be_ref, valid_ref, x_ref, wgu_ref, bgu_ref, wd_ref, bd_ref, y_ref):
    del be_ref
    row = lax.broadcasted_iota(jnp.int32, x_ref.shape, 0)
    x_hi, x_lo = _unpack_bf16_pairs(jnp.where(row < valid_ref[pl.program_id(0)], x_ref[...], 0))
    half = x_ref.shape[1]
    gu = (_dot(x_hi.astype(BF16), wgu_ref[:half, :]) + _dot(x_lo.astype(BF16), wgu_ref[half:, :])
          + bgu_ref[...])
    gate = jnp.minimum(gu[:, :D_EXPERT], SWIGLU_LIMIT)
    up = jnp.clip(gu[:, D_EXPERT:], -SWIGLU_LIMIT, SWIGLU_LIMIT)
    hmid = ((up + 1.0) * gate * _sigmoid(SWIGLU_ALPHA * gate)).astype(BF16)
    y_ref[...] = _pack_bf16_pairs(_dot(hmid, wd_ref[...]) + bd_ref[...])


def _moe_experts(slots, block_e, block_valid, p):
    n_slots, dh = slots.shape
    d = 2 * dh
    n_blocks = n_slots // MOE_ROWS
    grid_spec = pltpu.PrefetchScalarGridSpec(
        num_scalar_prefetch=2,
        grid=(n_blocks,),
        in_specs=[pl.BlockSpec((MOE_ROWS, dh), lambda i, be, bv: (i, 0)),
                  pl.BlockSpec((None, d, 2 * D_EXPERT), lambda i, be, bv: (be[i], 0, 0)),
                  pl.BlockSpec((None, 1, 2 * D_EXPERT), lambda i, be, bv: (be[i], 0, 0)),
                  pl.BlockSpec((None, D_EXPERT, d), lambda i, be, bv: (be[i], 0, 0)),
                  pl.BlockSpec((None, 1, d), lambda i, be, bv: (be[i], 0, 0))],
        out_specs=pl.BlockSpec((MOE_ROWS, dh), lambda i, be, bv: (i, 0)),
    )
    return pl.pallas_call(
        _moe_kernel,
        grid_spec=grid_spec,
        out_shape=jax.ShapeDtypeStruct((n_slots, dh), jnp.int32),
        compiler_params=pltpu.CompilerParams(dimension_semantics=("arbitrary",),
                                             vmem_limit_bytes=VMEM_LIMIT),
        name="moe_experts",
    )(block_e, block_valid, slots, p['w_gate_up'], p['b_gate_up'], p['w_down'], p['b_down'])


def _combine_kernel(h_ref, gate_ref, y_ref, gfin_ref, out_ref):
    half = y_ref.shape[2]
    acc_hi = h_ref[:, :half]
    acc_lo = h_ref[:, half:]
    gates = gate_ref[...]
    for j in range(TOP_K):
        y_hi, y_lo = _unpack_bf16_pairs(y_ref[j])
        acc_hi = acc_hi + gates[:, j:j + 1] * y_hi
        acc_lo = acc_lo + gates[:, j:j + 1] * y_lo
    out_ref[...] = _rmsnorm(jnp.concatenate([acc_hi, acc_lo], axis=1), gfin_ref[...])


def _combine(h, gates, y_tok, g_final, tm):
    n, d = h.shape
    return pl.pallas_call(
        _combine_kernel,
        grid=(n // tm,),
        in_specs=[pl.BlockSpec((tm, d), lambda i: (i, 0)),
                  pl.BlockSpec((tm, LANES), lambda i: (i, 0)),
                  pl.BlockSpec((TOP_K, tm, d // 2), lambda i: (0, i, 0)),
                  _const_spec(g_final.shape)],
        out_specs=pl.BlockSpec((tm, d), lambda i: (i, 0)),
        out_shape=jax.ShapeDtypeStruct((n, d), F32),
        compiler_params=pltpu.CompilerParams(dimension_semantics=("arbitrary",),
                                             vmem_limit_bytes=VMEM_LIMIT),
        name="moe_combine",
    )(h, gates, y_tok, g_final)


def _moe_outputs(xn, meta, counts, p):
    n, d = xn.shape
    n_rows = n * TOP_K
    n_blocks = n_rows // MOE_ROWS + N_EXPERTS
    n_slots = n_blocks * MOE_ROWS
    cnt = counts[0, :N_EXPERTS].astype(jnp.int32)
    padded = (cnt + MOE_ROWS - 1) // MOE_ROWS * MOE_ROWS
    pad_end = jnp.cumsum(padded)
    pad_start = pad_end - padded
    block_row0 = jnp.arange(n_blocks, dtype=jnp.int32) * MOE_ROWS
    block_e = jnp.minimum(jnp.sum((pad_end[None, :] <= block_row0[:, None]).astype(jnp.int32), axis=1),
                          N_EXPERTS - 1)
    block_valid = jnp.clip((pad_start + cnt)[block_e] - block_row0, 0, MOE_ROWS)
    dest = (pad_start[meta[:, :TOP_K]] + meta[:, TOP_K:2 * TOP_K]).astype(jnp.int32)
    slots = _dispatch(xn, dest, n_slots)
    y_slots = _moe_experts(slots, block_e, block_valid, p)
    return _gather_rows(y_slots, dest.T.reshape(-1)).reshape(TOP_K, n, d)


def _prepare(g_mix, w_in, conv_w, conv_b, dt_bias, a_log, d_skip, g_ssm, w_ssm_branch, lb_logits, g_hgrn,
             w_hgrn_branch, w_out, g_mem, w_xk, w_xv, g_xattn, w_xq, w_xo, g_moe, w_router, b_router,
             w_gate_up, b_gate_up, w_down, b_down):
    d = D_MODEL
    w = w_in[0]
    o_z, o_xbc = 0, SSM_INNER
    o_dt = o_xbc + CONV_CH
    o_q = o_dt + SSM_HEADS
    o_ga = o_q + 4 * d
    o_gb = o_ga + d

    def row(v):
        return v.reshape(1, -1).astype(F32)

    def lane_pad(v, fill=0.0):
        return jnp.pad(v.reshape(1, -1).astype(F32), ((0, 0), (0, LANES - v.shape[-1])), constant_values=fill)

    head_of_col = np.arange(SSM_INNER) // SSM_HEAD_DIM
    expand = (np.arange(LANES)[:, None] == head_of_col[None, :])
    return dict(
        g_mix=row(g_mix[0]),
        w_z=w[:, o_z:o_z + SSM_INNER].astype(BF16),
        w_xbc=w[:, o_xbc:o_xbc + CONV_CH].astype(BF16),
        w_dt=jnp.pad(w[:, o_dt:o_dt + SSM_HEADS], ((0, 0), (0, LANES - SSM_HEADS))).astype(BF16),
        w_hg=w[:, o_q:o_q + 4 * d].astype(BF16),
        w_ga=w[:, o_ga:o_ga + d].astype(BF16),
        w_gb=w[:, o_gb:o_gb + d].astype(BF16),
        conv_w=conv_w[0].astype(F32),
        conv_b=row(conv_b[0]),
        dt_bias=lane_pad(dt_bias[0]),
        a_log=lane_pad(a_log[0]),
        d_skip=row(jnp.repeat(d_skip[0], SSM_HEAD_DIM)),
        g_ssm=row(g_ssm[0]),
        w_ssm_branch=w_ssm_branch[0].astype(BF16),
        lb_logits=lb_logits.astype(F32),
        g_hgrn=row(g_hgrn[0]),
        w_hgrn_branch=w_hgrn_branch[0].astype(BF16),
        w_out=w_out[0].astype(BF16),
        g_mem=row(g_mem[0]),
        w_xk=w_xk[0].astype(BF16),
        w_xv=w_xv[0].astype(BF16),
        g_xattn=row(g_xattn[0]),
        w_xq=w_xq[0].astype(BF16),
        w_xo=w_xo[0].astype(BF16),
        g_moe=row(g_moe[0]),
        w_router=jnp.pad(w_router[0], ((0, 0), (0, LANES - N_EXPERTS))).astype(BF16),
        b_router=lane_pad(b_router[0], fill=-jnp.inf),
        w_gate_up=w_gate_up[0].astype(BF16),
        b_gate_up=b_gate_up[0].reshape(N_EXPERTS, 1, 2 * D_EXPERT).astype(F32),
        w_down=w_down[0].astype(BF16),
        b_down=b_down[0].reshape(N_EXPERTS, 1, d).astype(F32),
        expand=jnp.asarray(expand, BF16),
    )


def _group_to_experts(x, conv0, ssm0, hg0, mk, mv, p):
    b, t, d = x.shape
    tt = min(MIXER_TILE, t)
    ma, conv_n, ssm_n = _ssd_mixer(x, conv0, ssm0, p, tt)
    mb, hg_n = _hgrn_mixer(x, hg0, p, tt)
    tm = min(POST_TILE, t)
    cnt0 = jnp.zeros((1, LANES), F32)
    h2, xn3, meta, gates, counts = _post_mixer(x, ma, mb, mk, mv, cnt0, p, tm)
    n = b * t
    y_tok = _moe_outputs(xn3.reshape(n, d // 2), meta.reshape(n, LANES), counts, p)
    return (h2.reshape(n, d), gates.reshape(n, LANES), y_tok), (conv_n, ssm_n, hg_n)


def _group_combine(h2, gates, y_tok, g_final, shape):
    return _combine(h2, gates, y_tok, g_final, min(COMBINE_TILE, h2.shape[0])).reshape(shape)


def kernel(x_prompt, x_sample, mem_prompt, state_conv, state_ssm, state_hgrn, cache_mem_k, cache_mem_v, g_mix, w_in, conv_w, conv_b, dt_bias, a_log, d_skip, g_ssm, w_ssm_branch, lb_logits, g_hgrn, w_hgrn_branch, w_out, g_mem, w_xk, w_xv, g_xattn, w_xq, w_xo, g_moe, w_router, b_router, w_gate_up, b_gate_up, w_down, b_down, g_final):
    p = _prepare(g_mix, w_in, conv_w, conv_b, dt_bias, a_log, d_skip, g_ssm, w_ssm_branch, lb_logits, g_hgrn,
                 w_hgrn_branch, w_out, g_mem, w_xk, w_xv, g_xattn, w_xq, w_xo, g_moe, w_router, b_router,
                 w_gate_up, b_gate_up, w_down, b_down)
    g_fin = g_final.reshape(1, -1).astype(F32)
    bp = x_prompt.shape[0]
    bs = x_sample.shape[0]
    d = D_MODEL

    mk_p, mv_p = _memory_kv(mem_prompt, p)
    moe_p, (conv_p, ssm_p, hg_p) = _group_to_experts(
        x_prompt,
        jnp.zeros((bp, CONV_W - 1, CONV_CH), F32),
        jnp.zeros((bp, SSM_HEADS, SSM_HEAD_DIM, SSM_STATE), F32),
        jnp.zeros((bp, HG_HEADS, HG_K, HG_K), F32),
        mk_p, mv_p, p)
    moe_s, (conv_s, ssm_s, hg_s) = _group_to_experts(
        x_sample, state_conv[0], state_ssm[0], state_hgrn[0],
        cache_mem_k[0].reshape(bs, N_MEM, d), cache_mem_v[0].reshape(bs, N_MEM, d), p)
    y_p = _group_combine(*moe_p, g_fin, x_prompt.shape)
    y_s = _group_combine(*moe_s, g_fin, x_sample.shape)

    kv_shape = (1, bp, N_MEM, X_HEADS, X_HEAD_DIM)
    return (y_p, y_s,
            conv_p[None], ssm_p[None], hg_p[None], mk_p.reshape(kv_shape), mv_p.reshape(kv_shape),
            conv_s[None], ssm_s[None], hg_s[None])
```

```python
import jax
import jax.numpy as jnp
import numpy as np
from jax import lax
from jax.experimental import pallas as pl
from jax.experimental.pallas import tpu as pltpu
from jax.experimental.pallas import tpu_sc as plsc

F32 = jnp.float32
BF16 = jnp.bfloat16

D_MODEL = 1024
CHUNK = 64
EPS = 1e-6
SSM_INNER = 2 * D_MODEL
SSM_HEAD_DIM = 64
SSM_HEADS = SSM_INNER // SSM_HEAD_DIM
SSM_GROUPS = 4
SSM_STATE = 128
GROUP_COLS = SSM_INNER // SSM_GROUPS
CONV_W = 4
BC_COLS = SSM_GROUPS * SSM_STATE
CONV_CH = SSM_INNER + 2 * BC_COLS
HG_HEADS = 8
HG_K = D_MODEL // HG_HEADS
HG_SUB = CHUNK // 2
N_MEM = 256
X_HEADS = 4
X_HEAD_DIM = D_MODEL // X_HEADS
N_EXPERTS = 32
TOP_K = 4
D_EXPERT = D_MODEL
SWIGLU_LIMIT = 7.0
SWIGLU_ALPHA = 1.702

LANES = 128
CONV_PAD = 8
MOE_ROWS_MAX = 512
MOE_ROWS_MIN = 128
MIXER_TILE = 256
POST_TILE = 512
COMBINE_TILE = 512
VMEM_LIMIT = 56 * 1024 * 1024
SC_CORES = 2
SC_SUBCORES = 16
SC_WORKERS = SC_CORES * SC_SUBCORES
SC_CHUNK = 64
SC_BUFFERS = 2


def _const_spec(shape):
    nd = len(shape)
    return pl.BlockSpec(shape, lambda *_: (0,) * nd, pipeline_mode=pl.Buffered(1))


def _dot(a, b):
    return jnp.dot(a, b, preferred_element_type=F32)


def _dot_nt(a, b):
    return lax.dot_general(a, b, (((1,), (1,)), ((), ())), preferred_element_type=F32)


def _dot_tn(a, b):
    return lax.dot_general(a, b, (((0,), (0,)), ((), ())), preferred_element_type=F32)


def _split3(a):
    hi = a.astype(BF16)
    r1 = a - hi.astype(F32)
    mid = r1.astype(BF16)
    lo = (r1 - mid.astype(F32)).astype(BF16)
    return hi, mid, lo


def _exact_dot_lhs01(sel, a):
    hi, mid, lo = _split3(a)
    return _dot(sel, hi) + _dot(sel, mid) + _dot(sel, lo)


def _rmsnorm(x, g):
    return x * lax.rsqrt(jnp.mean(x * x, axis=-1, keepdims=True) + EPS) * g


def _sigmoid(x):
    return jax.nn.sigmoid(x)


def _silu(x):
    return x * jax.nn.sigmoid(x)


def _softplus(x):
    return jnp.maximum(x, 0.0) + jnp.log1p(jnp.exp(-jnp.abs(x)))


def _pack_bf16_pairs(x):
    c = x.shape[1] // 2
    hi = lax.bitcast_convert_type(x[:, :c].astype(BF16).astype(F32), jnp.int32)
    lo = lax.bitcast_convert_type(x[:, c:].astype(BF16).astype(F32), jnp.int32)
    return hi | lax.shift_right_logical(lo, 16)


def _unpack_bf16_pairs(u):
    hi = lax.bitcast_convert_type(u & jnp.int32(-65536), F32)
    lo = lax.bitcast_convert_type(lax.shift_left(u, 16), F32)
    return hi, lo


def _ssd_kernel(x_ref, conv0_ref, ssm0_ref, g_ref, wxbc_ref, wz_ref, wdt_ref, wga_ref, convw_ref,
                convb_ref, dtb_ref, alog_ref, dskip_ref, gssm_ref, wbr_ref, tri_ref, expand_ref,
                ma_ref, convn_ref, ssmn_ref,
                xpad, xact, xdt, ysc, st):
    tt = x_ref.shape[0]
    ti = pl.program_id(1)
    nt = pl.num_programs(1)

    @pl.when(ti == 0)
    def _():
        xpad[CONV_PAD - (CONV_W - 1):CONV_PAD, :] = conv0_ref[...]
        st[...] = ssm0_ref[...].reshape(SSM_INNER, SSM_STATE).T

    xn = _rmsnorm(x_ref[...], g_ref[...]).astype(BF16)
    xpad[CONV_PAD:CONV_PAD + tt, :] = _dot(xn, wxbc_ref[...])

    conv = convb_ref[...] + xpad[CONV_PAD:CONV_PAD + tt, :] * convw_ref[CONV_W - 1:CONV_W, :]
    for k in range(1, CONV_W):
        conv = conv + xpad[CONV_PAD - k:CONV_PAD - k + tt, :] * convw_ref[CONV_W - 1 - k:CONV_W - k, :]
    xact[...] = _silu(conv)
    tail = xpad[CONV_PAD + tt - (CONV_W - 1):CONV_PAD + tt, :]
    xpad[CONV_PAD - (CONV_W - 1):CONV_PAD, :] = tail

    dt = _softplus(_dot(xn, wdt_ref[...]) + dtb_ref[...])
    acum_all = _exact_dot_lhs01(tri_ref[...], dt * -jnp.exp(alog_ref[...]))
    dt_hi = dt.astype(BF16)
    dt_lo = (dt - dt_hi.astype(F32)).astype(BF16)
    dt_x = _dot(dt_hi, expand_ref[...]) + _dot(dt_lo, expand_ref[...])
    xdt[...] = xact[:, :SSM_INNER] * dt_x

    lane = lax.broadcasted_iota(jnp.int32, (CHUNK, LANES), 1)
    row = lax.broadcasted_iota(jnp.int32, (CHUNK, LANES), 0)
    causal2 = row >= (lane % CHUNK)
    diag2 = row == (lane % CHUNK)
    left = lane < CHUNK
    pair_of_lane = lane // CHUNK

    for c in range(tt // CHUNK):
        rows = slice(c * CHUNK, (c + 1) * CHUNK)
        acum = acum_all[rows, :]
        for g in range(SSM_GROUPS):
            gs = slice(g * GROUP_COLS, (g + 1) * GROUP_COLS)
            b_g = xact[rows, SSM_INNER + g * SSM_STATE:SSM_INNER + (g + 1) * SSM_STATE].astype(BF16)
            c_g = xact[rows, SSM_INNER + BC_COLS + g * SSM_STATE:
                       SSM_INNER + BC_COLS + (g + 1) * SSM_STATE].astype(BF16)
            cb2 = _dot_nt(c_g, jnp.concatenate([b_g, b_g], axis=0))
            st_g = st[:, gs]
            y_inter = _dot(c_g, st_g.astype(BF16))
            xw, decay = [], []
            for j in range(GROUP_COLS // LANES):
                pair = g * (GROUP_COLS // LANES) + j
                ps = slice(pair * LANES, (pair + 1) * LANES)
                a_col = jnp.take_along_axis(acum, pair_of_lane + 2 * pair, axis=1)
                a_row = jnp.sum(jnp.where(diag2, a_col, 0.0), axis=0, keepdims=True)
                a_last = a_col[CHUNK - 1:CHUNK, :]
                dec = jnp.where(causal2, jnp.exp(jnp.minimum(a_col - a_row, 0.0)), 0.0)
                wts = (cb2 * dec).astype(BF16)
                xp = xdt[rows, ps]
                xbd = jnp.concatenate([jnp.where(left, xp, 0.0), jnp.where(left, 0.0, xp)],
                                      axis=0).astype(BF16)
                ysc[rows, ps] = _dot(wts, xbd) + y_inter[:, j * LANES:(j + 1) * LANES] * jnp.exp(a_col)
                xw.append((jnp.exp(a_last - a_col) * xp).astype(BF16))
                decay.append(jnp.exp(a_last))
            st[:, gs] = jnp.concatenate(decay, axis=1) * st_g + _dot_tn(b_g, jnp.concatenate(xw, axis=1))

    xs = xact[:, :SSM_INNER]
    y = ysc[...] + dskip_ref[...] * xs
    yz = y * _silu(_dot(xn, wz_ref[...]))
    parts = []
    for g in range(SSM_GROUPS):
        blk = yz[:, g * GROUP_COLS:(g + 1) * GROUP_COLS]
        parts.append(blk * lax.rsqrt(jnp.mean(blk * blk, axis=-1, keepdims=True) + EPS))
    yn = (jnp.concatenate(parts, axis=1) * gssm_ref[...]).astype(BF16)
    ya = _dot(yn, wbr_ref[...])
    ma_ref[...] = _sigmoid(_dot(xn, wga_ref[...])) * ya

    @pl.when(ti == nt - 1)
    def _():
        convn_ref[...] = tail
        ssmn_ref[...] = st[...].T.reshape(SSM_HEADS, SSM_HEAD_DIM, SSM_STATE)


def _chunk_tri(tt):
    return jnp.asarray(np.kron(np.eye(tt // CHUNK), np.tril(np.ones((CHUNK, CHUNK)))), BF16)


def _ssd_mixer(x, conv0, ssm0, p, tt):
    b, t, d = x.shape
    grid = (b, t // tt)
    row_spec = pl.BlockSpec((None, tt, d), lambda i, j: (i, j, 0))
    consts = [p['g_mix'], p['w_xbc'], p['w_z'], p['w_dt'], p['w_ga'], p['conv_w'], p['conv_b'], p['dt_bias'],
              p['a_log'], p['d_skip'], p['g_ssm'], p['w_ssm_branch'], _chunk_tri(tt), p['expand']]
    return pl.pallas_call(
        _ssd_kernel,
        grid=grid,
        in_specs=[row_spec,
                  pl.BlockSpec((None, CONV_W - 1, CONV_CH), lambda i, j: (i, 0, 0)),
                  pl.BlockSpec((None, SSM_HEADS, SSM_HEAD_DIM, SSM_STATE), lambda i, j: (i, 0, 0, 0))]
                 + [_const_spec(c.shape) for c in consts],
        out_specs=[row_spec,
                   pl.BlockSpec((None, CONV_W - 1, CONV_CH), lambda i, j: (i, 0, 0)),
                   pl.BlockSpec((None, SSM_HEADS, SSM_HEAD_DIM, SSM_STATE), lambda i, j: (i, 0, 0, 0))],
        out_shape=[jax.ShapeDtypeStruct((b, t, d), F32),
                   jax.ShapeDtypeStruct((b, CONV_W - 1, CONV_CH), F32),
                   jax.ShapeDtypeStruct((b, SSM_HEADS, SSM_HEAD_DIM, SSM_STATE), F32)],
        scratch_shapes=[pltpu.VMEM((CONV_PAD + tt, CONV_CH), F32),
                        pltpu.VMEM((tt, CONV_CH), F32),
                        pltpu.VMEM((tt, SSM_INNER), F32),
                        pltpu.VMEM((tt, SSM_INNER), F32),
                        pltpu.VMEM((SSM_STATE, SSM_INNER), F32)],
        compiler_params=pltpu.CompilerParams(dimension_semantics=("arbitrary", "arbitrary"),
                                             vmem_limit_bytes=VMEM_LIMIT),
        name="ssd_mixer",
    )(x, conv0, ssm0, *consts)


def _hgrn_kernel(x_ref, hg0_ref, g_ref, wh_ref, wgb_ref, lbl_ref, ghg_ref, wbr_ref, tri_ref,
                 mb_ref, hgn_ref,
                 osc, st):
    tt = x_ref.shape[0]
    n_chunks = tt // CHUNK
    ti = pl.program_id(1)
    nt = pl.num_programs(1)

    @pl.when(ti == 0)
    def _():
        for h in range(HG_HEADS):
            st[h] = hg0_ref[h].T

    xn = _rmsnorm(x_ref[...], g_ref[...]).astype(BF16)
    proj = _dot(xn, wh_ref[...])
    l0 = lbl_ref[0:1, :]
    l1 = lbl_ref[1:2, :]
    lmax = jnp.maximum(l0, l1)
    e0 = jnp.exp(l0 - lmax)
    lb = e0 / (e0 + jnp.exp(l1 - lmax))
    fr = proj[:, D_MODEL:2 * D_MODEL]
    kk = (1.0 - lb) * _sigmoid(-fr)
    v = _silu(proj[:, 2 * D_MODEL:3 * D_MODEL]).astype(BF16)
    gc = _exact_dot_lhs01(tri_ref[...], jnp.log(lb + (1.0 - lb) * _sigmoid(fr)))
    q = proj[:, :D_MODEL]
    qh = (q * jnp.exp(gc)).astype(BF16)
    mid = []
    for m in range(tt // HG_SUB):
        r = m * HG_SUB + HG_SUB // 2 - 1
        mid.append(jnp.broadcast_to(gc[r:r + 1, :], (HG_SUB, D_MODEL)))
    d_mid = gc - jnp.concatenate(mid, axis=0)
    q_sub = (q * jnp.exp(d_mid)).astype(BF16)
    k_sub = (kk * jnp.exp(-d_mid)).astype(BF16)
    q_far, k_far, v_first = [], [], []
    for c in range(n_chunks):
        first = slice(c * CHUNK, c * CHUNK + HG_SUB)
        second = slice(c * CHUNK + HG_SUB, (c + 1) * CHUNK)
        g_half = gc[c * CHUNK + HG_SUB - 1:c * CHUNK + HG_SUB, :]
        q_far.append(q[second, :] * jnp.exp(gc[second, :] - g_half))
        k_far.append(kk[first, :] * jnp.exp(g_half - gc[first, :]))
        v_first.append(v[first, :])
    q_far = jnp.concatenate(q_far, axis=0).astype(BF16)
    k_far = jnp.concatenate(k_far, axis=0).astype(BF16)
    v_first = jnp.concatenate(v_first, axis=0)
    kt, decay = [], []
    for c in range(n_chunks):
        rows = slice(c * CHUNK, (c + 1) * CHUNK)
        glast = gc[(c + 1) * CHUNK - 1:(c + 1) * CHUNK, :]
        kt.append((kk[rows, :] * jnp.exp(glast - gc[rows, :])).astype(BF16))
        decay.append(jnp.exp(glast))

    row = lax.broadcasted_iota(jnp.int32, (tt, tt), 0)
    col = lax.broadcasted_iota(jnp.int32, (tt, tt), 1)
    near = jnp.logical_and(row >= col, row // HG_SUB == col // HG_SUB)
    same_chunk = (row // HG_SUB == col // HG_SUB)[:tt // 2, :tt // 2]
    no_far = jnp.zeros((HG_SUB, HG_K), F32)

    for h in range(HG_HEADS):
        hs = slice(h * HG_K, (h + 1) * HG_K)
        att = jnp.where(near, _dot_nt(q_sub[:, hs], k_sub[:, hs]), 0.0).astype(BF16)
        att_far = jnp.where(same_chunk, _dot_nt(q_far[:, hs], k_far[:, hs]), 0.0).astype(BF16)
        o_far = _dot(att_far, v_first[:, hs])
        far_rows = []
        for c in range(n_chunks):
            far_rows += [no_far, o_far[c * HG_SUB:(c + 1) * HG_SUB, :]]
        o_intra = _dot(att, v[:, hs]) + jnp.concatenate(far_rows, axis=0)
        s = st[h]
        o_inter = []
        for c in range(n_chunks):
            rows = slice(c * CHUNK, (c + 1) * CHUNK)
            o_inter.append(_dot_nt(qh[rows, hs], s.astype(BF16)))
            s = decay[c][:, hs] * s + _dot_tn(v[rows, hs], kt[c][:, hs])
        st[h] = s
        osc[:, hs] = o_intra + jnp.concatenate(o_inter, axis=0)

    parts = []
    for h in range(HG_HEADS):
        blk = osc[:, h * HG_K:(h + 1) * HG_K]
        parts.append(blk * lax.rsqrt(jnp.mean(blk * blk, axis=-1, keepdims=True) + EPS))
    on = (jnp.concatenate(parts, axis=1) * ghg_ref[...] * _silu(proj[:, 3 * D_MODEL:])).astype(BF16)
    yb = _dot(on, wbr_ref[...])
    mb_ref[...] = _sigmoid(_dot(xn, wgb_ref[...])) * yb

    @pl.when(ti == nt - 1)
    def _():
        for h in range(HG_HEADS):
            hgn_ref[h] = st[h].T


def _hgrn_mixer(x, hg0, p, tt):
    b, t, d = x.shape
    grid = (b, t // tt)
    row_spec = pl.BlockSpec((None, tt, d), lambda i, j: (i, j, 0))
    st_spec = pl.BlockSpec((None, HG_HEADS, HG_K, HG_K), lambda i, j: (i, 0, 0, 0))
    consts = [p['g_mix'], p['w_hg'], p['w_gb'], p['lb_logits'], p['g_hgrn'], p['w_hgrn_branch'], _chunk_tri(tt)]
    return pl.pallas_call(
        _hgrn_kernel,
        grid=grid,
        in_specs=[row_spec, st_spec] + [_const_spec(c.shape) for c in consts],
        out_specs=[row_spec, st_spec],
        out_shape=[jax.ShapeDtypeStruct((b, t, d), F32),
                   jax.ShapeDtypeStruct((b, HG_HEADS, HG_K, HG_K), F32)],
        scratch_shapes=[pltpu.VMEM((tt, d), F32), pltpu.VMEM((HG_HEADS, HG_K, HG_K), F32)],
        compiler_params=pltpu.CompilerParams(dimension_semantics=("arbitrary", "arbitrary"),
                                             vmem_limit_bytes=VMEM_LIMIT),
        name="hgrn_mixer",
    )(x, hg0, *consts)


def _memkv_kernel(m_ref, g_ref, wk_ref, wv_ref, k_ref, v_ref):
    mn = _rmsnorm(m_ref[...], g_ref[...]).astype(BF16)
    k_ref[...] = _dot(mn, wk_ref[...])
    v_ref[...] = _dot(mn, wv_ref[...])


def _memory_kv(mem, p):
    b, n, d = mem.shape
    spec = pl.BlockSpec((None, n, d), lambda i: (i, 0, 0))
    consts = [p['g_mem'], p['w_xk'], p['w_xv']]
    return pl.pallas_call(
        _memkv_kernel,
        grid=(b,),
        in_specs=[spec] + [_const_spec(c.shape) for c in consts],
        out_specs=[spec, spec],
        out_shape=[jax.ShapeDtypeStruct((b, n, d), F32)] * 2,
        compiler_params=pltpu.CompilerParams(dimension_semantics=("arbitrary",),
                                             vmem_limit_bytes=VMEM_LIMIT),
        name="memory_kv",
    )(mem, *consts)


def _post_kernel(x_ref, ma_ref, mb_ref, mk_ref, mv_ref, cnt0_ref, wo_ref, gx_ref, wq_ref, wxo_ref,
                 gmoe_ref, wr_ref, br_ref, tril_ref,
                 h_ref, xn_ref, meta_ref, gate_ref, cnt_ref,
                 base):
    tm = x_ref.shape[0]
    first = jnp.logical_and(pl.program_id(0) == 0, pl.program_id(1) == 0)

    @pl.when(first)
    def _():
        base[...] = cnt0_ref[...]

    m = (ma_ref[...] + mb_ref[...]).astype(BF16)
    h1 = x_ref[...] + _dot(m, wo_ref[...])

    hn = _rmsnorm(h1, gx_ref[...]).astype(BF16)
    q = _dot(hn, wq_ref[...])
    heads = []
    for hh in range(X_HEADS):
        hs = slice(hh * X_HEAD_DIM, (hh + 1) * X_HEAD_DIM)
        s = _dot_nt(q[:, hs].astype(BF16), mk_ref[:, hs].astype(BF16)) * (X_HEAD_DIM ** -0.5)
        s = s - jnp.max(s, axis=-1, keepdims=True)
        e = jnp.exp(s)
        pr = e / jnp.sum(e, axis=-1, keepdims=True)
        heads.append(_dot(pr.astype(BF16), mv_ref[:, hs].astype(BF16)))
    o = jnp.concatenate(heads, axis=1).astype(BF16)
    h2 = h1 + _dot(o, wxo_ref[...])
    h_ref[...] = h2

    xn3 = _rmsnorm(h2, gmoe_ref[...])
    xn_ref[...] = _pack_bf16_pairs(xn3)
    logits = _dot(xn3.astype(BF16), wr_ref[...]) + br_ref[...]

    lane = lax.broadcasted_iota(jnp.int32, (tm, LANES), 1)
    lane_f = lane.astype(F32)
    run = logits
    vals, hots = [], []
    for _ in range(TOP_K):
        mx = jnp.max(run, axis=-1, keepdims=True)
        idx = jnp.min(jnp.where(run == mx, lane_f, float(LANES)), axis=-1, keepdims=True)
        hot = lane_f == idx
        run = jnp.where(hot, -jnp.inf, run)
        vals.append(mx)
        hots.append(hot)
    es = [jnp.exp(v - vals[0]) for v in vals]
    den = es[0] + es[1] + es[2] + es[3]

    tot = jnp.zeros((tm, LANES), F32)
    for hot in hots:
        tot = tot + hot.astype(F32)
    before = base[...] + _dot(tril_ref[...], tot.astype(BF16))
    base[...] = base[...] + jnp.sum(tot, axis=0, keepdims=True)
    cnt_ref[...] = base[...]

    meta = jnp.zeros((tm, LANES), F32)
    gates = jnp.zeros((tm, LANES), F32)
    for j in range(TOP_K):
        e_idx = jnp.sum(jnp.where(hots[j], lane_f, 0.0), axis=-1, keepdims=True)
        rank = jnp.sum(jnp.where(hots[j], before, 0.0), axis=-1, keepdims=True)
        meta = jnp.where(lane == j, e_idx, meta)
        meta = jnp.where(lane == TOP_K + j, rank, meta)
        gates = jnp.where(lane == j, es[j] / den, gates)
    meta_ref[...] = meta.astype(jnp.int32)
    gate_ref[...] = gates


def _post_mixer(x, ma, mb, mk, mv, cnt0, p, tm):
    b, t, d = x.shape
    grid = (b, t // tm)
    row_spec = pl.BlockSpec((None, tm, d), lambda i, j: (i, j, 0))
    small_spec = pl.BlockSpec((None, tm, LANES), lambda i, j: (i, j, 0))
    mem_spec = pl.BlockSpec((None, N_MEM, d), lambda i, j: (i, 0, 0))
    cnt_spec = pl.BlockSpec((1, LANES), lambda i, j: (0, 0))
    tril = jnp.tril(jnp.ones((tm, tm), BF16), -1)
    consts = [p['w_out'], p['g_xattn'], p['w_xq'], p['w_xo'], p['g_moe'], p['w_router'], p['b_router'], tril]
    return pl.pallas_call(
        _post_kernel,
        grid=grid,
        in_specs=[row_spec, row_spec, row_spec, mem_spec, mem_spec, cnt_spec]
                 + [_const_spec(c.shape) for c in consts],
        out_specs=[row_spec, pl.BlockSpec((None, tm, d // 2), lambda i, j: (i, j, 0)), small_spec, small_spec,
                   cnt_spec],
        out_shape=[jax.ShapeDtypeStruct((b, t, d), F32),
                   jax.ShapeDtypeStruct((b, t, d // 2), jnp.int32),
                   jax.ShapeDtypeStruct((b, t, LANES), jnp.int32),
                   jax.ShapeDtypeStruct((b, t, LANES), F32),
                   jax.ShapeDtypeStruct((1, LANES), F32)],
        scratch_shapes=[pltpu.VMEM((1, LANES), F32)],
        compiler_params=pltpu.CompilerParams(dimension_semantics=("arbitrary", "arbitrary"),
                                             vmem_limit_bytes=VMEM_LIMIT),
        name="post_mixer",
    )(x, ma, mb, mk, mv, cnt0, *consts)


def _sc_chunk(rows_per_worker):
    chunk = SC_CHUNK
    while rows_per_worker % (SC_BUFFERS * chunk):
        chunk //= 2
    assert chunk % 8 == 0, rows_per_worker
    return chunk


def _sc_mesh():
    return plsc.VectorSubcoreMesh(core_axis_name="c", subcore_axis_name="s")


def _sc_worker():
    return lax.axis_index("s") * SC_CORES + lax.axis_index("c")


def _dispatch(xn, dest, n_slots):
    n, d = xn.shape
    per_w = n // SC_WORKERS
    chunk = _sc_chunk(per_w)
    n_chunks = per_w // chunk
    idx = dest.reshape(n // chunk, chunk, TOP_K).transpose(0, 2, 1)

    def body(x_hbm, idx_hbm, out_hbm, idx_v, rows_v, rsem, wsem):
        wid = _sc_worker()

        def read(i, b):
            blk = wid * n_chunks + i
            return pltpu.make_async_copy(x_hbm.at[pl.ds(blk * chunk, chunk)], rows_v.at[b], rsem.at[b])

        def read_start(i, b):
            pltpu.sync_copy(idx_hbm.at[wid * n_chunks + i], idx_v.at[b])
            read(i, b).start()

        def write(b, j):
            return pltpu.make_async_copy(rows_v.at[b], out_hbm.at[idx_v.at[b, j]], wsem.at[b])

        for b in range(SC_BUFFERS):
            read_start(b, b)

        @pl.loop(0, n_chunks, step=SC_BUFFERS)
        def _(i0):
            for b in range(SC_BUFFERS):
                i = i0 + b
                read(i, b).wait()
                for j in range(TOP_K):
                    write(b, j).start()
                for j in range(TOP_K):
                    write(b, j).wait()

                @pl.when(i + SC_BUFFERS < n_chunks)
                def _():
                    read_start(i + SC_BUFFERS, b)

    return pl.kernel(
        body, mesh=_sc_mesh(),
        out_type=jax.ShapeDtypeStruct((n_slots, d), xn.dtype),
        scratch_types=[pltpu.VMEM((SC_BUFFERS, TOP_K, chunk), jnp.int32),
                       pltpu.VMEM((SC_BUFFERS, chunk, d), xn.dtype),
                       pltpu.SemaphoreType.DMA((SC_BUFFERS,)),
                       pltpu.SemaphoreType.DMA((SC_BUFFERS,))],
        name="moe_dispatch",
    )(xn, idx)


def _gather_rows(table, idx):
    n_out = idx.shape[0]
    d = table.shape[1]
    per_w = n_out // SC_WORKERS
    chunk = _sc_chunk(per_w)
    n_chunks = per_w // chunk

    def body(table_hbm, idx_hbm, out_hbm, idx_v, rows_v, gsem, wsem):
        base = _sc_worker() * per_w

        def gather(b):
            return pltpu.make_async_copy(table_hbm.at[idx_v.at[b]], rows_v.at[b], gsem.at[b])

        def gather_start(i, b):
            pltpu.sync_copy(idx_hbm.at[pl.ds(base + i * chunk, chunk)], idx_v.at[b])
            gather(b).start()

        def write(i, b):
            return pltpu.make_async_copy(rows_v.at[b], out_hbm.at[pl.ds(base + i * chunk, chunk)], wsem.at[b])

        for b in range(SC_BUFFERS):
            gather_start(b, b)

        @pl.loop(0, n_chunks, step=SC_BUFFERS)
        def _(i0):
            for b in range(SC_BUFFERS):
                i = i0 + b
                gather(b).wait()
                write(i, b).start()
                write(i, b).wait()

                @pl.when(i + SC_BUFFERS < n_chunks)
                def _():
                    gather_start(i + SC_BUFFERS, b)

    return pl.kernel(
        body, mesh=_sc_mesh(),
        out_type=jax.ShapeDtypeStruct((n_out, d), table.dtype),
        scratch_types=[pltpu.VMEM((SC_BUFFERS, chunk), jnp.int32),
                       pltpu.VMEM((SC_BUFFERS, chunk, d), table.dtype),
                       pltpu.SemaphoreType.DMA((SC_BUFFERS,)),
                       pltpu.SemaphoreType.DMA((SC_BUFFERS,))],
        name="moe_gather",
    )(table, idx)


def _moe_kernel(be_ref, valid_ref, x_ref, wgu_ref, bgu_ref, wd_ref, bd_ref, y_ref):
    del be_ref
    valid = valid_ref[pl.program_id(0)]

    @pl.when(valid > 0)
    def _():
        row = lax.broadcasted_iota(jnp.int32, x_ref.shape, 0)
        x_hi, x_lo = _unpack_bf16_pairs(jnp.where(row < valid, x_ref[...], 0))
        half = x_ref.shape[1]
        gu = (_dot(x_hi.astype(BF16), wgu_ref[:half, :]) + _dot(x_lo.astype(BF16), wgu_ref[half:, :])
              + bgu_ref[...])
        gate = jnp.minimum(gu[:, :D_EXPERT], SWIGLU_LIMIT)
        up = jnp.clip(gu[:, D_EXPERT:], -SWIGLU_LIMIT, SWIGLU_LIMIT)
        hmid = ((up + 1.0) * gate * _sigmoid(SWIGLU_ALPHA * gate)).astype(BF16)
        y_ref[...] = _pack_bf16_pairs(_dot(hmid, wd_ref[...]) + bd_ref[...])


def _moe_experts(slots, block_e, block_valid, p):
    n_slots, dh = slots.shape
    d = 2 * dh
    n_blocks = block_e.shape[0]
    block_rows = n_slots // n_blocks
    grid_spec = pltpu.PrefetchScalarGridSpec(
        num_scalar_prefetch=2,
        grid=(n_blocks,),
        in_specs=[pl.BlockSpec((block_rows, dh), lambda i, be, bv: (i, 0)),
                  pl.BlockSpec((None, d, 2 * D_EXPERT), lambda i, be, bv: (be[i], 0, 0)),
                  pl.BlockSpec((None, 1, 2 * D_EXPERT), lambda i, be, bv: (be[i], 0, 0)),
                  pl.BlockSpec((None, D_EXPERT, d), lambda i, be, bv: (be[i], 0, 0)),
                  pl.BlockSpec((None, 1, d), lambda i, be, bv: (be[i], 0, 0))],
        out_specs=pl.BlockSpec((block_rows, dh), lambda i, be, bv: (i, 0)),
    )
    return pl.pallas_call(
        _moe_kernel,
        grid_spec=grid_spec,
        out_shape=jax.ShapeDtypeStruct((n_slots, dh), jnp.int32),
        compiler_params=pltpu.CompilerParams(dimension_semantics=("arbitrary",),
                                             vmem_limit_bytes=VMEM_LIMIT),
        name="moe_experts",
    )(block_e, block_valid, slots, p['w_gate_up'], p['b_gate_up'], p['w_down'], p['b_down'])


def _combine_kernel(h_ref, gate_ref, y_ref, gfin_ref, out_ref):
    half = y_ref.shape[2]
    acc_hi = h_ref[:, :half]
    acc_lo = h_ref[:, half:]
    gates = gate_ref[...]
    for j in range(TOP_K):
        y_hi, y_lo = _unpack_bf16_pairs(y_ref[j])
        acc_hi = acc_hi + gates[:, j:j + 1] * y_hi
        acc_lo = acc_lo + gates[:, j:j + 1] * y_lo
    out_ref[...] = _rmsnorm(jnp.concatenate([acc_hi, acc_lo], axis=1), gfin_ref[...])


def _combine(h, gates, y_tok, g_final, tm):
    n, d = h.shape
    return pl.pallas_call(
        _combine_kernel,
        grid=(n // tm,),
        in_specs=[pl.BlockSpec((tm, d), lambda i: (i, 0)),
                  pl.BlockSpec((tm, LANES), lambda i: (i, 0)),
                  pl.BlockSpec((TOP_K, tm, d // 2), lambda i: (0, i, 0)),
                  _const_spec(g_final.shape)],
        out_specs=pl.BlockSpec((tm, d), lambda i: (i, 0)),
        out_shape=jax.ShapeDtypeStruct((n, d), F32),
        compiler_params=pltpu.CompilerParams(dimension_semantics=("arbitrary",),
                                             vmem_limit_bytes=VMEM_LIMIT),
        name="moe_combine",
    )(h, gates, y_tok, g_final)


def _moe_outputs(xn, meta, counts, p):
    n, d = xn.shape
    n_rows = n * TOP_K
    block_rows = max(MOE_ROWS_MIN, min(MOE_ROWS_MAX, n_rows // N_EXPERTS))
    n_blocks = n_rows // block_rows + N_EXPERTS
    n_slots = n_blocks * block_rows
    cnt = counts[0, :N_EXPERTS].astype(jnp.int32)
    padded = (cnt + block_rows - 1) // block_rows * block_rows
    pad_end = jnp.cumsum(padded)
    pad_start = pad_end - padded
    block_row0 = jnp.arange(n_blocks, dtype=jnp.int32) * block_rows
    block_e = jnp.minimum(jnp.sum((pad_end[None, :] <= block_row0[:, None]).astype(jnp.int32), axis=1),
                          N_EXPERTS - 1)
    block_valid = jnp.clip((pad_start + cnt)[block_e] - block_row0, 0, block_rows)
    dest = (pad_start[meta[:, :TOP_K]] + meta[:, TOP_K:2 * TOP_K]).astype(jnp.int32)
    slots = _dispatch(xn, dest, n_slots)
    y_slots = _moe_experts(slots, block_e, block_valid, p)
    return _gather_rows(y_slots, dest.T.reshape(-1)).reshape(TOP_K, n, d)


def _prepare(g_mix, w_in, conv_w, conv_b, dt_bias, a_log, d_skip, g_ssm, w_ssm_branch, lb_logits, g_hgrn,
             w_hgrn_branch, w_out, g_mem, w_xk, w_xv, g_xattn, w_xq, w_xo, g_moe, w_router, b_router,
             w_gate_up, b_gate_up, w_down, b_down):
    d = D_MODEL
    w = w_in[0]
    o_z, o_xbc = 0, SSM_INNER
    o_dt = o_xbc + CONV_CH
    o_q = o_dt + SSM_HEADS
    o_ga = o_q + 4 * d
    o_gb = o_ga + d

    def row(v):
        return v.reshape(1, -1).astype(F32)

    def lane_pad(v, fill=0.0):
        return jnp.pad(v.reshape(1, -1).astype(F32), ((0, 0), (0, LANES - v.shape[-1])), constant_values=fill)

    head_of_col = np.arange(SSM_INNER) // SSM_HEAD_DIM
    expand = (np.arange(LANES)[:, None] == head_of_col[None, :])
    return dict(
        g_mix=row(g_mix[0]),
        w_z=w[:, o_z:o_z + SSM_INNER].astype(BF16),
        w_xbc=w[:, o_xbc:o_xbc + CONV_CH].astype(BF16),
        w_dt=jnp.pad(w[:, o_dt:o_dt + SSM_HEADS], ((0, 0), (0, LANES - SSM_HEADS))).astype(BF16),
        w_hg=w[:, o_q:o_q + 4 * d].astype(BF16),
        w_ga=w[:, o_ga:o_ga + d].astype(BF16),
        w_gb=w[:, o_gb:o_gb + d].astype(BF16),
        conv_w=conv_w[0].astype(F32),
        conv_b=row(conv_b[0]),
        dt_bias=lane_pad(dt_bias[0]),
        a_log=lane_pad(a_log[0]),
        d_skip=row(jnp.repeat(d_skip[0], SSM_HEAD_DIM)),
        g_ssm=row(g_ssm[0]),
        w_ssm_branch=w_ssm_branch[0].astype(BF16),
        lb_logits=lb_logits.astype(F32),
        g_hgrn=row(g_hgrn[0]),
        w_hgrn_branch=w_hgrn_branch[0].astype(BF16),
        w_out=w_out[0].astype(BF16),
        g_mem=row(g_mem[0]),
        w_xk=w_xk[0].astype(BF16),
        w_xv=w_xv[0].astype(BF16),
        g_xattn=row(g_xattn[0]),
        w_xq=w_xq[0].astype(BF16),
        w_xo=w_xo[0].astype(BF16),
        g_moe=row(g_moe[0]),
        w_router=jnp.pad(w_router[0], ((0, 0), (0, LANES - N_EXPERTS))).astype(BF16),
        b_router=lane_pad(b_router[0], fill=-jnp.inf),
        w_gate_up=w_gate_up[0].astype(BF16),
        b_gate_up=b_gate_up[0].reshape(N_EXPERTS, 1, 2 * D_EXPERT).astype(F32),
        w_down=w_down[0].astype(BF16),
        b_down=b_down[0].reshape(N_EXPERTS, 1, d).astype(F32),
        expand=jnp.asarray(expand, BF16),
    )


def _group_to_experts(x, conv0, ssm0, hg0, mk, mv, p):
    b, t, d = x.shape
    tt = min(MIXER_TILE, t)
    ma, conv_n, ssm_n = _ssd_mixer(x, conv0, ssm0, p, tt)
    mb, hg_n = _hgrn_mixer(x, hg0, p, tt)
    tm = min(POST_TILE, t)
    cnt0 = jnp.zeros((1, LANES), F32)
    h2, xn3, meta, gates, counts = _post_mixer(x, ma, mb, mk, mv, cnt0, p, tm)
    n = b * t
    y_tok = _moe_outputs(xn3.reshape(n, d // 2), meta.reshape(n, LANES), counts, p)
    return (h2.reshape(n, d), gates.reshape(n, LANES), y_tok), (conv_n, ssm_n, hg_n)


def _group_combine(h2, gates, y_tok, g_final, shape):
    return _combine(h2, gates, y_tok, g_final, min(COMBINE_TILE, h2.shape[0])).reshape(shape)


def kernel(x_prompt, x_sample, mem_prompt, state_conv, state_ssm, state_hgrn, cache_mem_k, cache_mem_v, g_mix, w_in, conv_w, conv_b, dt_bias, a_log, d_skip, g_ssm, w_ssm_branch, lb_logits, g_hgrn, w_hgrn_branch, w_out, g_mem, w_xk, w_xv, g_xattn, w_xq, w_xo, g_moe, w_router, b_router, w_gate_up, b_gate_up, w_down, b_down, g_final):
    p = _prepare(g_mix, w_in, conv_w, conv_b, dt_bias, a_log, d_skip, g_ssm, w_ssm_branch, lb_logits, g_hgrn,
                 w_hgrn_branch, w_out, g_mem, w_xk, w_xv, g_xattn, w_xq, w_xo, g_moe, w_router, b_router,
                 w_gate_up, b_gate_up, w_down, b_down)
    g_fin = g_final.reshape(1, -1).astype(F32)
    bp = x_prompt.shape[0]
    bs = x_sample.shape[0]
    d = D_MODEL

    mk_p, mv_p = _memory_kv(mem_prompt, p)
    moe_p, (conv_p, ssm_p, hg_p) = _group_to_experts(
        x_prompt,
        jnp.zeros((bp, CONV_W - 1, CONV_CH), F32),
        jnp.zeros((bp, SSM_HEADS, SSM_HEAD_DIM, SSM_STATE), F32),
        jnp.zeros((bp, HG_HEADS, HG_K, HG_K), F32),
        mk_p, mv_p, p)
    moe_s, (conv_s, ssm_s, hg_s) = _group_to_experts(
        x_sample, state_conv[0], state_ssm[0], state_hgrn[0],
        cache_mem_k[0].reshape(bs, N_MEM, d), cache_mem_v[0].reshape(bs, N_MEM, d), p)
    y_p = _group_combine(*moe_p, g_fin, x_prompt.shape)
    y_s = _group_combine(*moe_s, g_fin, x_sample.shape)

    kv_shape = (1, bp, N_MEM, X_HEADS, X_HEAD_DIM)
    return (y_p, y_s,
            conv_p[None], ssm_p[None], hg_p[None], mk_p.reshape(kv_shape), mv_p.reshape(kv_shape),
            conv_s[None], ssm_s[None], hg_s[None])
```

```python
import jax
import jax.numpy as jnp
import numpy as np
from jax import lax
from jax.experimental import pallas as pl
from jax.experimental.pallas import tpu as pltpu
from jax.experimental.pallas import tpu_sc as plsc

F32 = jnp.float32
BF16 = jnp.bfloat16

D_MODEL = 1024
CHUNK = 64
EPS = 1e-6
SSM_INNER = 2 * D_MODEL
SSM_HEAD_DIM = 64
SSM_HEADS = SSM_INNER // SSM_HEAD_DIM
SSM_GROUPS = 4
SSM_STATE = 128
GROUP_COLS = SSM_INNER // SSM_GROUPS
CONV_W = 4
BC_COLS = SSM_GROUPS * SSM_STATE
CONV_CH = SSM_INNER + 2 * BC_COLS
HG_HEADS = 8
HG_K = D_MODEL // HG_HEADS
HG_SUB = CHUNK // 2
N_MEM = 256
X_HEADS = 4
X_HEAD_DIM = D_MODEL // X_HEADS
N_EXPERTS = 32
TOP_K = 4
D_EXPERT = D_MODEL
SWIGLU_LIMIT = 7.0
SWIGLU_ALPHA = 1.702

LANES = 128
CONV_PAD = 8
MOE_ROWS_MAX = 512
MOE_ROWS_MIN = 128
SSD_TILE = 256
HGRN_TILE = 256
POST_TILE = 512
COMBINE_TILE = 512
VMEM_LIMIT = 56 * 1024 * 1024
SC_CORES = 2
SC_SUBCORES = 16
SC_WORKERS = SC_CORES * SC_SUBCORES
SC_CHUNK = 64
SC_BUFFERS = 2


def _const_spec(shape):
    nd = len(shape)
    return pl.BlockSpec(shape, lambda *_: (0,) * nd, pipeline_mode=pl.Buffered(1))


def _dot(a, b):
    return jnp.dot(a, b, preferred_element_type=F32)


def _dot_nt(a, b):
    return lax.dot_general(a, b, (((1,), (1,)), ((), ())), preferred_element_type=F32)


def _dot_tn(a, b):
    return lax.dot_general(a, b, (((0,), (0,)), ((), ())), preferred_element_type=F32)


def _split3(a):
    hi = a.astype(BF16)
    r1 = a - hi.astype(F32)
    mid = r1.astype(BF16)
    lo = (r1 - mid.astype(F32)).astype(BF16)
    return hi, mid, lo


def _exact_dot_lhs01(sel, a):
    hi, mid, lo = _split3(a)
    return _dot(sel, hi) + _dot(sel, mid) + _dot(sel, lo)


def _rmsnorm(x, g):
    return x * lax.rsqrt(jnp.mean(x * x, axis=-1, keepdims=True) + EPS) * g


def _sigmoid(x):
    return jax.nn.sigmoid(x)


def _silu(x):
    return x * jax.nn.sigmoid(x)


def _softplus(x):
    return jnp.maximum(x, 0.0) + jnp.log1p(jnp.exp(-jnp.abs(x)))


def _pack_bf16_pairs(x):
    c = x.shape[1] // 2
    hi = lax.bitcast_convert_type(x[:, :c].astype(BF16).astype(F32), jnp.int32)
    lo = lax.bitcast_convert_type(x[:, c:].astype(BF16).astype(F32), jnp.int32)
    return hi | lax.shift_right_logical(lo, 16)


def _unpack_bf16_pairs(u):
    hi = lax.bitcast_convert_type(u & jnp.int32(-65536), F32)
    lo = lax.bitcast_convert_type(lax.shift_left(u, 16), F32)
    return hi, lo


def _ssd_kernel(x_ref, conv0_ref, ssm0_ref, g_ref, wxbc_ref, wz_ref, wdt_ref, wga_ref, convw_ref,
                convb_ref, dtb_ref, alog_ref, dskip_ref, gssm_ref, wbr_ref, tri_ref, expand_ref,
                ma_ref, convn_ref, ssmn_ref,
                xpad, xact, xdt, ysc, st):
    tt = x_ref.shape[0]
    ti = pl.program_id(1)
    nt = pl.num_programs(1)

    @pl.when(ti == 0)
    def _():
        xpad[CONV_PAD - (CONV_W - 1):CONV_PAD, :] = conv0_ref[...]
        st[...] = ssm0_ref[...].reshape(SSM_INNER, SSM_STATE).T

    xn = _rmsnorm(x_ref[...], g_ref[...]).astype(BF16)
    xpad[CONV_PAD:CONV_PAD + tt, :] = _dot(xn, wxbc_ref[...])

    conv = convb_ref[...] + xpad[CONV_PAD:CONV_PAD + tt, :] * convw_ref[CONV_W - 1:CONV_W, :]
    for k in range(1, CONV_W):
        conv = conv + xpad[CONV_PAD - k:CONV_PAD - k + tt, :] * convw_ref[CONV_W - 1 - k:CONV_W - k, :]
    xact[...] = _silu(conv)
    tail = xpad[CONV_PAD + tt - (CONV_W - 1):CONV_PAD + tt, :]
    xpad[CONV_PAD - (CONV_W - 1):CONV_PAD, :] = tail

    dt = _softplus(_dot(xn, wdt_ref[...]) + dtb_ref[...])
    acum_all = _exact_dot_lhs01(tri_ref[...], dt * -jnp.exp(alog_ref[...]))
    dt_hi = dt.astype(BF16)
    dt_lo = (dt - dt_hi.astype(F32)).astype(BF16)
    dt_x = _dot(dt_hi, expand_ref[...]) + _dot(dt_lo, expand_ref[...])
    xdt[...] = xact[:, :SSM_INNER] * dt_x

    lane = lax.broadcasted_iota(jnp.int32, (CHUNK, LANES), 1)
    row = lax.broadcasted_iota(jnp.int32, (CHUNK, LANES), 0)
    causal2 = row >= (lane % CHUNK)
    diag2 = row == (lane % CHUNK)
    left = lane < CHUNK
    pair_of_lane = lane // CHUNK

    for c in range(tt // CHUNK):
        rows = slice(c * CHUNK, (c + 1) * CHUNK)
        acum = acum_all[rows, :]
        for g in range(SSM_GROUPS):
            gs = slice(g * GROUP_COLS, (g + 1) * GROUP_COLS)
            b_g = xact[rows, SSM_INNER + g * SSM_STATE:SSM_INNER + (g + 1) * SSM_STATE].astype(BF16)
            c_g = xact[rows, SSM_INNER + BC_COLS + g * SSM_STATE:
                       SSM_INNER + BC_COLS + (g + 1) * SSM_STATE].astype(BF16)
            cb2 = _dot_nt(c_g, jnp.concatenate([b_g, b_g], axis=0))
            st_g = st[:, gs]
            y_inter = _dot(c_g, st_g.astype(BF16))
            xw, decay = [], []
            for j in range(GROUP_COLS // LANES):
                pair = g * (GROUP_COLS // LANES) + j
                ps = slice(pair * LANES, (pair + 1) * LANES)
                a_col = jnp.take_along_axis(acum, pair_of_lane + 2 * pair, axis=1)
                a_row = jnp.sum(jnp.where(diag2, a_col, 0.0), axis=0, keepdims=True)
                a_last = a_col[CHUNK - 1:CHUNK, :]
                dec = jnp.where(causal2, jnp.exp(jnp.minimum(a_col - a_row, 0.0)), 0.0)
                wts = (cb2 * dec).astype(BF16)
                xp = xdt[rows, ps]
                xbd = jnp.concatenate([jnp.where(left, xp, 0.0), jnp.where(left, 0.0, xp)],
                                      axis=0).astype(BF16)
                ysc[rows, ps] = _dot(wts, xbd) + y_inter[:, j * LANES:(j + 1) * LANES] * jnp.exp(a_col)
                xw.append((jnp.exp(a_last - a_col) * xp).astype(BF16))
                decay.append(jnp.exp(a_last))
            st[:, gs] = jnp.concatenate(decay, axis=1) * st_g + _dot_tn(b_g, jnp.concatenate(xw, axis=1))

    xs = xact[:, :SSM_INNER]
    y = ysc[...] + dskip_ref[...] * xs
    yz = y * _silu(_dot(xn, wz_ref[...]))
    parts = []
    for g in range(SSM_GROUPS):
        blk = yz[:, g * GROUP_COLS:(g + 1) * GROUP_COLS]
        parts.append(blk * lax.rsqrt(jnp.mean(blk * blk, axis=-1, keepdims=True) + EPS))
    yn = (jnp.concatenate(parts, axis=1) * gssm_ref[...]).astype(BF16)
    ya = _dot(yn, wbr_ref[...])
    ma_ref[...] = _sigmoid(_dot(xn, wga_ref[...])) * ya

    @pl.when(ti == nt - 1)
    def _():
        convn_ref[...] = tail
        ssmn_ref[...] = st[...].T.reshape(SSM_HEADS, SSM_HEAD_DIM, SSM_STATE)


def _chunk_tri(tt):
    return jnp.asarray(np.kron(np.eye(tt // CHUNK), np.tril(np.ones((CHUNK, CHUNK)))), BF16)


def _ssd_mixer(x, conv0, ssm0, p, tt):
    b, t, d = x.shape
    grid = (b, t // tt)
    row_spec = pl.BlockSpec((None, tt, d), lambda i, j: (i, j, 0))
    consts = [p['g_mix'], p['w_xbc'], p['w_z'], p['w_dt'], p['w_ga'], p['conv_w'], p['conv_b'], p['dt_bias'],
              p['a_log'], p['d_skip'], p['g_ssm'], p['w_ssm_branch'], _chunk_tri(tt), p['expand']]
    return pl.pallas_call(
        _ssd_kernel,
        grid=grid,
        in_specs=[row_spec,
                  pl.BlockSpec((None, CONV_W - 1, CONV_CH), lambda i, j: (i, 0, 0)),
                  pl.BlockSpec((None, SSM_HEADS, SSM_HEAD_DIM, SSM_STATE), lambda i, j: (i, 0, 0, 0))]
                 + [_const_spec(c.shape) for c in consts],
        out_specs=[row_spec,
                   pl.BlockSpec((None, CONV_W - 1, CONV_CH), lambda i, j: (i, 0, 0)),
                   pl.BlockSpec((None, SSM_HEADS, SSM_HEAD_DIM, SSM_STATE), lambda i, j: (i, 0, 0, 0))],
        out_shape=[jax.ShapeDtypeStruct((b, t, d), F32),
                   jax.ShapeDtypeStruct((b, CONV_W - 1, CONV_CH), F32),
                   jax.ShapeDtypeStruct((b, SSM_HEADS, SSM_HEAD_DIM, SSM_STATE), F32)],
        scratch_shapes=[pltpu.VMEM((CONV_PAD + tt, CONV_CH), F32),
                        pltpu.VMEM((tt, CONV_CH), F32),
                        pltpu.VMEM((tt, SSM_INNER), F32),
                        pltpu.VMEM((tt, SSM_INNER), F32),
                        pltpu.VMEM((SSM_STATE, SSM_INNER), F32)],
        compiler_params=pltpu.CompilerParams(dimension_semantics=("arbitrary", "arbitrary"),
                                             vmem_limit_bytes=VMEM_LIMIT),
        name="ssd_mixer",
    )(x, conv0, ssm0, *consts)


def _hgrn_kernel(x_ref, hg0_ref, g_ref, wh_ref, wgb_ref, lbl_ref, ghg_ref, wbr_ref, tri_ref,
                 mb_ref, hgn_ref,
                 osc, st):
    tt = x_ref.shape[0]
    n_chunks = tt // CHUNK
    ti = pl.program_id(1)
    nt = pl.num_programs(1)

    @pl.when(ti == 0)
    def _():
        for h in range(HG_HEADS):
            st[h] = hg0_ref[h].T

    xn = _rmsnorm(x_ref[...], g_ref[...]).astype(BF16)
    proj = _dot(xn, wh_ref[...])
    l0 = lbl_ref[0:1, :]
    l1 = lbl_ref[1:2, :]
    lmax = jnp.maximum(l0, l1)
    e0 = jnp.exp(l0 - lmax)
    lb = e0 / (e0 + jnp.exp(l1 - lmax))
    fr = proj[:, D_MODEL:2 * D_MODEL]
    kk = (1.0 - lb) * _sigmoid(-fr)
    v = _silu(proj[:, 2 * D_MODEL:3 * D_MODEL]).astype(BF16)
    gc = _exact_dot_lhs01(tri_ref[...], jnp.log(lb + (1.0 - lb) * _sigmoid(fr)))
    q = proj[:, :D_MODEL]
    qh = (q * jnp.exp(gc)).astype(BF16)
    mid = []
    for m in range(tt // HG_SUB):
        r = m * HG_SUB + HG_SUB // 2 - 1
        mid.append(jnp.broadcast_to(gc[r:r + 1, :], (HG_SUB, D_MODEL)))
    d_mid = gc - jnp.concatenate(mid, axis=0)
    q_sub = (q * jnp.exp(d_mid)).astype(BF16)
    k_sub = (kk * jnp.exp(-d_mid)).astype(BF16)
    q_far, k_far, v_first = [], [], []
    for c in range(n_chunks):
        first = slice(c * CHUNK, c * CHUNK + HG_SUB)
        second = slice(c * CHUNK + HG_SUB, (c + 1) * CHUNK)
        g_half = gc[c * CHUNK + HG_SUB - 1:c * CHUNK + HG_SUB, :]
        q_far.append(q[second, :] * jnp.exp(gc[second, :] - g_half))
        k_far.append(kk[first, :] * jnp.exp(g_half - gc[first, :]))
        v_first.append(v[first, :])
    q_far = jnp.concatenate(q_far, axis=0).astype(BF16)
    k_far = jnp.concatenate(k_far, axis=0).astype(BF16)
    v_first = jnp.concatenate(v_first, axis=0)
    kt, decay = [], []
    for c in range(n_chunks):
        rows = slice(c * CHUNK, (c + 1) * CHUNK)
        glast = gc[(c + 1) * CHUNK - 1:(c + 1) * CHUNK, :]
        kt.append((kk[rows, :] * jnp.exp(glast - gc[rows, :])).astype(BF16))
        decay.append(jnp.exp(glast))

    row = lax.broadcasted_iota(jnp.int32, (tt, tt), 0)
    col = lax.broadcasted_iota(jnp.int32, (tt, tt), 1)
    near = jnp.logical_and(row >= col, row // HG_SUB == col // HG_SUB)
    same_chunk = (row // HG_SUB == col // HG_SUB)[:tt // 2, :tt // 2]
    no_far = jnp.zeros((HG_SUB, HG_K), F32)

    for h in range(HG_HEADS):
        hs = slice(h * HG_K, (h + 1) * HG_K)
        att = jnp.where(near, _dot_nt(q_sub[:, hs], k_sub[:, hs]), 0.0).astype(BF16)
        att_far = jnp.where(same_chunk, _dot_nt(q_far[:, hs], k_far[:, hs]), 0.0).astype(BF16)
        o_far = _dot(att_far, v_first[:, hs])
        far_rows = []
        for c in range(n_chunks):
            far_rows += [no_far, o_far[c * HG_SUB:(c + 1) * HG_SUB, :]]
        o_intra = _dot(att, v[:, hs]) + jnp.concatenate(far_rows, axis=0)
        s = st[h]
        o_inter = []
        for c in range(n_chunks):
            rows = slice(c * CHUNK, (c + 1) * CHUNK)
            o_inter.append(_dot_nt(qh[rows, hs], s.astype(BF16)))
            s = decay[c][:, hs] * s + _dot_tn(v[rows, hs], kt[c][:, hs])
        st[h] = s
        osc[:, hs] = o_intra + jnp.concatenate(o_inter, axis=0)

    parts = []
    for h in range(HG_HEADS):
        blk = osc[:, h * HG_K:(h + 1) * HG_K]
        parts.append(blk * lax.rsqrt(jnp.mean(blk * blk, axis=-1, keepdims=True) + EPS))
    on = (jnp.concatenate(parts, axis=1) * ghg_ref[...] * _silu(proj[:, 3 * D_MODEL:])).astype(BF16)
    yb = _dot(on, wbr_ref[...])
    mb_ref[...] = _sigmoid(_dot(xn, wgb_ref[...])) * yb

    @pl.when(ti == nt - 1)
    def _():
        for h in range(HG_HEADS):
            hgn_ref[h] = st[h].T


def _hgrn_mixer(x, hg0, p, tt):
    b, t, d = x.shape
    grid = (b, t // tt)
    row_spec = pl.BlockSpec((None, tt, d), lambda i, j: (i, j, 0))
    st_spec = pl.BlockSpec((None, HG_HEADS, HG_K, HG_K), lambda i, j: (i, 0, 0, 0))
    consts = [p['g_mix'], p['w_hg'], p['w_gb'], p['lb_logits'], p['g_hgrn'], p['w_hgrn_branch'], _chunk_tri(tt)]
    return pl.pallas_call(
        _hgrn_kernel,
        grid=grid,
        in_specs=[row_spec, st_spec] + [_const_spec(c.shape) for c in consts],
        out_specs=[row_spec, st_spec],
        out_shape=[jax.ShapeDtypeStruct((b, t, d), F32),
                   jax.ShapeDtypeStruct((b, HG_HEADS, HG_K, HG_K), F32)],
        scratch_shapes=[pltpu.VMEM((tt, d), F32), pltpu.VMEM((HG_HEADS, HG_K, HG_K), F32)],
        compiler_params=pltpu.CompilerParams(dimension_semantics=("arbitrary", "arbitrary"),
                                             vmem_limit_bytes=VMEM_LIMIT),
        name="hgrn_mixer",
    )(x, hg0, *consts)


def _memkv_kernel(m_ref, g_ref, wk_ref, wv_ref, k_ref, v_ref):
    mn = _rmsnorm(m_ref[...], g_ref[...]).astype(BF16)
    k_ref[...] = _dot(mn, wk_ref[...])
    v_ref[...] = _dot(mn, wv_ref[...])


def _memory_kv(mem, p):
    b, n, d = mem.shape
    spec = pl.BlockSpec((None, n, d), lambda i: (i, 0, 0))
    consts = [p['g_mem'], p['w_xk'], p['w_xv']]
    return pl.pallas_call(
        _memkv_kernel,
        grid=(b,),
        in_specs=[spec] + [_const_spec(c.shape) for c in consts],
        out_specs=[spec, spec],
        out_shape=[jax.ShapeDtypeStruct((b, n, d), F32)] * 2,
        compiler_params=pltpu.CompilerParams(dimension_semantics=("arbitrary",),
                                             vmem_limit_bytes=VMEM_LIMIT),
        name="memory_kv",
    )(mem, *consts)


def _post_kernel(x_ref, ma_ref, mb_ref, mk_ref, mv_ref, cnt0_ref, wo_ref, gx_ref, wq_ref, wxo_ref,
                 gmoe_ref, wr_ref, br_ref, triu_ref,
                 h_ref, xn_ref, meta_ref, gate_ref, cnt_ref,
                 base):
    tm = x_ref.shape[0]
    first = jnp.logical_and(pl.program_id(0) == 0, pl.program_id(1) == 0)

    @pl.when(first)
    def _():
        base[...] = cnt0_ref[...]

    m = (ma_ref[...] + mb_ref[...]).astype(BF16)
    h1 = x_ref[...] + _dot(m, wo_ref[...])

    hn = _rmsnorm(h1, gx_ref[...]).astype(BF16)
    q = _dot(hn, wq_ref[...])
    heads = []
    for hh in range(X_HEADS):
        hs = slice(hh * X_HEAD_DIM, (hh + 1) * X_HEAD_DIM)
        s = _dot_nt(q[:, hs].astype(BF16), mk_ref[:, hs].astype(BF16)) * (X_HEAD_DIM ** -0.5)
        s = s - jnp.max(s, axis=-1, keepdims=True)
        e = jnp.exp(s)
        pr = e / jnp.sum(e, axis=-1, keepdims=True)
        heads.append(_dot(pr.astype(BF16), mv_ref[:, hs].astype(BF16)))
    o = jnp.concatenate(heads, axis=1).astype(BF16)
    h2 = h1 + _dot(o, wxo_ref[...])
    h_ref[...] = h2

    xn3 = _rmsnorm(h2, gmoe_ref[...])
    xn_ref[...] = _pack_bf16_pairs(xn3)
    logits = _dot(xn3.astype(BF16), wr_ref[...]) + br_ref[...]

    run = logits.T[:N_EXPERTS, :]
    eid = lax.broadcasted_iota(jnp.int32, (N_EXPERTS, tm), 0).astype(F32)
    vals, ids, hots = [], [], []
    for _ in range(TOP_K):
        mx = jnp.max(run, axis=0, keepdims=True)
        idx = jnp.min(jnp.where(run == mx, eid, float(N_EXPERTS)), axis=0, keepdims=True)
        hot = eid == idx
        run = jnp.where(hot, -jnp.inf, run)
        vals.append(mx)
        ids.append(idx)
        hots.append(hot)
    es = [jnp.exp(v - vals[0]) for v in vals]
    den = es[0] + es[1] + es[2] + es[3]

    tot = jnp.zeros((N_EXPERTS, tm), F32)
    for hot in hots:
        tot = tot + hot.astype(F32)
    before = base[:, :1] + _dot(tot.astype(BF16), triu_ref[...])
    base[...] = base[...] + jnp.sum(tot, axis=1, keepdims=True)
    cnt_ref[...] = base[...]

    ranks = [jnp.sum(jnp.where(hot, before, 0.0), axis=0, keepdims=True) for hot in hots]
    meta_ref[...] = jnp.concatenate(ids + ranks, axis=0).astype(jnp.int32)
    gate_ref[...] = jnp.concatenate([e / den for e in es] + [jnp.zeros_like(den)] * TOP_K, axis=0)


def _post_mixer(x, ma, mb, mk, mv, cnt0, p, tm):
    b, t, d = x.shape
    grid = (b, t // tm)
    row_spec = pl.BlockSpec((None, tm, d), lambda i, j: (i, j, 0))
    small_spec = pl.BlockSpec((None, 2 * TOP_K, tm), lambda i, j: (i, 0, j))
    mem_spec = pl.BlockSpec((None, N_MEM, d), lambda i, j: (i, 0, 0))
    cnt_spec = pl.BlockSpec((N_EXPERTS, LANES), lambda i, j: (0, 0))
    triu = jnp.triu(jnp.ones((tm, tm), BF16), 1)
    consts = [p['w_out'], p['g_xattn'], p['w_xq'], p['w_xo'], p['g_moe'], p['w_router'], p['b_router'], triu]
    return pl.pallas_call(
        _post_kernel,
        grid=grid,
        in_specs=[row_spec, row_spec, row_spec, mem_spec, mem_spec, cnt_spec]
                 + [_const_spec(c.shape) for c in consts],
        out_specs=[row_spec, pl.BlockSpec((None, tm, d // 2), lambda i, j: (i, j, 0)), small_spec, small_spec,
                   cnt_spec],
        out_shape=[jax.ShapeDtypeStruct((b, t, d), F32),
                   jax.ShapeDtypeStruct((b, t, d // 2), jnp.int32),
                   jax.ShapeDtypeStruct((b, 2 * TOP_K, t), jnp.int32),
                   jax.ShapeDtypeStruct((b, 2 * TOP_K, t), F32),
                   jax.ShapeDtypeStruct((N_EXPERTS, LANES), F32)],
        scratch_shapes=[pltpu.VMEM((N_EXPERTS, LANES), F32)],
        compiler_params=pltpu.CompilerParams(dimension_semantics=("arbitrary", "arbitrary"),
                                             vmem_limit_bytes=VMEM_LIMIT),
        name="post_mixer",
    )(x, ma, mb, mk, mv, cnt0, *consts)


def _sc_chunk(rows_per_worker):
    chunk = SC_CHUNK
    while rows_per_worker % (SC_BUFFERS * chunk):
        chunk //= 2
    assert chunk % 8 == 0, rows_per_worker
    return chunk


def _sc_mesh():
    return plsc.VectorSubcoreMesh(core_axis_name="c", subcore_axis_name="s")


def _sc_worker():
    return lax.axis_index("s") * SC_CORES + lax.axis_index("c")


def _dispatch(xn, dest, n_slots):
    n, d = xn.shape
    per_w = n // SC_WORKERS
    chunk = _sc_chunk(per_w)
    n_chunks = per_w // chunk
    idx = dest.reshape(TOP_K, n // chunk, chunk).transpose(1, 0, 2)

    def body(x_hbm, idx_hbm, out_hbm, idx_v, rows_v, rsem, wsem):
        wid = _sc_worker()

        def read(i, b):
            blk = wid * n_chunks + i
            return pltpu.make_async_copy(x_hbm.at[pl.ds(blk * chunk, chunk)], rows_v.at[b], rsem.at[b])

        def read_start(i, b):
            pltpu.sync_copy(idx_hbm.at[wid * n_chunks + i], idx_v.at[b])
            read(i, b).start()

        def write(b, j):
            return pltpu.make_async_copy(rows_v.at[b], out_hbm.at[idx_v.at[b, j]], wsem.at[b])

        for b in range(SC_BUFFERS):
            read_start(b, b)

        @pl.loop(0, n_chunks, step=SC_BUFFERS)
        def _(i0):
            for b in range(SC_BUFFERS):
                i = i0 + b
                read(i, b).wait()
                for j in range(TOP_K):
                    write(b, j).start()
                for j in range(TOP_K):
                    write(b, j).wait()

                @pl.when(i + SC_BUFFERS < n_chunks)
                def _():
                    read_start(i + SC_BUFFERS, b)

    return pl.kernel(
        body, mesh=_sc_mesh(),
        out_type=jax.ShapeDtypeStruct((n_slots, d), xn.dtype),
        scratch_types=[pltpu.VMEM((SC_BUFFERS, TOP_K, chunk), jnp.int32),
                       pltpu.VMEM((SC_BUFFERS, chunk, d), xn.dtype),
                       pltpu.SemaphoreType.DMA((SC_BUFFERS,)),
                       pltpu.SemaphoreType.DMA((SC_BUFFERS,))],
        name="moe_dispatch",
    )(xn, idx)


def _gather_rows(table, idx):
    n_out = idx.shape[0]
    d = table.shape[1]
    per_w = n_out // SC_WORKERS
    chunk = _sc_chunk(per_w)
    n_chunks = per_w // chunk

    def body(table_hbm, idx_hbm, out_hbm, idx_v, rows_v, gsem, wsem):
        base = _sc_worker() * per_w

        def gather(b):
            return pltpu.make_async_copy(table_hbm.at[idx_v.at[b]], rows_v.at[b], gsem.at[b])

        def gather_start(i, b):
            pltpu.sync_copy(idx_hbm.at[pl.ds(base + i * chunk, chunk)], idx_v.at[b])
            gather(b).start()

        def write(i, b):
            return pltpu.make_async_copy(rows_v.at[b], out_hbm.at[pl.ds(base + i * chunk, chunk)], wsem.at[b])

        for b in range(SC_BUFFERS):
            gather_start(b, b)

        @pl.loop(0, n_chunks, step=SC_BUFFERS)
        def _(i0):
            for b in range(SC_BUFFERS):
                i = i0 + b
                gather(b).wait()
                write(i, b).start()
                write(i, b).wait()

                @pl.when(i + SC_BUFFERS < n_chunks)
                def _():
                    gather_start(i + SC_BUFFERS, b)

    return pl.kernel(
        body, mesh=_sc_mesh(),
        out_type=jax.ShapeDtypeStruct((n_out, d), table.dtype),
        scratch_types=[pltpu.VMEM((SC_BUFFERS, chunk), jnp.int32),
                       pltpu.VMEM((SC_BUFFERS, chunk, d), table.dtype),
                       pltpu.SemaphoreType.DMA((SC_BUFFERS,)),
                       pltpu.SemaphoreType.DMA((SC_BUFFERS,))],
        name="moe_gather",
    )(table, idx)


def _moe_kernel(be_ref, valid_ref, x_ref, wgu_ref, bgu_ref, wd_ref, bd_ref, y_ref):
    del be_ref
    valid = valid_ref[pl.program_id(0)]

    @pl.when(valid > 0)
    def _():
        row = lax.broadcasted_iota(jnp.int32, x_ref.shape, 0)
        x_hi, x_lo = _unpack_bf16_pairs(jnp.where(row < valid, x_ref[...], 0))
        half = x_ref.shape[1]
        gu = (_dot(x_hi.astype(BF16), wgu_ref[:half, :]) + _dot(x_lo.astype(BF16), wgu_ref[half:, :])
              + bgu_ref[...])
        gate = jnp.minimum(gu[:, :D_EXPERT], SWIGLU_LIMIT)
        up = jnp.clip(gu[:, D_EXPERT:], -SWIGLU_LIMIT, SWIGLU_LIMIT)
        hmid = ((up + 1.0) * gate * _sigmoid(SWIGLU_ALPHA * gate)).astype(BF16)
        y_ref[...] = _pack_bf16_pairs(_dot(hmid, wd_ref[...]) + bd_ref[...])


def _moe_experts(slots, block_e, block_valid, p):
    n_slots, dh = slots.shape
    d = 2 * dh
    n_blocks = block_e.shape[0]
    block_rows = n_slots // n_blocks
    grid_spec = pltpu.PrefetchScalarGridSpec(
        num_scalar_prefetch=2,
        grid=(n_blocks,),
        in_specs=[pl.BlockSpec((block_rows, dh), lambda i, be, bv: (i, 0)),
                  pl.BlockSpec((None, d, 2 * D_EXPERT), lambda i, be, bv: (be[i], 0, 0)),
                  pl.BlockSpec((None, 1, 2 * D_EXPERT), lambda i, be, bv: (be[i], 0, 0)),
                  pl.BlockSpec((None, D_EXPERT, d), lambda i, be, bv: (be[i], 0, 0)),
                  pl.BlockSpec((None, 1, d), lambda i, be, bv: (be[i], 0, 0))],
        out_specs=pl.BlockSpec((block_rows, dh), lambda i, be, bv: (i, 0)),
    )
    return pl.pallas_call(
        _moe_kernel,
        grid_spec=grid_spec,
        out_shape=jax.ShapeDtypeStruct((n_slots, dh), jnp.int32),
        compiler_params=pltpu.CompilerParams(dimension_semantics=("arbitrary",),
                                             vmem_limit_bytes=VMEM_LIMIT),
        name="moe_experts",
    )(block_e, block_valid, slots, p['w_gate_up'], p['b_gate_up'], p['w_down'], p['b_down'])


def _combine_kernel(h_ref, gate_ref, y_ref, gfin_ref, out_ref):
    half = y_ref.shape[2]
    acc_hi = h_ref[:, :half]
    acc_lo = h_ref[:, half:]
    gates = gate_ref[...].T
    for j in range(TOP_K):
        y_hi, y_lo = _unpack_bf16_pairs(y_ref[j])
        acc_hi = acc_hi + gates[:, j:j + 1] * y_hi
        acc_lo = acc_lo + gates[:, j:j + 1] * y_lo
    out_ref[...] = _rmsnorm(jnp.concatenate([acc_hi, acc_lo], axis=1), gfin_ref[...])


def _combine(h, gates, y_tok, g_final, tm):
    n, d = h.shape
    return pl.pallas_call(
        _combine_kernel,
        grid=(n // tm,),
        in_specs=[pl.BlockSpec((tm, d), lambda i: (i, 0)),
                  pl.BlockSpec((2 * TOP_K, tm), lambda i: (0, i)),
                  pl.BlockSpec((TOP_K, tm, d // 2), lambda i: (0, i, 0)),
                  _const_spec(g_final.shape)],
        out_specs=pl.BlockSpec((tm, d), lambda i: (i, 0)),
        out_shape=jax.ShapeDtypeStruct((n, d), F32),
        compiler_params=pltpu.CompilerParams(dimension_semantics=("arbitrary",),
                                             vmem_limit_bytes=VMEM_LIMIT),
        name="moe_combine",
    )(h, gates, y_tok, g_final)


def _moe_outputs(xn, meta, counts, p):
    n, d = xn.shape
    n_rows = n * TOP_K
    block_rows = max(MOE_ROWS_MIN, min(MOE_ROWS_MAX, n_rows // N_EXPERTS))
    n_blocks = n_rows // block_rows + N_EXPERTS
    n_slots = n_blocks * block_rows
    cnt = counts[:, 0].astype(jnp.int32)
    padded = (cnt + block_rows - 1) // block_rows * block_rows
    pad_end = jnp.cumsum(padded)
    pad_start = pad_end - padded
    block_row0 = jnp.arange(n_blocks, dtype=jnp.int32) * block_rows
    block_e = jnp.minimum(jnp.sum((pad_end[None, :] <= block_row0[:, None]).astype(jnp.int32), axis=1),
                          N_EXPERTS - 1)
    block_valid = jnp.clip((pad_start + cnt)[block_e] - block_row0, 0, block_rows)
    dest = (pad_start[meta[:TOP_K]] + meta[TOP_K:]).astype(jnp.int32)
    slots = _dispatch(xn, dest, n_slots)
    y_slots = _moe_experts(slots, block_e, block_valid, p)
    return _gather_rows(y_slots, dest.reshape(-1)).reshape(TOP_K, n, d)


def _prepare(g_mix, w_in, conv_w, conv_b, dt_bias, a_log, d_skip, g_ssm, w_ssm_branch, lb_logits, g_hgrn,
             w_hgrn_branch, w_out, g_mem, w_xk, w_xv, g_xattn, w_xq, w_xo, g_moe, w_router, b_router,
             w_gate_up, b_gate_up, w_down, b_down):
    d = D_MODEL
    w = w_in[0]
    o_z, o_xbc = 0, SSM_INNER
    o_dt = o_xbc + CONV_CH
    o_q = o_dt + SSM_HEADS
    o_ga = o_q + 4 * d
    o_gb = o_ga + d

    def row(v):
        return v.reshape(1, -1).astype(F32)

    def lane_pad(v, fill=0.0):
        return jnp.pad(v.reshape(1, -1).astype(F32), ((0, 0), (0, LANES - v.shape[-1])), constant_values=fill)

    head_of_col = np.arange(SSM_INNER) // SSM_HEAD_DIM
    expand = (np.arange(LANES)[:, None] == head_of_col[None, :])
    return dict(
        g_mix=row(g_mix[0]),
        w_z=w[:, o_z:o_z + SSM_INNER].astype(BF16),
        w_xbc=w[:, o_xbc:o_xbc + CONV_CH].astype(BF16),
        w_dt=jnp.pad(w[:, o_dt:o_dt + SSM_HEADS], ((0, 0), (0, LANES - SSM_HEADS))).astype(BF16),
        w_hg=w[:, o_q:o_q + 4 * d].astype(BF16),
        w_ga=w[:, o_ga:o_ga + d].astype(BF16),
        w_gb=w[:, o_gb:o_gb + d].astype(BF16),
        conv_w=conv_w[0].astype(F32),
        conv_b=row(conv_b[0]),
        dt_bias=lane_pad(dt_bias[0]),
        a_log=lane_pad(a_log[0]),
        d_skip=row(jnp.repeat(d_skip[0], SSM_HEAD_DIM)),
        g_ssm=row(g_ssm[0]),
        w_ssm_branch=w_ssm_branch[0].astype(BF16),
        lb_logits=lb_logits.astype(F32),
        g_hgrn=row(g_hgrn[0]),
        w_hgrn_branch=w_hgrn_branch[0].astype(BF16),
        w_out=w_out[0].astype(BF16),
        g_mem=row(g_mem[0]),
        w_xk=w_xk[0].astype(BF16),
        w_xv=w_xv[0].astype(BF16),
        g_xattn=row(g_xattn[0]),
        w_xq=w_xq[0].astype(BF16),
        w_xo=w_xo[0].astype(BF16),
        g_moe=row(g_moe[0]),
        w_router=jnp.pad(w_router[0], ((0, 0), (0, LANES - N_EXPERTS))).astype(BF16),
        b_router=lane_pad(b_router[0], fill=-jnp.inf),
        w_gate_up=w_gate_up[0].astype(BF16),
        b_gate_up=b_gate_up[0].reshape(N_EXPERTS, 1, 2 * D_EXPERT).astype(F32),
        w_down=w_down[0].astype(BF16),
        b_down=b_down[0].reshape(N_EXPERTS, 1, d).astype(F32),
        expand=jnp.asarray(expand, BF16),
    )


def _group_to_experts(x, conv0, ssm0, hg0, mk, mv, p):
    b, t, d = x.shape
    ma, conv_n, ssm_n = _ssd_mixer(x, conv0, ssm0, p, min(SSD_TILE, t))
    mb, hg_n = _hgrn_mixer(x, hg0, p, min(HGRN_TILE, t))
    tm = min(POST_TILE, t)
    cnt0 = jnp.zeros((N_EXPERTS, LANES), F32)
    h2, xn3, meta, gates, counts = _post_mixer(x, ma, mb, mk, mv, cnt0, p, tm)
    n = b * t
    meta = meta.transpose(1, 0, 2).reshape(2 * TOP_K, n)
    gates = gates.transpose(1, 0, 2).reshape(2 * TOP_K, n)
    y_tok = _moe_outputs(xn3.reshape(n, d // 2), meta, counts, p)
    return (h2.reshape(n, d), gates, y_tok), (conv_n, ssm_n, hg_n)


def _group_combine(h2, gates, y_tok, g_final, shape):
    return _combine(h2, gates, y_tok, g_final, min(COMBINE_TILE, h2.shape[0])).reshape(shape)


def kernel(x_prompt, x_sample, mem_prompt, state_conv, state_ssm, state_hgrn, cache_mem_k, cache_mem_v, g_mix, w_in, conv_w, conv_b, dt_bias, a_log, d_skip, g_ssm, w_ssm_branch, lb_logits, g_hgrn, w_hgrn_branch, w_out, g_mem, w_xk, w_xv, g_xattn, w_xq, w_xo, g_moe, w_router, b_router, w_gate_up, b_gate_up, w_down, b_down, g_final):
    p = _prepare(g_mix, w_in, conv_w, conv_b, dt_bias, a_log, d_skip, g_ssm, w_ssm_branch, lb_logits, g_hgrn,
                 w_hgrn_branch, w_out, g_mem, w_xk, w_xv, g_xattn, w_xq, w_xo, g_moe, w_router, b_router,
                 w_gate_up, b_gate_up, w_down, b_down)
    g_fin = g_final.reshape(1, -1).astype(F32)
    bp = x_prompt.shape[0]
    bs = x_sample.shape[0]
    d = D_MODEL

    mk_p, mv_p = _memory_kv(mem_prompt, p)
    moe_p, (conv_p, ssm_p, hg_p) = _group_to_experts(
        x_prompt,
        jnp.zeros((bp, CONV_W - 1, CONV_CH), F32),
        jnp.zeros((bp, SSM_HEADS, SSM_HEAD_DIM, SSM_STATE), F32),
        jnp.zeros((bp, HG_HEADS, HG_K, HG_K), F32),
        mk_p, mv_p, p)
    moe_s, (conv_s, ssm_s, hg_s) = _group_to_experts(
        x_sample, state_conv[0], state_ssm[0], state_hgrn[0],
        cache_mem_k[0].reshape(bs, N_MEM, d), cache_mem_v[0].reshape(bs, N_MEM, d), p)
    y_p = _group_combine(*moe_p, g_fin, x_prompt.shape)
    y_s = _group_combine(*moe_s, g_fin, x_sample.shape)

    kv_shape = (1, bp, N_MEM, X_HEADS, X_HEAD_DIM)
    return (y_p, y_s,
            conv_p[None], ssm_p[None], hg_p[None], mk_p.reshape(kv_shape), mv_p.reshape(kv_shape),
            conv_s[None], ssm_s[None], hg_s[None])
```

```python
import jax
import jax.numpy as jnp
import numpy as np
from jax import lax
from jax.experimental import pallas as pl
from jax.experimental.pallas import tpu as pltpu
from jax.experimental.pallas import tpu_sc as plsc

F32 = jnp.float32
BF16 = jnp.bfloat16

D_MODEL = 1024
CHUNK = 64
EPS = 1e-6
SSM_INNER = 2 * D_MODEL
SSM_HEAD_DIM = 64
SSM_HEADS = SSM_INNER // SSM_HEAD_DIM
SSM_GROUPS = 4
SSM_STATE = 128
GROUP_COLS = SSM_INNER // SSM_GROUPS
CONV_W = 4
BC_COLS = SSM_GROUPS * SSM_STATE
CONV_CH = SSM_INNER + 2 * BC_COLS
HG_HEADS = 8
HG_K = D_MODEL // HG_HEADS
HG_SUB = CHUNK // 2
N_MEM = 256
X_HEADS = 4
X_HEAD_DIM = D_MODEL // X_HEADS
N_EXPERTS = 32
TOP_K = 4
D_EXPERT = D_MODEL
SWIGLU_LIMIT = 7.0
SWIGLU_ALPHA = 1.702

LANES = 128
CONV_PAD = 8
MOE_ROWS_MAX = 512
MOE_ROWS_MIN = 128
SSD_TILE = 256
HGRN_TILE = 256
POST_TILE = 512
COMBINE_TILE = 512
VMEM_LIMIT = 56 * 1024 * 1024
SC_CORES = 2
SC_SUBCORES = 16
SC_WORKERS = SC_CORES * SC_SUBCORES
SC_CHUNK = 64
SC_BUFFERS = 2


def _const_spec(shape):
    nd = len(shape)
    return pl.BlockSpec(shape, lambda *_: (0,) * nd, pipeline_mode=pl.Buffered(1))


def _dot(a, b):
    return jnp.dot(a, b, preferred_element_type=F32)


def _dot_nt(a, b):
    return lax.dot_general(a, b, (((1,), (1,)), ((), ())), preferred_element_type=F32)


def _dot_tn(a, b):
    return lax.dot_general(a, b, (((0,), (0,)), ((), ())), preferred_element_type=F32)


def _split3(a):
    hi = a.astype(BF16)
    r1 = a - hi.astype(F32)
    mid = r1.astype(BF16)
    lo = (r1 - mid.astype(F32)).astype(BF16)
    return hi, mid, lo


def _exact_dot_lhs01(sel, a):
    hi, mid, lo = _split3(a)
    return _dot(sel, hi) + _dot(sel, mid) + _dot(sel, lo)


def _rmsnorm(x, g):
    return x * lax.rsqrt(jnp.mean(x * x, axis=-1, keepdims=True) + EPS) * g


def _sigmoid(x):
    return jax.nn.sigmoid(x)


def _silu(x):
    return x * jax.nn.sigmoid(x)


def _softplus(x):
    return jnp.maximum(x, 0.0) + jnp.log1p(jnp.exp(-jnp.abs(x)))


def _pack_bf16_pairs(x):
    c = x.shape[1] // 2
    hi = lax.bitcast_convert_type(x[:, :c].astype(BF16).astype(F32), jnp.int32)
    lo = lax.bitcast_convert_type(x[:, c:].astype(BF16).astype(F32), jnp.int32)
    return hi | lax.shift_right_logical(lo, 16)


def _unpack_bf16_pairs(u):
    hi = lax.bitcast_convert_type(u & jnp.int32(-65536), F32)
    lo = lax.bitcast_convert_type(lax.shift_left(u, 16), F32)
    return hi, lo


def _ssd_kernel(x_ref, conv0_ref, ssm0_ref, g_ref, wxbc_ref, wz_ref, wdt_ref, wga_ref, convw_ref,
                convb_ref, dtb_ref, alog_ref, dskip_ref, gssm_ref, wbr_ref, tri_ref, expand_ref,
                ma_ref, convn_ref, ssmn_ref,
                xpad, xact, xdt, ysc, st):
    tt = x_ref.shape[0]
    ti = pl.program_id(1)
    nt = pl.num_programs(1)

    @pl.when(ti == 0)
    def _():
        xpad[CONV_PAD - (CONV_W - 1):CONV_PAD, :] = conv0_ref[...]
        st[...] = ssm0_ref[...].reshape(SSM_INNER, SSM_STATE).T

    xn = _rmsnorm(x_ref[...], g_ref[...]).astype(BF16)
    xpad[CONV_PAD:CONV_PAD + tt, :] = _dot(xn, wxbc_ref[...])

    conv = convb_ref[...] + xpad[CONV_PAD:CONV_PAD + tt, :] * convw_ref[CONV_W - 1:CONV_W, :]
    for k in range(1, CONV_W):
        conv = conv + xpad[CONV_PAD - k:CONV_PAD - k + tt, :] * convw_ref[CONV_W - 1 - k:CONV_W - k, :]
    xact[...] = _silu(conv)
    tail = xpad[CONV_PAD + tt - (CONV_W - 1):CONV_PAD + tt, :]
    xpad[CONV_PAD - (CONV_W - 1):CONV_PAD, :] = tail

    dt = _softplus(_dot(xn, wdt_ref[...]) + dtb_ref[...])
    acum_all = _exact_dot_lhs01(tri_ref[...], dt * -jnp.exp(alog_ref[...]))
    dt_hi = dt.astype(BF16)
    dt_lo = (dt - dt_hi.astype(F32)).astype(BF16)
    dt_x = _dot(dt_hi, expand_ref[...]) + _dot(dt_lo, expand_ref[...])
    xdt[...] = xact[:, :SSM_INNER] * dt_x

    lane = lax.broadcasted_iota(jnp.int32, (CHUNK, LANES), 1)
    row = lax.broadcasted_iota(jnp.int32, (CHUNK, LANES), 0)
    causal2 = row >= (lane % CHUNK)
    diag2 = row == (lane % CHUNK)
    left = lane < CHUNK
    pair_of_lane = lane // CHUNK

    for c in range(tt // CHUNK):
        rows = slice(c * CHUNK, (c + 1) * CHUNK)
        acum = acum_all[rows, :]
        for g in range(SSM_GROUPS):
            gs = slice(g * GROUP_COLS, (g + 1) * GROUP_COLS)
            b_g = xact[rows, SSM_INNER + g * SSM_STATE:SSM_INNER + (g + 1) * SSM_STATE].astype(BF16)
            c_g = xact[rows, SSM_INNER + BC_COLS + g * SSM_STATE:
                       SSM_INNER + BC_COLS + (g + 1) * SSM_STATE].astype(BF16)
            cb2 = _dot_nt(c_g, jnp.concatenate([b_g, b_g], axis=0))
            st_g = st[:, gs]
            y_inter = _dot(c_g, st_g.astype(BF16))
            xw, decay = [], []
            for j in range(GROUP_COLS // LANES):
                pair = g * (GROUP_COLS // LANES) + j
                ps = slice(pair * LANES, (pair + 1) * LANES)
                a_col = jnp.take_along_axis(acum, pair_of_lane + 2 * pair, axis=1)
                a_row = jnp.sum(jnp.where(diag2, a_col, 0.0), axis=0, keepdims=True)
                a_last = a_col[CHUNK - 1:CHUNK, :]
                dec = jnp.where(causal2, jnp.exp(jnp.minimum(a_col - a_row, 0.0)), 0.0)
                wts = (cb2 * dec).astype(BF16)
                xp = xdt[rows, ps]
                xbd = jnp.concatenate([jnp.where(left, xp, 0.0), jnp.where(left, 0.0, xp)],
                                      axis=0).astype(BF16)
                ysc[rows, ps] = _dot(wts, xbd) + y_inter[:, j * LANES:(j + 1) * LANES] * jnp.exp(a_col)
                xw.append((jnp.exp(a_last - a_col) * xp).astype(BF16))
                decay.append(jnp.exp(a_last))
            st[:, gs] = jnp.concatenate(decay, axis=1) * st_g + _dot_tn(b_g, jnp.concatenate(xw, axis=1))

    xs = xact[:, :SSM_INNER]
    y = ysc[...] + dskip_ref[...] * xs
    yz = y * _silu(_dot(xn, wz_ref[...]))
    parts = []
    for g in range(SSM_GROUPS):
        blk = yz[:, g * GROUP_COLS:(g + 1) * GROUP_COLS]
        parts.append(blk * lax.rsqrt(jnp.mean(blk * blk, axis=-1, keepdims=True) + EPS))
    yn = (jnp.concatenate(parts, axis=1) * gssm_ref[...]).astype(BF16)
    ya = _dot(yn, wbr_ref[...])
    ma_ref[...] = _sigmoid(_dot(xn, wga_ref[...])) * ya

    @pl.when(ti == nt - 1)
    def _():
        convn_ref[...] = tail
        ssmn_ref[...] = st[...].T.reshape(SSM_HEADS, SSM_HEAD_DIM, SSM_STATE)


def _chunk_tri(tt):
    return jnp.asarray(np.kron(np.eye(tt // CHUNK), np.tril(np.ones((CHUNK, CHUNK)))), BF16)


def _ssd_mixer(x, conv0, ssm0, p, tt):
    b, t, d = x.shape
    grid = (b, t // tt)
    row_spec = pl.BlockSpec((None, tt, d), lambda i, j: (i, j, 0))
    consts = [p['g_mix'], p['w_xbc'], p['w_z'], p['w_dt'], p['w_ga'], p['conv_w'], p['conv_b'], p['dt_bias'],
              p['a_log'], p['d_skip'], p['g_ssm'], p['w_ssm_branch'], _chunk_tri(tt), p['expand']]
    return pl.pallas_call(
        _ssd_kernel,
        grid=grid,
        in_specs=[row_spec,
                  pl.BlockSpec((None, CONV_W - 1, CONV_CH), lambda i, j: (i, 0, 0)),
                  pl.BlockSpec((None, SSM_HEADS, SSM_HEAD_DIM, SSM_STATE), lambda i, j: (i, 0, 0, 0))]
                 + [_const_spec(c.shape) for c in consts],
        out_specs=[row_spec,
                   pl.BlockSpec((None, CONV_W - 1, CONV_CH), lambda i, j: (i, 0, 0)),
                   pl.BlockSpec((None, SSM_HEADS, SSM_HEAD_DIM, SSM_STATE), lambda i, j: (i, 0, 0, 0))],
        out_shape=[jax.ShapeDtypeStruct((b, t, d), F32),
                   jax.ShapeDtypeStruct((b, CONV_W - 1, CONV_CH), F32),
                   jax.ShapeDtypeStruct((b, SSM_HEADS, SSM_HEAD_DIM, SSM_STATE), F32)],
        scratch_shapes=[pltpu.VMEM((CONV_PAD + tt, CONV_CH), F32),
                        pltpu.VMEM((tt, CONV_CH), F32),
                        pltpu.VMEM((tt, SSM_INNER), F32),
                        pltpu.VMEM((tt, SSM_INNER), F32),
                        pltpu.VMEM((SSM_STATE, SSM_INNER), F32)],
        compiler_params=pltpu.CompilerParams(dimension_semantics=("arbitrary", "arbitrary"),
                                             vmem_limit_bytes=VMEM_LIMIT),
        name="ssd_mixer",
    )(x, conv0, ssm0, *consts)


def _hgrn_kernel(x_ref, hg0_ref, g_ref, wh_ref, wgb_ref, lbl_ref, ghg_ref, wbr_ref, tri_ref,
                 mb_ref, hgn_ref,
                 osc, st):
    tt = x_ref.shape[0]
    n_chunks = tt // CHUNK
    ti = pl.program_id(1)
    nt = pl.num_programs(1)

    @pl.when(ti == 0)
    def _():
        for h in range(HG_HEADS):
            st[h] = hg0_ref[h].T

    xn = _rmsnorm(x_ref[...], g_ref[...]).astype(BF16)
    proj = _dot(xn, wh_ref[...])
    l0 = lbl_ref[0:1, :]
    l1 = lbl_ref[1:2, :]
    lmax = jnp.maximum(l0, l1)
    e0 = jnp.exp(l0 - lmax)
    lb = e0 / (e0 + jnp.exp(l1 - lmax))
    fr = proj[:, D_MODEL:2 * D_MODEL]
    kk = (1.0 - lb) * _sigmoid(-fr)
    v = _silu(proj[:, 2 * D_MODEL:3 * D_MODEL]).astype(BF16)
    gc = _exact_dot_lhs01(tri_ref[...], jnp.log(lb + (1.0 - lb) * _sigmoid(fr)))
    q = proj[:, :D_MODEL]
    qh = (q * jnp.exp(gc)).astype(BF16)
    mid = []
    for m in range(tt // HG_SUB):
        r = m * HG_SUB + HG_SUB // 2 - 1
        mid.append(jnp.broadcast_to(gc[r:r + 1, :], (HG_SUB, D_MODEL)))
    d_mid = gc - jnp.concatenate(mid, axis=0)
    q_sub = (q * jnp.exp(d_mid)).astype(BF16)
    k_sub = (kk * jnp.exp(-d_mid)).astype(BF16)
    q_far, k_far, v_first = [], [], []
    for c in range(n_chunks):
        first = slice(c * CHUNK, c * CHUNK + HG_SUB)
        second = slice(c * CHUNK + HG_SUB, (c + 1) * CHUNK)
        g_half = gc[c * CHUNK + HG_SUB - 1:c * CHUNK + HG_SUB, :]
        q_far.append(q[second, :] * jnp.exp(gc[second, :] - g_half))
        k_far.append(kk[first, :] * jnp.exp(g_half - gc[first, :]))
        v_first.append(v[first, :])
    q_far = jnp.concatenate(q_far, axis=0).astype(BF16)
    k_far = jnp.concatenate(k_far, axis=0).astype(BF16)
    v_first = jnp.concatenate(v_first, axis=0)
    kt, decay = [], []
    for c in range(n_chunks):
        rows = slice(c * CHUNK, (c + 1) * CHUNK)
        glast = gc[(c + 1) * CHUNK - 1:(c + 1) * CHUNK, :]
        kt.append((kk[rows, :] * jnp.exp(glast - gc[rows, :])).astype(BF16))
        decay.append(jnp.exp(glast))

    row = lax.broadcasted_iota(jnp.int32, (tt, tt), 0)
    col = lax.broadcasted_iota(jnp.int32, (tt, tt), 1)
    near = jnp.logical_and(row >= col, row // HG_SUB == col // HG_SUB)
    same_chunk = (row // HG_SUB == col // HG_SUB)[:tt // 2, :tt // 2]
    no_far = jnp.zeros((HG_SUB, HG_K), F32)

    for h in range(HG_HEADS):
        hs = slice(h * HG_K, (h + 1) * HG_K)
        att = jnp.where(near, _dot_nt(q_sub[:, hs], k_sub[:, hs]), 0.0).astype(BF16)
        att_far = jnp.where(same_chunk, _dot_nt(q_far[:, hs], k_far[:, hs]), 0.0).astype(BF16)
        o_far = _dot(att_far, v_first[:, hs])
        far_rows = []
        for c in range(n_chunks):
            far_rows += [no_far, o_far[c * HG_SUB:(c + 1) * HG_SUB, :]]
        o_intra = _dot(att, v[:, hs]) + jnp.concatenate(far_rows, axis=0)
        s = st[h]
        o_inter = []
        for c in range(n_chunks):
            rows = slice(c * CHUNK, (c + 1) * CHUNK)
            o_inter.append(_dot_nt(qh[rows, hs], s.astype(BF16)))
            s = decay[c][:, hs] * s + _dot_tn(v[rows, hs], kt[c][:, hs])
        st[h] = s
        osc[:, hs] = o_intra + jnp.concatenate(o_inter, axis=0)

    parts = []
    for h in range(HG_HEADS):
        blk = osc[:, h * HG_K:(h + 1) * HG_K]
        parts.append(blk * lax.rsqrt(jnp.mean(blk * blk, axis=-1, keepdims=True) + EPS))
    on = (jnp.concatenate(parts, axis=1) * ghg_ref[...] * _silu(proj[:, 3 * D_MODEL:])).astype(BF16)
    yb = _dot(on, wbr_ref[...])
    mb_ref[...] = _sigmoid(_dot(xn, wgb_ref[...])) * yb

    @pl.when(ti == nt - 1)
    def _():
        for h in range(HG_HEADS):
            hgn_ref[h] = st[h].T


def _hgrn_mixer(x, hg0, p, tt):
    b, t, d = x.shape
    grid = (b, t // tt)
    row_spec = pl.BlockSpec((None, tt, d), lambda i, j: (i, j, 0))
    st_spec = pl.BlockSpec((None, HG_HEADS, HG_K, HG_K), lambda i, j: (i, 0, 0, 0))
    consts = [p['g_mix'], p['w_hg'], p['w_gb'], p['lb_logits'], p['g_hgrn'], p['w_hgrn_branch'], _chunk_tri(tt)]
    return pl.pallas_call(
        _hgrn_kernel,
        grid=grid,
        in_specs=[row_spec, st_spec] + [_const_spec(c.shape) for c in consts],
        out_specs=[row_spec, st_spec],
        out_shape=[jax.ShapeDtypeStruct((b, t, d), F32),
                   jax.ShapeDtypeStruct((b, HG_HEADS, HG_K, HG_K), F32)],
        scratch_shapes=[pltpu.VMEM((tt, d), F32), pltpu.VMEM((HG_HEADS, HG_K, HG_K), F32)],
        compiler_params=pltpu.CompilerParams(dimension_semantics=("arbitrary", "arbitrary"),
                                             vmem_limit_bytes=VMEM_LIMIT),
        name="hgrn_mixer",
    )(x, hg0, *consts)


def _memkv_kernel(m_ref, g_ref, wk_ref, wv_ref, k_ref, v_ref):
    mn = _rmsnorm(m_ref[...], g_ref[...]).astype(BF16)
    k_ref[...] = _dot(mn, wk_ref[...])
    v_ref[...] = _dot(mn, wv_ref[...])


def _memory_kv(mem, p):
    b, n, d = mem.shape
    spec = pl.BlockSpec((None, n, d), lambda i: (i, 0, 0))
    consts = [p['g_mem'], p['w_xk'], p['w_xv']]
    return pl.pallas_call(
        _memkv_kernel,
        grid=(b,),
        in_specs=[spec] + [_const_spec(c.shape) for c in consts],
        out_specs=[spec, spec],
        out_shape=[jax.ShapeDtypeStruct((b, n, d), F32)] * 2,
        compiler_params=pltpu.CompilerParams(dimension_semantics=("arbitrary",),
                                             vmem_limit_bytes=VMEM_LIMIT),
        name="memory_kv",
    )(mem, *consts)


def _post_kernel(x_ref, ma_ref, mb_ref, mk_ref, mv_ref, cnt0_ref, wo_ref, gx_ref, wq_ref, wxo_ref,
                 gmoe_ref, wr_ref, br_ref, triu_ref,
                 h_ref, xn_ref, meta_ref, gate_ref, cnt_ref,
                 base):
    tm = x_ref.shape[0]
    first = jnp.logical_and(pl.program_id(0) == 0, pl.program_id(1) == 0)

    @pl.when(first)
    def _():
        base[...] = cnt0_ref[...]

    m = (ma_ref[...] + mb_ref[...]).astype(BF16)
    h1 = x_ref[...] + _dot(m, wo_ref[...])

    hn = _rmsnorm(h1, gx_ref[...]).astype(BF16)
    q = _dot(hn, wq_ref[...])
    heads = []
    for hh in range(X_HEADS):
        hs = slice(hh * X_HEAD_DIM, (hh + 1) * X_HEAD_DIM)
        s = _dot_nt(q[:, hs].astype(BF16), mk_ref[:, hs].astype(BF16)) * (X_HEAD_DIM ** -0.5)
        s = s - jnp.max(s, axis=-1, keepdims=True)
        e = jnp.exp(s)
        pr = e / jnp.sum(e, axis=-1, keepdims=True)
        heads.append(_dot(pr.astype(BF16), mv_ref[:, hs].astype(BF16)))
    o = jnp.concatenate(heads, axis=1).astype(BF16)
    h2 = h1 + _dot(o, wxo_ref[...])
    h_ref[...] = h2

    xn3 = _rmsnorm(h2, gmoe_ref[...])
    xn_ref[...] = _pack_bf16_pairs(xn3)
    logits = _dot(xn3.astype(BF16), wr_ref[...]) + br_ref[...]

    run = logits.T[:N_EXPERTS, :]
    eid = lax.broadcasted_iota(jnp.int32, (N_EXPERTS, tm), 0).astype(F32)
    vals, ids, hots = [], [], []
    for _ in range(TOP_K):
        mx = jnp.max(run, axis=0, keepdims=True)
        idx = jnp.min(jnp.where(run == mx, eid, float(N_EXPERTS)), axis=0, keepdims=True)
        hot = eid == idx
        run = jnp.where(hot, -jnp.inf, run)
        vals.append(mx)
        ids.append(idx)
        hots.append(hot)
    es = [jnp.exp(v - vals[0]) for v in vals]
    den = es[0] + es[1] + es[2] + es[3]

    tot = jnp.zeros((N_EXPERTS, tm), F32)
    for hot in hots:
        tot = tot + hot.astype(F32)
    before = base[:, :1] + _dot(tot.astype(BF16), triu_ref[...])
    base[...] = base[...] + jnp.sum(tot, axis=1, keepdims=True)
    cnt_ref[...] = base[...]

    ranks = [jnp.sum(jnp.where(hot, before, 0.0), axis=0, keepdims=True) for hot in hots]
    meta_ref[...] = jnp.concatenate(ids + ranks, axis=0).astype(jnp.int32)
    gate_ref[...] = jnp.concatenate([e / den for e in es] + [jnp.zeros_like(den)] * TOP_K, axis=0)


def _post_mixer(x, ma, mb, mk, mv, cnt0, p, tm):
    b, t, d = x.shape
    grid = (b, t // tm)
    row_spec = pl.BlockSpec((None, tm, d), lambda i, j: (i, j, 0))
    small_spec = pl.BlockSpec((None, 2 * TOP_K, tm), lambda i, j: (i, 0, j))
    mem_spec = pl.BlockSpec((None, N_MEM, d), lambda i, j: (i, 0, 0))
    cnt_spec = pl.BlockSpec((N_EXPERTS, LANES), lambda i, j: (0, 0))
    triu = jnp.triu(jnp.ones((tm, tm), BF16), 1)
    consts = [p['w_out'], p['g_xattn'], p['w_xq'], p['w_xo'], p['g_moe'], p['w_router'], p['b_router'], triu]
    return pl.pallas_call(
        _post_kernel,
        grid=grid,
        in_specs=[row_spec, row_spec, row_spec, mem_spec, mem_spec, cnt_spec]
                 + [_const_spec(c.shape) for c in consts],
        out_specs=[row_spec, pl.BlockSpec((None, tm, d // 2), lambda i, j: (i, j, 0)), small_spec, small_spec,
                   cnt_spec],
        out_shape=[jax.ShapeDtypeStruct((b, t, d), F32),
                   jax.ShapeDtypeStruct((b, t, d // 2), jnp.int32),
                   jax.ShapeDtypeStruct((b, 2 * TOP_K, t), jnp.int32),
                   jax.ShapeDtypeStruct((b, 2 * TOP_K, t), F32),
                   jax.ShapeDtypeStruct((N_EXPERTS, LANES), F32)],
        scratch_shapes=[pltpu.VMEM((N_EXPERTS, LANES), F32)],
        compiler_params=pltpu.CompilerParams(dimension_semantics=("arbitrary", "arbitrary"),
                                             vmem_limit_bytes=VMEM_LIMIT),
        name="post_mixer",
    )(x, ma, mb, mk, mv, cnt0, *consts)


def _sc_chunk(rows_per_worker):
    chunk = SC_CHUNK
    while rows_per_worker % (SC_BUFFERS * chunk):
        chunk //= 2
    assert chunk % 8 == 0, rows_per_worker
    return chunk


def _sc_mesh():
    return plsc.VectorSubcoreMesh(core_axis_name="c", subcore_axis_name="s")


def _sc_worker():
    return lax.axis_index("s") * SC_CORES + lax.axis_index("c")


def _dispatch(xn, dest, n_slots):
    n, d = xn.shape
    per_w = n // SC_WORKERS
    chunk = _sc_chunk(per_w)
    n_chunks = per_w // chunk
    idx = dest.reshape(TOP_K, n // chunk, chunk).transpose(1, 0, 2)

    def body(x_hbm, idx_hbm, out_hbm, idx_v, rows_v, rsem, wsem):
        wid = _sc_worker()

        def read(i, b):
            blk = wid * n_chunks + i
            return pltpu.make_async_copy(x_hbm.at[pl.ds(blk * chunk, chunk)], rows_v.at[b], rsem.at[b])

        def read_start(i, b):
            pltpu.sync_copy(idx_hbm.at[wid * n_chunks + i], idx_v.at[b])
            read(i, b).start()

        def write(b, j):
            return pltpu.make_async_copy(rows_v.at[b], out_hbm.at[idx_v.at[b, j]], wsem.at[b])

        for b in range(SC_BUFFERS):
            read_start(b, b)

        @pl.loop(0, n_chunks, step=SC_BUFFERS)
        def _(i0):
            for b in range(SC_BUFFERS):
                i = i0 + b
                read(i, b).wait()
                for j in range(TOP_K):
                    write(b, j).start()
                for j in range(TOP_K):
                    write(b, j).wait()

                @pl.when(i + SC_BUFFERS < n_chunks)
                def _():
                    read_start(i + SC_BUFFERS, b)

    return pl.kernel(
        body, mesh=_sc_mesh(),
        out_type=jax.ShapeDtypeStruct((n_slots, d), xn.dtype),
        scratch_types=[pltpu.VMEM((SC_BUFFERS, TOP_K, chunk), jnp.int32),
                       pltpu.VMEM((SC_BUFFERS, chunk, d), xn.dtype),
                       pltpu.SemaphoreType.DMA((SC_BUFFERS,)),
                       pltpu.SemaphoreType.DMA((SC_BUFFERS,))],
        name="moe_dispatch",
    )(xn, idx)


def _gather_rows(table, idx):
    n_out = idx.shape[0]
    d = table.shape[1]
    per_w = n_out // SC_WORKERS
    chunk = _sc_chunk(per_w)
    n_chunks = per_w // chunk

    def body(table_hbm, idx_hbm, out_hbm, idx_v, rows_v, gsem, wsem):
        base = _sc_worker() * per_w

        def gather(b):
            return pltpu.make_async_copy(table_hbm.at[idx_v.at[b]], rows_v.at[b], gsem.at[b])

        def gather_start(i, b):
            pltpu.sync_copy(idx_hbm.at[pl.ds(base + i * chunk, chunk)], idx_v.at[b])
            gather(b).start()

        def write(i, b):
            return pltpu.make_async_copy(rows_v.at[b], out_hbm.at[pl.ds(base + i * chunk, chunk)], wsem.at[b])

        for b in range(SC_BUFFERS):
            gather_start(b, b)

        @pl.loop(0, n_chunks, step=SC_BUFFERS)
        def _(i0):
            for b in range(SC_BUFFERS):
                i = i0 + b
                gather(b).wait()
                write(i, b).start()
                write(i, b).wait()

                @pl.when(i + SC_BUFFERS < n_chunks)
                def _():
                    gather_start(i + SC_BUFFERS, b)

    return pl.kernel(
        body, mesh=_sc_mesh(),
        out_type=jax.ShapeDtypeStruct((n_out, d), table.dtype),
        scratch_types=[pltpu.VMEM((SC_BUFFERS, chunk), jnp.int32),
                       pltpu.VMEM((SC_BUFFERS, chunk, d), table.dtype),
                       pltpu.SemaphoreType.DMA((SC_BUFFERS,)),
                       pltpu.SemaphoreType.DMA((SC_BUFFERS,))],
        name="moe_gather",
    )(table, idx)


def _moe_kernel(be_ref, valid_ref, x_ref, wgu_ref, bgu_ref, wd_ref, bd_ref, y_ref):
    del be_ref
    valid = valid_ref[pl.program_id(0)]

    @pl.when(valid > 0)
    def _():
        row = lax.broadcasted_iota(jnp.int32, x_ref.shape, 0)
        x_hi, x_lo = _unpack_bf16_pairs(jnp.where(row < valid, x_ref[...], 0))
        half = x_ref.shape[1]
        gu = (_dot(x_hi.astype(BF16), wgu_ref[:half, :]) + _dot(x_lo.astype(BF16), wgu_ref[half:, :])
              + bgu_ref[...])
        gate = jnp.minimum(gu[:, :D_EXPERT], SWIGLU_LIMIT)
        up = jnp.clip(gu[:, D_EXPERT:], -SWIGLU_LIMIT, SWIGLU_LIMIT)
        hmid = ((up + 1.0) * gate * _sigmoid(SWIGLU_ALPHA * gate)).astype(BF16)
        y_ref[...] = _pack_bf16_pairs(_dot(hmid, wd_ref[...]) + bd_ref[...])


def _moe_experts(slots, block_e, block_valid, p):
    n_slots, dh = slots.shape
    d = 2 * dh
    n_blocks = block_e.shape[0]
    block_rows = n_slots // n_blocks
    grid_spec = pltpu.PrefetchScalarGridSpec(
        num_scalar_prefetch=2,
        grid=(n_blocks,),
        in_specs=[pl.BlockSpec((block_rows, dh), lambda i, be, bv: (i, 0)),
                  pl.BlockSpec((None, d, 2 * D_EXPERT), lambda i, be, bv: (be[i], 0, 0)),
                  pl.BlockSpec((None, 1, 2 * D_EXPERT), lambda i, be, bv: (be[i], 0, 0)),
                  pl.BlockSpec((None, D_EXPERT, d), lambda i, be, bv: (be[i], 0, 0)),
                  pl.BlockSpec((None, 1, d), lambda i, be, bv: (be[i], 0, 0))],
        out_specs=pl.BlockSpec((block_rows, dh), lambda i, be, bv: (i, 0)),
    )
    return pl.pallas_call(
        _moe_kernel,
        grid_spec=grid_spec,
        out_shape=jax.ShapeDtypeStruct((n_slots, dh), jnp.int32),
        compiler_params=pltpu.CompilerParams(dimension_semantics=("arbitrary",),
                                             vmem_limit_bytes=VMEM_LIMIT),
        name="moe_experts",
    )(block_e, block_valid, slots, p['w_gate_up'], p['b_gate_up'], p['w_down'], p['b_down'])


def _combine_kernel(h_ref, gate_ref, y_ref, gfin_ref, out_ref):
    half = y_ref.shape[2]
    acc_hi = h_ref[:, :half]
    acc_lo = h_ref[:, half:]
    gates = gate_ref[...].T
    for j in range(TOP_K):
        y_hi, y_lo = _unpack_bf16_pairs(y_ref[j])
        acc_hi = acc_hi + gates[:, j:j + 1] * y_hi
        acc_lo = acc_lo + gates[:, j:j + 1] * y_lo
    out_ref[...] = _rmsnorm(jnp.concatenate([acc_hi, acc_lo], axis=1), gfin_ref[...])


def _combine(h, gates, y_tok, g_final, tm):
    n, d = h.shape
    return pl.pallas_call(
        _combine_kernel,
        grid=(n // tm,),
        in_specs=[pl.BlockSpec((tm, d), lambda i: (i, 0)),
                  pl.BlockSpec((2 * TOP_K, tm), lambda i: (0, i)),
                  pl.BlockSpec((TOP_K, tm, d // 2), lambda i: (0, i, 0)),
                  _const_spec(g_final.shape)],
        out_specs=pl.BlockSpec((tm, d), lambda i: (i, 0)),
        out_shape=jax.ShapeDtypeStruct((n, d), F32),
        compiler_params=pltpu.CompilerParams(dimension_semantics=("arbitrary",),
                                             vmem_limit_bytes=VMEM_LIMIT),
        name="moe_combine",
    )(h, gates, y_tok, g_final)


def _moe_outputs(xn, meta, counts, p):
    n, d = xn.shape
    n_rows = n * TOP_K
    block_rows = max(MOE_ROWS_MIN, min(MOE_ROWS_MAX, n_rows // N_EXPERTS))
    n_blocks = n_rows // block_rows + N_EXPERTS
    n_slots = n_blocks * block_rows
    cnt = counts[:, 0].astype(jnp.int32)
    padded = (cnt + block_rows - 1) // block_rows * block_rows
    pad_end = jnp.cumsum(padded)
    pad_start = pad_end - padded
    block_row0 = jnp.arange(n_blocks, dtype=jnp.int32) * block_rows
    block_e = jnp.minimum(jnp.sum((pad_end[None, :] <= block_row0[:, None]).astype(jnp.int32), axis=1),
                          N_EXPERTS - 1)
    used_end = jnp.sum(jnp.where(block_e[:, None] == jnp.arange(N_EXPERTS)[None, :], (pad_start + cnt)[None, :], 0),
                       axis=1)
    block_valid = jnp.clip(used_end - block_row0, 0, block_rows)
    expert = meta[:TOP_K]
    start = jnp.zeros_like(expert)
    for e in range(N_EXPERTS):
        start = jnp.where(expert == e, pad_start[e], start)
    dest = (start + meta[TOP_K:]).astype(jnp.int32)
    slots = _dispatch(xn, dest, n_slots)
    y_slots = _moe_experts(slots, block_e, block_valid, p)
    return _gather_rows(y_slots, dest.reshape(-1)).reshape(TOP_K, n, d)


def _prepare(g_mix, w_in, conv_w, conv_b, dt_bias, a_log, d_skip, g_ssm, w_ssm_branch, lb_logits, g_hgrn,
             w_hgrn_branch, w_out, g_mem, w_xk, w_xv, g_xattn, w_xq, w_xo, g_moe, w_router, b_router,
             w_gate_up, b_gate_up, w_down, b_down):
    d = D_MODEL
    w = w_in[0]
    o_z, o_xbc = 0, SSM_INNER
    o_dt = o_xbc + CONV_CH
    o_q = o_dt + SSM_HEADS
    o_ga = o_q + 4 * d
    o_gb = o_ga + d

    def row(v):
        return v.reshape(1, -1).astype(F32)

    def lane_pad(v, fill=0.0):
        return jnp.pad(v.reshape(1, -1).astype(F32), ((0, 0), (0, LANES - v.shape[-1])), constant_values=fill)

    head_of_col = np.arange(SSM_INNER) // SSM_HEAD_DIM
    expand = (np.arange(LANES)[:, None] == head_of_col[None, :])
    return dict(
        g_mix=row(g_mix[0]),
        w_z=w[:, o_z:o_z + SSM_INNER].astype(BF16),
        w_xbc=w[:, o_xbc:o_xbc + CONV_CH].astype(BF16),
        w_dt=jnp.pad(w[:, o_dt:o_dt + SSM_HEADS], ((0, 0), (0, LANES - SSM_HEADS))).astype(BF16),
        w_hg=w[:, o_q:o_q + 4 * d].astype(BF16),
        w_ga=w[:, o_ga:o_ga + d].astype(BF16),
        w_gb=w[:, o_gb:o_gb + d].astype(BF16),
        conv_w=conv_w[0].astype(F32),
        conv_b=row(conv_b[0]),
        dt_bias=lane_pad(dt_bias[0]),
        a_log=lane_pad(a_log[0]),
        d_skip=row(jnp.repeat(d_skip[0], SSM_HEAD_DIM)),
        g_ssm=row(g_ssm[0]),
        w_ssm_branch=w_ssm_branch[0].astype(BF16),
        lb_logits=lb_logits.astype(F32),
        g_hgrn=row(g_hgrn[0]),
        w_hgrn_branch=w_hgrn_branch[0].astype(BF16),
        w_out=w_out[0].astype(BF16),
        g_mem=row(g_mem[0]),
        w_xk=w_xk[0].astype(BF16),
        w_xv=w_xv[0].astype(BF16),
        g_xattn=row(g_xattn[0]),
        w_xq=w_xq[0].astype(BF16),
        w_xo=w_xo[0].astype(BF16),
        g_moe=row(g_moe[0]),
        w_router=jnp.pad(w_router[0], ((0, 0), (0, LANES - N_EXPERTS))).astype(BF16),
        b_router=lane_pad(b_router[0], fill=-jnp.inf),
        w_gate_up=w_gate_up[0].astype(BF16),
        b_gate_up=b_gate_up[0].reshape(N_EXPERTS, 1, 2 * D_EXPERT).astype(F32),
        w_down=w_down[0].astype(BF16),
        b_down=b_down[0].reshape(N_EXPERTS, 1, d).astype(F32),
        expand=jnp.asarray(expand, BF16),
    )


def _group_to_experts(x, conv0, ssm0, hg0, mk, mv, p):
    b, t, d = x.shape
    ma, conv_n, ssm_n = _ssd_mixer(x, conv0, ssm0, p, min(SSD_TILE, t))
    mb, hg_n = _hgrn_mixer(x, hg0, p, min(HGRN_TILE, t))
    tm = min(POST_TILE, t)
    cnt0 = jnp.zeros((N_EXPERTS, LANES), F32)
    h2, xn3, meta, gates, counts = _post_mixer(x, ma, mb, mk, mv, cnt0, p, tm)
    n = b * t
    meta = meta.transpose(1, 0, 2).reshape(2 * TOP_K, n)
    gates = gates.transpose(1, 0, 2).reshape(2 * TOP_K, n)
    y_tok = _moe_outputs(xn3.reshape(n, d // 2), meta, counts, p)
    return (h2.reshape(n, d), gates, y_tok), (conv_n, ssm_n, hg_n)


def _group_combine(h2, gates, y_tok, g_final, shape):
    return _combine(h2, gates, y_tok, g_final, min(COMBINE_TILE, h2.shape[0])).reshape(shape)


def kernel(x_prompt, x_sample, mem_prompt, state_conv, state_ssm, state_hgrn, cache_mem_k, cache_mem_v, g_mix, w_in, conv_w, conv_b, dt_bias, a_log, d_skip, g_ssm, w_ssm_branch, lb_logits, g_hgrn, w_hgrn_branch, w_out, g_mem, w_xk, w_xv, g_xattn, w_xq, w_xo, g_moe, w_router, b_router, w_gate_up, b_gate_up, w_down, b_down, g_final):
    p = _prepare(g_mix, w_in, conv_w, conv_b, dt_bias, a_log, d_skip, g_ssm, w_ssm_branch, lb_logits, g_hgrn,
                 w_hgrn_branch, w_out, g_mem, w_xk, w_xv, g_xattn, w_xq, w_xo, g_moe, w_router, b_router,
                 w_gate_up, b_gate_up, w_down, b_down)
    g_fin = g_final.reshape(1, -1).astype(F32)
    bp = x_prompt.shape[0]
    bs = x_sample.shape[0]
    d = D_MODEL

    mk_p, mv_p = _memory_kv(mem_prompt, p)
    moe_p, (conv_p, ssm_p, hg_p) = _group_to_experts(
        x_prompt,
        jnp.zeros((bp, CONV_W - 1, CONV_CH), F32),
        jnp.zeros((bp, SSM_HEADS, SSM_HEAD_DIM, SSM_STATE), F32),
        jnp.zeros((bp, HG_HEADS, HG_K, HG_K), F32),
        mk_p, mv_p, p)
    moe_s, (conv_s, ssm_s, hg_s) = _group_to_experts(
        x_sample, state_conv[0], state_ssm[0], state_hgrn[0],
        cache_mem_k[0].reshape(bs, N_MEM, d), cache_mem_v[0].reshape(bs, N_MEM, d), p)
    y_p = _group_combine(*moe_p, g_fin, x_prompt.shape)
    y_s = _group_combine(*moe_s, g_fin, x_sample.shape)

    kv_shape = (1, bp, N_MEM, X_HEADS, X_HEAD_DIM)
    return (y_p, y_s,
            conv_p[None], ssm_p[None], hg_p[None], mk_p.reshape(kv_shape), mv_p.reshape(kv_shape),
            conv_s[None], ssm_s[None], hg_s[None])
```

```python
import jax
import jax.numpy as jnp
import numpy as np
from jax import lax
from jax.experimental import pallas as pl
from jax.experimental.pallas import tpu as pltpu
from jax.experimental.pallas import tpu_sc as plsc

F32 = jnp.float32
BF16 = jnp.bfloat16

D_MODEL = 1024
CHUNK = 64
EPS = 1e-6
SSM_INNER = 2 * D_MODEL
SSM_HEAD_DIM = 64
SSM_HEADS = SSM_INNER // SSM_HEAD_DIM
SSM_GROUPS = 4
SSM_STATE = 128
GROUP_COLS = SSM_INNER // SSM_GROUPS
CONV_W = 4
BC_COLS = SSM_GROUPS * SSM_STATE
CONV_CH = SSM_INNER + 2 * BC_COLS
HG_HEADS = 8
HG_K = D_MODEL // HG_HEADS
HG_SUB = CHUNK // 2
N_MEM = 256
X_HEADS = 4
X_HEAD_DIM = D_MODEL // X_HEADS
N_EXPERTS = 32
TOP_K = 4
D_EXPERT = D_MODEL
SWIGLU_LIMIT = 7.0
SWIGLU_ALPHA = 1.702

LANES = 128
CONV_PAD = 8
MOE_ROWS_MAX = 512
MOE_ROWS_MIN = 128
SSD_TILE = 256
HGRN_TILE = 256
POST_TILE = 512
COMBINE_TILE = 512
VMEM_LIMIT = 56 * 1024 * 1024
SC_CORES = 2
SC_SUBCORES = 16
SC_WORKERS = SC_CORES * SC_SUBCORES
SC_CHUNK = 64
SC_BUFFERS = 2


def _const_spec(shape):
    nd = len(shape)
    return pl.BlockSpec(shape, lambda *_: (0,) * nd, pipeline_mode=pl.Buffered(1))


def _dot(a, b):
    return jnp.dot(a, b, preferred_element_type=F32)


def _dot_nt(a, b):
    return lax.dot_general(a, b, (((1,), (1,)), ((), ())), preferred_element_type=F32)


def _dot_tn(a, b):
    return lax.dot_general(a, b, (((0,), (0,)), ((), ())), preferred_element_type=F32)


def _split3(a):
    hi = a.astype(BF16)
    r1 = a - hi.astype(F32)
    mid = r1.astype(BF16)
    lo = (r1 - mid.astype(F32)).astype(BF16)
    return hi, mid, lo


def _exact_dot_lhs01(sel, a):
    hi, mid, lo = _split3(a)
    return _dot(sel, hi) + _dot(sel, mid) + _dot(sel, lo)


def _rmsnorm(x, g):
    return x * lax.rsqrt(jnp.mean(x * x, axis=-1, keepdims=True) + EPS) * g


def _sigmoid(x):
    return jax.nn.sigmoid(x)


def _silu(x):
    return x * jax.nn.sigmoid(x)


def _softplus(x):
    return jnp.maximum(x, 0.0) + jnp.log1p(jnp.exp(-jnp.abs(x)))


def _pack_bf16_pairs(x):
    c = x.shape[1] // 2
    hi = lax.bitcast_convert_type(x[:, :c].astype(BF16).astype(F32), jnp.int32)
    lo = lax.bitcast_convert_type(x[:, c:].astype(BF16).astype(F32), jnp.int32)
    return hi | lax.shift_right_logical(lo, 16)


def _unpack_bf16_pairs(u):
    hi = lax.bitcast_convert_type(u & jnp.int32(-65536), F32)
    lo = lax.bitcast_convert_type(lax.shift_left(u, 16), F32)
    return hi, lo


def _ssd_kernel(x_ref, conv0_ref, ssm0_ref, g_ref, wxbc_ref, wz_ref, wdt_ref, wga_ref, convw_ref,
                convb_ref, dtb_ref, alog_ref, dskip_ref, gssm_ref, wbr_ref, tri_ref, expand_ref,
                ma_ref, convn_ref, ssmn_ref,
                xpad, xact, xdt, ysc, st):
    tt = x_ref.shape[0]
    ti = pl.program_id(1)
    nt = pl.num_programs(1)

    @pl.when(ti == 0)
    def _():
        xpad[CONV_PAD - (CONV_W - 1):CONV_PAD, :] = conv0_ref[...]
        st[...] = ssm0_ref[...].reshape(SSM_INNER, SSM_STATE).T

    xn = _rmsnorm(x_ref[...], g_ref[...]).astype(BF16)
    xpad[CONV_PAD:CONV_PAD + tt, :] = _dot(xn, wxbc_ref[...])

    conv = convb_ref[...] + xpad[CONV_PAD:CONV_PAD + tt, :] * convw_ref[CONV_W - 1:CONV_W, :]
    for k in range(1, CONV_W):
        conv = conv + xpad[CONV_PAD - k:CONV_PAD - k + tt, :] * convw_ref[CONV_W - 1 - k:CONV_W - k, :]
    xact[...] = _silu(conv)
    tail = xpad[CONV_PAD + tt - (CONV_W - 1):CONV_PAD + tt, :]
    xpad[CONV_PAD - (CONV_W - 1):CONV_PAD, :] = tail

    dt = _softplus(_dot(xn, wdt_ref[...]) + dtb_ref[...])
    acum_all = _exact_dot_lhs01(tri_ref[...], dt * -jnp.exp(alog_ref[...]))
    dt_hi = dt.astype(BF16)
    dt_lo = (dt - dt_hi.astype(F32)).astype(BF16)
    dt_x = _dot(dt_hi, expand_ref[...]) + _dot(dt_lo, expand_ref[...])
    xdt[...] = xact[:, :SSM_INNER] * dt_x

    lane = lax.broadcasted_iota(jnp.int32, (CHUNK, LANES), 1)
    row = lax.broadcasted_iota(jnp.int32, (CHUNK, LANES), 0)
    causal2 = row >= (lane % CHUNK)
    diag2 = row == (lane % CHUNK)
    left = lane < CHUNK
    pair_of_lane = lane // CHUNK

    for c in range(tt // CHUNK):
        rows = slice(c * CHUNK, (c + 1) * CHUNK)
        acum = acum_all[rows, :]
        for g in range(SSM_GROUPS):
            gs = slice(g * GROUP_COLS, (g + 1) * GROUP_COLS)
            b_g = xact[rows, SSM_INNER + g * SSM_STATE:SSM_INNER + (g + 1) * SSM_STATE].astype(BF16)
            c_g = xact[rows, SSM_INNER + BC_COLS + g * SSM_STATE:
                       SSM_INNER + BC_COLS + (g + 1) * SSM_STATE].astype(BF16)
            cb2 = _dot_nt(c_g, jnp.concatenate([b_g, b_g], axis=0))
            st_g = st[:, gs]
            y_inter = _dot(c_g, st_g.astype(BF16))
            xw, decay = [], []
            for j in range(GROUP_COLS // LANES):
                pair = g * (GROUP_COLS // LANES) + j
                ps = slice(pair * LANES, (pair + 1) * LANES)
                a_col = jnp.take_along_axis(acum, pair_of_lane + 2 * pair, axis=1)
                a_row = jnp.sum(jnp.where(diag2, a_col, 0.0), axis=0, keepdims=True)
                a_last = a_col[CHUNK - 1:CHUNK, :]
                dec = jnp.where(causal2, jnp.exp(jnp.minimum(a_col - a_row, 0.0)), 0.0)
                wts = (cb2 * dec).astype(BF16)
                xp = xdt[rows, ps]
                xbd = jnp.concatenate([jnp.where(left, xp, 0.0), jnp.where(left, 0.0, xp)],
                                      axis=0).astype(BF16)
                ysc[rows, ps] = _dot(wts, xbd) + y_inter[:, j * LANES:(j + 1) * LANES] * jnp.exp(a_col)
                xw.append((jnp.exp(a_last - a_col) * xp).astype(BF16))
                decay.append(jnp.exp(a_last))
            st[:, gs] = jnp.concatenate(decay, axis=1) * st_g + _dot_tn(b_g, jnp.concatenate(xw, axis=1))

    xs = xact[:, :SSM_INNER]
    y = ysc[...] + dskip_ref[...] * xs
    yz = y * _silu(_dot(xn, wz_ref[...]))
    parts = []
    for g in range(SSM_GROUPS):
        blk = yz[:, g * GROUP_COLS:(g + 1) * GROUP_COLS]
        parts.append(blk * lax.rsqrt(jnp.mean(blk * blk, axis=-1, keepdims=True) + EPS))
    yn = (jnp.concatenate(parts, axis=1) * gssm_ref[...]).astype(BF16)
    ya = _dot(yn, wbr_ref[...])
    ma_ref[...] = _sigmoid(_dot(xn, wga_ref[...])) * ya

    @pl.when(ti == nt - 1)
    def _():
        convn_ref[...] = tail
        ssmn_ref[...] = st[...].T.reshape(SSM_HEADS, SSM_HEAD_DIM, SSM_STATE)


def _chunk_tri(tt):
    return jnp.asarray(np.kron(np.eye(tt // CHUNK), np.tril(np.ones((CHUNK, CHUNK)))), BF16)


def _ssd_mixer(x, conv0, ssm0, p, tt):
    b, t, d = x.shape
    grid = (b, t // tt)
    row_spec = pl.BlockSpec((None, tt, d), lambda i, j: (i, j, 0))
    consts = [p['g_mix'], p['w_xbc'], p['w_z'], p['w_dt'], p['w_ga'], p['conv_w'], p['conv_b'], p['dt_bias'],
              p['a_log'], p['d_skip'], p['g_ssm'], p['w_ssm_branch'], _chunk_tri(tt), p['expand']]
    return pl.pallas_call(
        _ssd_kernel,
        grid=grid,
        in_specs=[row_spec,
                  pl.BlockSpec((None, None, CONV_W - 1, CONV_CH), lambda i, j: (0, i, 0, 0)),
                  pl.BlockSpec((None, None, SSM_HEADS, SSM_HEAD_DIM, SSM_STATE), lambda i, j: (0, i, 0, 0, 0))]
                 + [_const_spec(c.shape) for c in consts],
        out_specs=[row_spec,
                   pl.BlockSpec((None, CONV_W - 1, CONV_CH), lambda i, j: (i, 0, 0)),
                   pl.BlockSpec((None, SSM_HEADS, SSM_HEAD_DIM, SSM_STATE), lambda i, j: (i, 0, 0, 0))],
        out_shape=[jax.ShapeDtypeStruct((b, t, d), F32),
                   jax.ShapeDtypeStruct((b, CONV_W - 1, CONV_CH), F32),
                   jax.ShapeDtypeStruct((b, SSM_HEADS, SSM_HEAD_DIM, SSM_STATE), F32)],
        scratch_shapes=[pltpu.VMEM((CONV_PAD + tt, CONV_CH), F32),
                        pltpu.VMEM((tt, CONV_CH), F32),
                        pltpu.VMEM((tt, SSM_INNER), F32),
                        pltpu.VMEM((tt, SSM_INNER), F32),
                        pltpu.VMEM((SSM_STATE, SSM_INNER), F32)],
        compiler_params=pltpu.CompilerParams(dimension_semantics=("arbitrary", "arbitrary"),
                                             vmem_limit_bytes=VMEM_LIMIT),
        name="ssd_mixer",
    )(x, conv0, ssm0, *consts)


def _hgrn_kernel(x_ref, hg0_ref, g_ref, wh_ref, wgb_ref, lbl_ref, ghg_ref, wbr_ref, tri_ref,
                 mb_ref, hgn_ref,
                 osc, st):
    tt = x_ref.shape[0]
    n_chunks = tt // CHUNK
    ti = pl.program_id(1)
    nt = pl.num_programs(1)

    @pl.when(ti == 0)
    def _():
        for h in range(HG_HEADS):
            st[h] = hg0_ref[h].T

    xn = _rmsnorm(x_ref[...], g_ref[...]).astype(BF16)
    proj = _dot(xn, wh_ref[...])
    l0 = lbl_ref[0:1, :]
    l1 = lbl_ref[1:2, :]
    lmax = jnp.maximum(l0, l1)
    e0 = jnp.exp(l0 - lmax)
    lb = e0 / (e0 + jnp.exp(l1 - lmax))
    fr = proj[:, D_MODEL:2 * D_MODEL]
    kk = (1.0 - lb) * _sigmoid(-fr)
    v = _silu(proj[:, 2 * D_MODEL:3 * D_MODEL]).astype(BF16)
    gc = _exact_dot_lhs01(tri_ref[...], jnp.log(lb + (1.0 - lb) * _sigmoid(fr)))
    q = proj[:, :D_MODEL]
    qh = (q * jnp.exp(gc)).astype(BF16)
    mid = []
    for m in range(tt // HG_SUB):
        r = m * HG_SUB + HG_SUB // 2 - 1
        mid.append(jnp.broadcast_to(gc[r:r + 1, :], (HG_SUB, D_MODEL)))
    d_mid = gc - jnp.concatenate(mid, axis=0)
    q_sub = (q * jnp.exp(d_mid)).astype(BF16)
    k_sub = (kk * jnp.exp(-d_mid)).astype(BF16)
    q_far, k_far, v_first = [], [], []
    for c in range(n_chunks):
        first = slice(c * CHUNK, c * CHUNK + HG_SUB)
        second = slice(c * CHUNK + HG_SUB, (c + 1) * CHUNK)
        g_half = gc[c * CHUNK + HG_SUB - 1:c * CHUNK + HG_SUB, :]
        q_far.append(q[second, :] * jnp.exp(gc[second, :] - g_half))
        k_far.append(kk[first, :] * jnp.exp(g_half - gc[first, :]))
        v_first.append(v[first, :])
    q_far = jnp.concatenate(q_far, axis=0).astype(BF16)
    k_far = jnp.concatenate(k_far, axis=0).astype(BF16)
    v_first = jnp.concatenate(v_first, axis=0)
    kt, decay = [], []
    for c in range(n_chunks):
        rows = slice(c * CHUNK, (c + 1) * CHUNK)
        glast = gc[(c + 1) * CHUNK - 1:(c + 1) * CHUNK, :]
        kt.append((kk[rows, :] * jnp.exp(glast - gc[rows, :])).astype(BF16))
        decay.append(jnp.exp(glast))

    row = lax.broadcasted_iota(jnp.int32, (tt, tt), 0)
    col = lax.broadcasted_iota(jnp.int32, (tt, tt), 1)
    near = jnp.logical_and(row >= col, row // HG_SUB == col // HG_SUB)
    same_chunk = (row // HG_SUB == col // HG_SUB)[:tt // 2, :tt // 2]
    no_far = jnp.zeros((HG_SUB, HG_K), F32)

    for h in range(HG_HEADS):
        hs = slice(h * HG_K, (h + 1) * HG_K)
        att = jnp.where(near, _dot_nt(q_sub[:, hs], k_sub[:, hs]), 0.0).astype(BF16)
        att_far = jnp.where(same_chunk, _dot_nt(q_far[:, hs], k_far[:, hs]), 0.0).astype(BF16)
        o_far = _dot(att_far, v_first[:, hs])
        far_rows = []
        for c in range(n_chunks):
            far_rows += [no_far, o_far[c * HG_SUB:(c + 1) * HG_SUB, :]]
        o_intra = _dot(att, v[:, hs]) + jnp.concatenate(far_rows, axis=0)
        s = st[h]
        o_inter = []
        for c in range(n_chunks):
            rows = slice(c * CHUNK, (c + 1) * CHUNK)
            o_inter.append(_dot_nt(qh[rows, hs], s.astype(BF16)))
            s = decay[c][:, hs] * s + _dot_tn(v[rows, hs], kt[c][:, hs])
        st[h] = s
        osc[:, hs] = o_intra + jnp.concatenate(o_inter, axis=0)

    parts = []
    for h in range(HG_HEADS):
        blk = osc[:, h * HG_K:(h + 1) * HG_K]
        parts.append(blk * lax.rsqrt(jnp.mean(blk * blk, axis=-1, keepdims=True) + EPS))
    on = (jnp.concatenate(parts, axis=1) * ghg_ref[...] * _silu(proj[:, 3 * D_MODEL:])).astype(BF16)
    yb = _dot(on, wbr_ref[...])
    mb_ref[...] = _sigmoid(_dot(xn, wgb_ref[...])) * yb

    @pl.when(ti == nt - 1)
    def _():
        for h in range(HG_HEADS):
            hgn_ref[h] = st[h].T


def _hgrn_mixer(x, hg0, p, tt):
    b, t, d = x.shape
    grid = (b, t // tt)
    row_spec = pl.BlockSpec((None, tt, d), lambda i, j: (i, j, 0))
    st_spec = pl.BlockSpec((None, HG_HEADS, HG_K, HG_K), lambda i, j: (i, 0, 0, 0))
    consts = [p['g_mix'], p['w_hg'], p['w_gb'], p['lb_logits'], p['g_hgrn'], p['w_hgrn_branch'], _chunk_tri(tt)]
    return pl.pallas_call(
        _hgrn_kernel,
        grid=grid,
        in_specs=[row_spec, pl.BlockSpec((None, None, HG_HEADS, HG_K, HG_K), lambda i, j: (0, i, 0, 0, 0))]
                 + [_const_spec(c.shape) for c in consts],
        out_specs=[row_spec, st_spec],
        out_shape=[jax.ShapeDtypeStruct((b, t, d), F32),
                   jax.ShapeDtypeStruct((b, HG_HEADS, HG_K, HG_K), F32)],
        scratch_shapes=[pltpu.VMEM((tt, d), F32), pltpu.VMEM((HG_HEADS, HG_K, HG_K), F32)],
        compiler_params=pltpu.CompilerParams(dimension_semantics=("arbitrary", "arbitrary"),
                                             vmem_limit_bytes=VMEM_LIMIT),
        name="hgrn_mixer",
    )(x, hg0, *consts)


def _memkv_kernel(m_ref, g_ref, wk_ref, wv_ref, k_ref, v_ref):
    mn = _rmsnorm(m_ref[...], g_ref[...]).astype(BF16)
    k_ref[...] = _dot(mn, wk_ref[...])
    v_ref[...] = _dot(mn, wv_ref[...])


def _memory_kv(mem, p):
    b, n, d = mem.shape
    spec = pl.BlockSpec((None, n, d), lambda i: (i, 0, 0))
    consts = [p['g_mem'], p['w_xk'], p['w_xv']]
    return pl.pallas_call(
        _memkv_kernel,
        grid=(b,),
        in_specs=[spec] + [_const_spec(c.shape) for c in consts],
        out_specs=[spec, spec],
        out_shape=[jax.ShapeDtypeStruct((b, n, d), F32)] * 2,
        compiler_params=pltpu.CompilerParams(dimension_semantics=("arbitrary",),
                                             vmem_limit_bytes=VMEM_LIMIT),
        name="memory_kv",
    )(mem, *consts)


def _post_kernel(x_ref, ma_ref, mb_ref, mk_ref, mv_ref, cnt0_ref, wo_ref, gx_ref, wq_ref, wxo_ref,
                 gmoe_ref, wr_ref, br_ref, triu_ref,
                 h_ref, xn_ref, meta_ref, gate_ref, cnt_ref,
                 base):
    tm = x_ref.shape[0]
    first = jnp.logical_and(pl.program_id(0) == 0, pl.program_id(1) == 0)

    @pl.when(first)
    def _():
        base[...] = cnt0_ref[...]

    m = (ma_ref[...] + mb_ref[...]).astype(BF16)
    h1 = x_ref[...] + _dot(m, wo_ref[...])

    hn = _rmsnorm(h1, gx_ref[...]).astype(BF16)
    q = _dot(hn, wq_ref[...])
    heads = []
    for hh in range(X_HEADS):
        hs = slice(hh * X_HEAD_DIM, (hh + 1) * X_HEAD_DIM)
        s = _dot_nt(q[:, hs].astype(BF16), mk_ref[:, hs].astype(BF16)) * (X_HEAD_DIM ** -0.5)
        s = s - jnp.max(s, axis=-1, keepdims=True)
        e = jnp.exp(s)
        pr = e / jnp.sum(e, axis=-1, keepdims=True)
        heads.append(_dot(pr.astype(BF16), mv_ref[:, hs].astype(BF16)))
    o = jnp.concatenate(heads, axis=1).astype(BF16)
    h2 = h1 + _dot(o, wxo_ref[...])
    h_ref[...] = h2

    xn3 = _rmsnorm(h2, gmoe_ref[...])
    xn_ref[...] = _pack_bf16_pairs(xn3)
    logits = _dot(xn3.astype(BF16), wr_ref[...]) + br_ref[...]

    run = logits.T[:N_EXPERTS, :]
    eid = lax.broadcasted_iota(jnp.int32, (N_EXPERTS, tm), 0).astype(F32)
    vals, ids, hots = [], [], []
    for _ in range(TOP_K):
        mx = jnp.max(run, axis=0, keepdims=True)
        idx = jnp.min(jnp.where(run == mx, eid, float(N_EXPERTS)), axis=0, keepdims=True)
        hot = eid == idx
        run = jnp.where(hot, -jnp.inf, run)
        vals.append(mx)
        ids.append(idx)
        hots.append(hot)
    es = [jnp.exp(v - vals[0]) for v in vals]
    den = es[0] + es[1] + es[2] + es[3]

    tot = jnp.zeros((N_EXPERTS, tm), F32)
    for hot in hots:
        tot = tot + hot.astype(F32)
    before = base[:, :1] + _dot(tot.astype(BF16), triu_ref[...])
    base[...] = base[...] + jnp.sum(tot, axis=1, keepdims=True)
    cnt_ref[...] = base[...]

    ranks = [jnp.sum(jnp.where(hot, before, 0.0), axis=0, keepdims=True) for hot in hots]
    meta_ref[...] = jnp.concatenate(ids + ranks, axis=0).astype(jnp.int32)
    gate_ref[...] = jnp.concatenate([e / den for e in es] + [jnp.zeros_like(den)] * TOP_K, axis=0)


def _post_mixer(x, ma, mb, mk, mv, cnt0, p, tm):
    b, t, d = x.shape
    grid = (b, t // tm)
    row_spec = pl.BlockSpec((None, tm, d), lambda i, j: (i, j, 0))
    small_spec = pl.BlockSpec((None, 2 * TOP_K, tm), lambda i, j: (i, 0, j))
    mem_spec = pl.BlockSpec((None, N_MEM, d), lambda i, j: (i, 0, 0))
    cnt_spec = pl.BlockSpec((N_EXPERTS, LANES), lambda i, j: (0, 0))
    triu = jnp.triu(jnp.ones((tm, tm), BF16), 1)
    consts = [p['w_out'], p['g_xattn'], p['w_xq'], p['w_xo'], p['g_moe'], p['w_router'], p['b_router'], triu]
    return pl.pallas_call(
        _post_kernel,
        grid=grid,
        in_specs=[row_spec, row_spec, row_spec, mem_spec, mem_spec, cnt_spec]
                 + [_const_spec(c.shape) for c in consts],
        out_specs=[row_spec, pl.BlockSpec((None, tm, d // 2), lambda i, j: (i, j, 0)), small_spec, small_spec,
                   cnt_spec],
        out_shape=[jax.ShapeDtypeStruct((b, t, d), F32),
                   jax.ShapeDtypeStruct((b, t, d // 2), jnp.int32),
                   jax.ShapeDtypeStruct((b, 2 * TOP_K, t), jnp.int32),
                   jax.ShapeDtypeStruct((b, 2 * TOP_K, t), F32),
                   jax.ShapeDtypeStruct((N_EXPERTS, LANES), F32)],
        scratch_shapes=[pltpu.VMEM((N_EXPERTS, LANES), F32)],
        compiler_params=pltpu.CompilerParams(dimension_semantics=("arbitrary", "arbitrary"),
                                             vmem_limit_bytes=VMEM_LIMIT),
        name="post_mixer",
    )(x, ma, mb, mk, mv, cnt0, *consts)


def _sc_chunk(rows_per_worker):
    chunk = SC_CHUNK
    while rows_per_worker % (SC_BUFFERS * chunk):
        chunk //= 2
    assert chunk % 8 == 0, rows_per_worker
    return chunk


def _sc_mesh():
    return plsc.VectorSubcoreMesh(core_axis_name="c", subcore_axis_name="s")


def _sc_worker():
    return lax.axis_index("s") * SC_CORES + lax.axis_index("c")


def _dispatch(xn, dest, n_slots):
    n, d = xn.shape
    per_w = n // SC_WORKERS
    chunk = _sc_chunk(per_w)
    n_chunks = per_w // chunk
    idx = dest.reshape(TOP_K, n // chunk, chunk).transpose(1, 0, 2)

    def body(x_hbm, idx_hbm, out_hbm, idx_v, rows_v, rsem, wsem):
        wid = _sc_worker()

        def read(i, b):
            blk = wid * n_chunks + i
            return pltpu.make_async_copy(x_hbm.at[pl.ds(blk * chunk, chunk)], rows_v.at[b], rsem.at[b])

        def read_start(i, b):
            pltpu.sync_copy(idx_hbm.at[wid * n_chunks + i], idx_v.at[b])
            read(i, b).start()

        def write(b, j):
            return pltpu.make_async_copy(rows_v.at[b], out_hbm.at[idx_v.at[b, j]], wsem.at[b])

        for b in range(SC_BUFFERS):
            read_start(b, b)

        @pl.loop(0, n_chunks, step=SC_BUFFERS)
        def _(i0):
            for b in range(SC_BUFFERS):
                i = i0 + b
                read(i, b).wait()
                for j in range(TOP_K):
                    write(b, j).start()
                for j in range(TOP_K):
                    write(b, j).wait()

                @pl.when(i + SC_BUFFERS < n_chunks)
                def _():
                    read_start(i + SC_BUFFERS, b)

    return pl.kernel(
        body, mesh=_sc_mesh(),
        out_type=jax.ShapeDtypeStruct((n_slots, d), xn.dtype),
        scratch_types=[pltpu.VMEM((SC_BUFFERS, TOP_K, chunk), jnp.int32),
                       pltpu.VMEM((SC_BUFFERS, chunk, d), xn.dtype),
                       pltpu.SemaphoreType.DMA((SC_BUFFERS,)),
                       pltpu.SemaphoreType.DMA((SC_BUFFERS,))],
        name="moe_dispatch",
    )(xn, idx)


def _gather_rows(table, idx):
    n_out = idx.shape[0]
    d = table.shape[1]
    per_w = n_out // SC_WORKERS
    chunk = _sc_chunk(per_w)
    n_chunks = per_w // chunk

    def body(table_hbm, idx_hbm, out_hbm, idx_v, rows_v, gsem, wsem):
        base = _sc_worker() * per_w

        def gather(b):
            return pltpu.make_async_copy(table_hbm.at[idx_v.at[b]], rows_v.at[b], gsem.at[b])

        def gather_start(i, b):
            pltpu.sync_copy(idx_hbm.at[pl.ds(base + i * chunk, chunk)], idx_v.at[b])
            gather(b).start()

        def write(i, b):
            return pltpu.make_async_copy(rows_v.at[b], out_hbm.at[pl.ds(base + i * chunk, chunk)], wsem.at[b])

        for b in range(SC_BUFFERS):
            gather_start(b, b)

        @pl.loop(0, n_chunks, step=SC_BUFFERS)
        def _(i0):
            for b in range(SC_BUFFERS):
                i = i0 + b
                gather(b).wait()
                write(i, b).start()
                write(i, b).wait()

                @pl.when(i + SC_BUFFERS < n_chunks)
                def _():
                    gather_start(i + SC_BUFFERS, b)

    return pl.kernel(
        body, mesh=_sc_mesh(),
        out_type=jax.ShapeDtypeStruct((n_out, d), table.dtype),
        scratch_types=[pltpu.VMEM((SC_BUFFERS, chunk), jnp.int32),
                       pltpu.VMEM((SC_BUFFERS, chunk, d), table.dtype),
                       pltpu.SemaphoreType.DMA((SC_BUFFERS,)),
                       pltpu.SemaphoreType.DMA((SC_BUFFERS,))],
        name="moe_gather",
    )(table, idx)


def _moe_kernel(be_ref, valid_ref, x_ref, wgu_ref, bgu_ref, wd_ref, bd_ref, y_ref, wgu_bf, wd_bf):
    i = pl.program_id(0)
    valid = valid_ref[i]

    @pl.when(jnp.logical_or(i == 0, be_ref[i] != be_ref[jnp.maximum(i - 1, 0)]))
    def _():
        wgu_bf[...] = wgu_ref[...].astype(BF16)
        wd_bf[...] = wd_ref[...].astype(BF16)

    @pl.when(valid > 0)
    def _():
        row = lax.broadcasted_iota(jnp.int32, x_ref.shape, 0)
        x_hi, x_lo = _unpack_bf16_pairs(jnp.where(row < valid, x_ref[...], 0))
        half = x_ref.shape[1]
        gu = (_dot(x_hi.astype(BF16), wgu_bf[:half, :]) + _dot(x_lo.astype(BF16), wgu_bf[half:, :])
              + bgu_ref[...])
        gate = jnp.minimum(gu[:, :D_EXPERT], SWIGLU_LIMIT)
        up = jnp.clip(gu[:, D_EXPERT:], -SWIGLU_LIMIT, SWIGLU_LIMIT)
        hmid = ((up + 1.0) * gate * _sigmoid(SWIGLU_ALPHA * gate)).astype(BF16)
        y_ref[...] = _pack_bf16_pairs(_dot(hmid, wd_bf[...]) + bd_ref[...])


def _moe_experts(slots, block_e, block_valid, p):
    n_slots, dh = slots.shape
    d = 2 * dh
    n_blocks = block_e.shape[0]
    block_rows = n_slots // n_blocks
    grid_spec = pltpu.PrefetchScalarGridSpec(
        num_scalar_prefetch=2,
        grid=(n_blocks,),
        in_specs=[pl.BlockSpec((block_rows, dh), lambda i, be, bv: (i, 0)),
                  pl.BlockSpec((None, None, d, 2 * D_EXPERT), lambda i, be, bv: (0, be[i], 0, 0)),
                  pl.BlockSpec((None, 1, 2 * D_EXPERT), lambda i, be, bv: (be[i], 0, 0)),
                  pl.BlockSpec((None, None, D_EXPERT, d), lambda i, be, bv: (0, be[i], 0, 0)),
                  pl.BlockSpec((None, 1, d), lambda i, be, bv: (be[i], 0, 0))],
        out_specs=pl.BlockSpec((block_rows, dh), lambda i, be, bv: (i, 0)),
        scratch_shapes=[pltpu.VMEM((d, 2 * D_EXPERT), BF16), pltpu.VMEM((D_EXPERT, d), BF16)],
    )
    return pl.pallas_call(
        _moe_kernel,
        grid_spec=grid_spec,
        out_shape=jax.ShapeDtypeStruct((n_slots, dh), jnp.int32),
        compiler_params=pltpu.CompilerParams(dimension_semantics=("arbitrary",),
                                             vmem_limit_bytes=VMEM_LIMIT),
        name="moe_experts",
    )(block_e, block_valid, slots, p['w_gate_up'], p['b_gate_up'], p['w_down'], p['b_down'])


def _combine_kernel(h_ref, gate_ref, y_ref, gfin_ref, out_ref):
    half = y_ref.shape[2]
    acc_hi = h_ref[:, :half]
    acc_lo = h_ref[:, half:]
    gates = gate_ref[...].T
    for j in range(TOP_K):
        y_hi, y_lo = _unpack_bf16_pairs(y_ref[j])
        acc_hi = acc_hi + gates[:, j:j + 1] * y_hi
        acc_lo = acc_lo + gates[:, j:j + 1] * y_lo
    out_ref[...] = _rmsnorm(jnp.concatenate([acc_hi, acc_lo], axis=1), gfin_ref[...])


def _combine(h, gates, y_tok, g_final, tm):
    n, d = h.shape
    return pl.pallas_call(
        _combine_kernel,
        grid=(n // tm,),
        in_specs=[pl.BlockSpec((tm, d), lambda i: (i, 0)),
                  pl.BlockSpec((2 * TOP_K, tm), lambda i: (0, i)),
                  pl.BlockSpec((TOP_K, tm, d // 2), lambda i: (0, i, 0)),
                  _const_spec(g_final.shape)],
        out_specs=pl.BlockSpec((tm, d), lambda i: (i, 0)),
        out_shape=jax.ShapeDtypeStruct((n, d), F32),
        compiler_params=pltpu.CompilerParams(dimension_semantics=("arbitrary",),
                                             vmem_limit_bytes=VMEM_LIMIT),
        name="moe_combine",
    )(h, gates, y_tok, g_final)


def _moe_outputs(xn, meta, counts, p):
    n, d = xn.shape
    n_rows = n * TOP_K
    block_rows = max(MOE_ROWS_MIN, min(MOE_ROWS_MAX, n_rows // N_EXPERTS))
    n_blocks = n_rows // block_rows + N_EXPERTS
    n_slots = n_blocks * block_rows
    cnt = counts[:, 0].astype(jnp.int32)
    padded = (cnt + block_rows - 1) // block_rows * block_rows
    pad_end = jnp.cumsum(padded)
    pad_start = pad_end - padded
    block_row0 = jnp.arange(n_blocks, dtype=jnp.int32) * block_rows
    block_e = jnp.minimum(jnp.sum((pad_end[None, :] <= block_row0[:, None]).astype(jnp.int32), axis=1),
                          N_EXPERTS - 1)
    used_end = jnp.sum(jnp.where(block_e[:, None] == jnp.arange(N_EXPERTS)[None, :], (pad_start + cnt)[None, :], 0),
                       axis=1)
    block_valid = jnp.clip(used_end - block_row0, 0, block_rows)
    expert = meta[:TOP_K]
    start = jnp.zeros_like(expert)
    for e in range(N_EXPERTS):
        start = jnp.where(expert == e, pad_start[e], start)
    dest = (start + meta[TOP_K:]).astype(jnp.int32)
    slots = _dispatch(xn, dest, n_slots)
    y_slots = _moe_experts(slots, block_e, block_valid, p)
    return _gather_rows(y_slots, dest.reshape(-1)).reshape(TOP_K, n, d)


def _prepare(g_mix, w_in, conv_w, conv_b, dt_bias, a_log, d_skip, g_ssm, w_ssm_branch, lb_logits, g_hgrn,
             w_hgrn_branch, w_out, g_mem, w_xk, w_xv, g_xattn, w_xq, w_xo, g_moe, w_router, b_router,
             w_gate_up, b_gate_up, w_down, b_down):
    d = D_MODEL
    w = w_in[0]
    o_z, o_xbc = 0, SSM_INNER
    o_dt = o_xbc + CONV_CH
    o_q = o_dt + SSM_HEADS
    o_ga = o_q + 4 * d
    o_gb = o_ga + d

    def row(v):
        return v.reshape(1, -1).astype(F32)

    def lane_pad(v, fill=0.0):
        return jnp.pad(v.reshape(1, -1).astype(F32), ((0, 0), (0, LANES - v.shape[-1])), constant_values=fill)

    head_of_col = np.arange(SSM_INNER) // SSM_HEAD_DIM
    expand = (np.arange(LANES)[:, None] == head_of_col[None, :])
    return dict(
        g_mix=row(g_mix[0]),
        w_z=w[:, o_z:o_z + SSM_INNER].astype(BF16),
        w_xbc=w[:, o_xbc:o_xbc + CONV_CH].astype(BF16),
        w_dt=jnp.pad(w[:, o_dt:o_dt + SSM_HEADS], ((0, 0), (0, LANES - SSM_HEADS))).astype(BF16),
        w_hg=w[:, o_q:o_q + 4 * d].astype(BF16),
        w_ga=w[:, o_ga:o_ga + d].astype(BF16),
        w_gb=w[:, o_gb:o_gb + d].astype(BF16),
        conv_w=conv_w[0].astype(F32),
        conv_b=row(conv_b[0]),
        dt_bias=lane_pad(dt_bias[0]),
        a_log=lane_pad(a_log[0]),
        d_skip=row(jnp.repeat(d_skip[0], SSM_HEAD_DIM)),
        g_ssm=row(g_ssm[0]),
        w_ssm_branch=w_ssm_branch[0].astype(BF16),
        lb_logits=lb_logits.astype(F32),
        g_hgrn=row(g_hgrn[0]),
        w_hgrn_branch=w_hgrn_branch[0].astype(BF16),
        w_out=w_out[0].astype(BF16),
        g_mem=row(g_mem[0]),
        w_xk=w_xk[0].astype(BF16),
        w_xv=w_xv[0].astype(BF16),
        g_xattn=row(g_xattn[0]),
        w_xq=w_xq[0].astype(BF16),
        w_xo=w_xo[0].astype(BF16),
        g_moe=row(g_moe[0]),
        w_router=jnp.pad(w_router[0], ((0, 0), (0, LANES - N_EXPERTS))).astype(BF16),
        b_router=lane_pad(b_router[0], fill=-jnp.inf),
        w_gate_up=w_gate_up.astype(F32),
        b_gate_up=b_gate_up[0].reshape(N_EXPERTS, 1, 2 * D_EXPERT).astype(F32),
        w_down=w_down.astype(F32),
        b_down=b_down[0].reshape(N_EXPERTS, 1, d).astype(F32),
        expand=jnp.asarray(expand, BF16),
    )


def _group_to_experts(x, conv0, ssm0, hg0, mk, mv, p):
    b, t, d = x.shape
    ma, conv_n, ssm_n = _ssd_mixer(x, conv0, ssm0, p, min(SSD_TILE, t))
    mb, hg_n = _hgrn_mixer(x, hg0, p, min(HGRN_TILE, t))
    tm = min(POST_TILE, t)
    cnt0 = jnp.zeros((N_EXPERTS, LANES), F32)
    h2, xn3, meta, gates, counts = _post_mixer(x, ma, mb, mk, mv, cnt0, p, tm)
    n = b * t
    meta = meta.transpose(1, 0, 2).reshape(2 * TOP_K, n)
    gates = gates.transpose(1, 0, 2).reshape(2 * TOP_K, n)
    y_tok = _moe_outputs(xn3.reshape(n, d // 2), meta, counts, p)
    return (h2.reshape(n, d), gates, y_tok), (conv_n, ssm_n, hg_n)


def _group_combine(h2, gates, y_tok, g_final, shape):
    return _combine(h2, gates, y_tok, g_final, min(COMBINE_TILE, h2.shape[0])).reshape(shape)


def kernel(x_prompt, x_sample, mem_prompt, state_conv, state_ssm, state_hgrn, cache_mem_k, cache_mem_v, g_mix, w_in, conv_w, conv_b, dt_bias, a_log, d_skip, g_ssm, w_ssm_branch, lb_logits, g_hgrn, w_hgrn_branch, w_out, g_mem, w_xk, w_xv, g_xattn, w_xq, w_xo, g_moe, w_router, b_router, w_gate_up, b_gate_up, w_down, b_down, g_final):
    p = _prepare(g_mix, w_in, conv_w, conv_b, dt_bias, a_log, d_skip, g_ssm, w_ssm_branch, lb_logits, g_hgrn,
                 w_hgrn_branch, w_out, g_mem, w_xk, w_xv, g_xattn, w_xq, w_xo, g_moe, w_router, b_router,
                 w_gate_up, b_gate_up, w_down, b_down)
    g_fin = g_final.reshape(1, -1).astype(F32)
    bp = x_prompt.shape[0]
    bs = x_sample.shape[0]
    d = D_MODEL

    mk_p, mv_p = _memory_kv(mem_prompt, p)
    moe_p, (conv_p, ssm_p, hg_p) = _group_to_experts(
        x_prompt,
        jnp.zeros((1, bp, CONV_W - 1, CONV_CH), F32),
        jnp.zeros((1, bp, SSM_HEADS, SSM_HEAD_DIM, SSM_STATE), F32),
        jnp.zeros((1, bp, HG_HEADS, HG_K, HG_K), F32),
        mk_p, mv_p, p)
    moe_s, (conv_s, ssm_s, hg_s) = _group_to_experts(
        x_sample, state_conv, state_ssm, state_hgrn,
        cache_mem_k[0].reshape(bs, N_MEM, d), cache_mem_v[0].reshape(bs, N_MEM, d), p)
    y_p = _group_combine(*moe_p, g_fin, x_prompt.shape)
    y_s = _group_combine(*moe_s, g_fin, x_sample.shape)

    kv_shape = (1, bp, N_MEM, X_HEADS, X_HEAD_DIM)
    return (y_p, y_s,
            conv_p[None], ssm_p[None], hg_p[None], mk_p.reshape(kv_shape), mv_p.reshape(kv_shape),
            conv_s[None], ssm_s[None], hg_s[None])
```

```python
import jax
import jax.numpy as jnp
import numpy as np
from jax import lax
from jax.experimental import pallas as pl
from jax.experimental.pallas import tpu as pltpu
from jax.experimental.pallas import tpu_sc as plsc

F32 = jnp.float32
BF16 = jnp.bfloat16

D_MODEL = 1024
CHUNK = 64
EPS = 1e-6
SSM_INNER = 2 * D_MODEL
SSM_HEAD_DIM = 64
SSM_HEADS = SSM_INNER // SSM_HEAD_DIM
SSM_GROUPS = 4
SSM_STATE = 128
GROUP_COLS = SSM_INNER // SSM_GROUPS
CONV_W = 4
BC_COLS = SSM_GROUPS * SSM_STATE
CONV_CH = SSM_INNER + 2 * BC_COLS
HG_HEADS = 8
HG_K = D_MODEL // HG_HEADS
HG_SUB = CHUNK // 2
N_MEM = 256
X_HEADS = 4
X_HEAD_DIM = D_MODEL // X_HEADS
N_EXPERTS = 32
TOP_K = 4
D_EXPERT = D_MODEL
SWIGLU_LIMIT = 7.0
SWIGLU_ALPHA = 1.702

LANES = 128
CONV_PAD = 8
MOE_ROWS_MAX = 512
MOE_ROWS_MIN = 128
SSD_TILE = 256
HGRN_TILE = 256
POST_TILE = 512
COMBINE_TILE = 512
VMEM_LIMIT = 56 * 1024 * 1024
SC_CORES = 2
SC_SUBCORES = 16
SC_WORKERS = SC_CORES * SC_SUBCORES
SC_CHUNK = 64
SC_BUFFERS = 2


def _const_spec(shape):
    nd = len(shape)
    return pl.BlockSpec(shape, lambda *_: (0,) * nd, pipeline_mode=pl.Buffered(1))


def _dot(a, b):
    return jnp.dot(a, b, preferred_element_type=F32)


def _dot_nt(a, b):
    return lax.dot_general(a, b, (((1,), (1,)), ((), ())), preferred_element_type=F32)


def _dot_tn(a, b):
    return lax.dot_general(a, b, (((0,), (0,)), ((), ())), preferred_element_type=F32)


def _split3(a):
    hi = a.astype(BF16)
    r1 = a - hi.astype(F32)
    mid = r1.astype(BF16)
    lo = (r1 - mid.astype(F32)).astype(BF16)
    return hi, mid, lo


def _exact_dot_lhs01(sel, a):
    hi, mid, lo = _split3(a)
    return _dot(sel, hi) + _dot(sel, mid) + _dot(sel, lo)


def _rmsnorm(x, g):
    return x * lax.rsqrt(jnp.mean(x * x, axis=-1, keepdims=True) + EPS) * g


def _sigmoid(x):
    return jax.nn.sigmoid(x)


def _silu(x):
    return x * jax.nn.sigmoid(x)


def _softplus(x):
    return jnp.maximum(x, 0.0) + jnp.log1p(jnp.exp(-jnp.abs(x)))


def _pack_bf16_pairs(x):
    c = x.shape[1] // 2
    hi = lax.bitcast_convert_type(x[:, :c].astype(BF16).astype(F32), jnp.int32)
    lo = lax.bitcast_convert_type(x[:, c:].astype(BF16).astype(F32), jnp.int32)
    return hi | lax.shift_right_logical(lo, 16)


def _unpack_bf16_pairs(u):
    hi = lax.bitcast_convert_type(u & jnp.int32(-65536), F32)
    lo = lax.bitcast_convert_type(lax.shift_left(u, 16), F32)
    return hi, lo


def _ssd_kernel(x_ref, conv0_ref, ssm0_ref, g_ref, wxbc_ref, wz_ref, wdt_ref, wga_ref, convw_ref,
                convb_ref, dtb_ref, alog_ref, dskip_ref, gssm_ref, wbr_ref, tri_ref, expand_ref,
                ma_ref, convn_ref, ssmn_ref,
                xpad, xact, xdt, ysc, st):
    tt = x_ref.shape[0]
    ti = pl.program_id(1)
    nt = pl.num_programs(1)

    @pl.when(ti == 0)
    def _():
        xpad[CONV_PAD - (CONV_W - 1):CONV_PAD, :] = conv0_ref[...]
        st[...] = ssm0_ref[...].reshape(SSM_INNER, SSM_STATE).T

    xn = _rmsnorm(x_ref[...], g_ref[...]).astype(BF16)
    xpad[CONV_PAD:CONV_PAD + tt, :] = _dot(xn, wxbc_ref[...])

    conv = convb_ref[...] + xpad[CONV_PAD:CONV_PAD + tt, :] * convw_ref[CONV_W - 1:CONV_W, :]
    for k in range(1, CONV_W):
        conv = conv + xpad[CONV_PAD - k:CONV_PAD - k + tt, :] * convw_ref[CONV_W - 1 - k:CONV_W - k, :]
    xact[...] = _silu(conv)
    tail = xpad[CONV_PAD + tt - (CONV_W - 1):CONV_PAD + tt, :]
    xpad[CONV_PAD - (CONV_W - 1):CONV_PAD, :] = tail

    dt = _softplus(_dot(xn, wdt_ref[...]) + dtb_ref[...])
    acum_all = _exact_dot_lhs01(tri_ref[...], dt * -jnp.exp(alog_ref[...]))
    dt_hi = dt.astype(BF16)
    dt_lo = (dt - dt_hi.astype(F32)).astype(BF16)
    dt_x = _dot(dt_hi, expand_ref[...]) + _dot(dt_lo, expand_ref[...])
    xdt[...] = xact[:, :SSM_INNER] * dt_x

    lane = lax.broadcasted_iota(jnp.int32, (CHUNK, LANES), 1)
    row = lax.broadcasted_iota(jnp.int32, (CHUNK, LANES), 0)
    causal2 = row >= (lane % CHUNK)
    diag2 = row == (lane % CHUNK)
    left = lane < CHUNK
    pair_of_lane = lane // CHUNK

    for c in range(tt // CHUNK):
        rows = slice(c * CHUNK, (c + 1) * CHUNK)
        acum = acum_all[rows, :]
        for g in range(SSM_GROUPS):
            gs = slice(g * GROUP_COLS, (g + 1) * GROUP_COLS)
            b_g = xact[rows, SSM_INNER + g * SSM_STATE:SSM_INNER + (g + 1) * SSM_STATE].astype(BF16)
            c_g = xact[rows, SSM_INNER + BC_COLS + g * SSM_STATE:
                       SSM_INNER + BC_COLS + (g + 1) * SSM_STATE].astype(BF16)
            cb2 = _dot_nt(c_g, jnp.concatenate([b_g, b_g], axis=0))
            st_g = st[:, gs]
            y_inter = _dot(c_g, st_g.astype(BF16))
            xw, decay = [], []
            for j in range(GROUP_COLS // LANES):
                pair = g * (GROUP_COLS // LANES) + j
                ps = slice(pair * LANES, (pair + 1) * LANES)
                a_col = jnp.take_along_axis(acum, pair_of_lane + 2 * pair, axis=1)
                a_row = jnp.sum(jnp.where(diag2, a_col, 0.0), axis=0, keepdims=True)
                a_last = a_col[CHUNK - 1:CHUNK, :]
                dec = jnp.where(causal2, jnp.exp(jnp.minimum(a_col - a_row, 0.0)), 0.0)
                wts = (cb2 * dec).astype(BF16)
                xp = xdt[rows, ps]
                xbd = jnp.concatenate([jnp.where(left, xp, 0.0), jnp.where(left, 0.0, xp)],
                                      axis=0).astype(BF16)
                ysc[rows, ps] = _dot(wts, xbd) + y_inter[:, j * LANES:(j + 1) * LANES] * jnp.exp(a_col)
                xw.append((jnp.exp(a_last - a_col) * xp).astype(BF16))
                decay.append(jnp.exp(a_last))
            st[:, gs] = jnp.concatenate(decay, axis=1) * st_g + _dot_tn(b_g, jnp.concatenate(xw, axis=1))

    xs = xact[:, :SSM_INNER]
    y = ysc[...] + dskip_ref[...] * xs
    yz = y * _silu(_dot(xn, wz_ref[...]))
    parts = []
    for g in range(SSM_GROUPS):
        blk = yz[:, g * GROUP_COLS:(g + 1) * GROUP_COLS]
        parts.append(blk * lax.rsqrt(jnp.mean(blk * blk, axis=-1, keepdims=True) + EPS))
    yn = (jnp.concatenate(parts, axis=1) * gssm_ref[...]).astype(BF16)
    ya = _dot(yn, wbr_ref[...])
    ma_ref[...] = _sigmoid(_dot(xn, wga_ref[...])) * ya

    @pl.when(ti == nt - 1)
    def _():
        convn_ref[...] = tail
        ssmn_ref[...] = st[...].T.reshape(SSM_HEADS, SSM_HEAD_DIM, SSM_STATE)


def _chunk_tri(tt):
    return jnp.asarray(np.kron(np.eye(tt // CHUNK), np.tril(np.ones((CHUNK, CHUNK)))), BF16)


def _ssd_mixer(x, conv0, ssm0, p, tt):
    b, t, d = x.shape
    grid = (b, t // tt)
    row_spec = pl.BlockSpec((None, tt, d), lambda i, j: (i, j, 0))
    consts = [p['g_mix'], p['w_xbc'], p['w_z'], p['w_dt'], p['w_ga'], p['conv_w'], p['conv_b'], p['dt_bias'],
              p['a_log'], p['d_skip'], p['g_ssm'], p['w_ssm_branch'], _chunk_tri(tt), p['expand']]
    return pl.pallas_call(
        _ssd_kernel,
        grid=grid,
        in_specs=[row_spec,
                  pl.BlockSpec((None, None, CONV_W - 1, CONV_CH), lambda i, j: (0, i, 0, 0)),
                  pl.BlockSpec((None, None, SSM_HEADS, SSM_HEAD_DIM, SSM_STATE), lambda i, j: (0, i, 0, 0, 0))]
                 + [_const_spec(c.shape) for c in consts],
        out_specs=[row_spec,
                   pl.BlockSpec((None, CONV_W - 1, CONV_CH), lambda i, j: (i, 0, 0)),
                   pl.BlockSpec((None, SSM_HEADS, SSM_HEAD_DIM, SSM_STATE), lambda i, j: (i, 0, 0, 0))],
        out_shape=[jax.ShapeDtypeStruct((b, t, d), F32),
                   jax.ShapeDtypeStruct((b, CONV_W - 1, CONV_CH), F32),
                   jax.ShapeDtypeStruct((b, SSM_HEADS, SSM_HEAD_DIM, SSM_STATE), F32)],
        scratch_shapes=[pltpu.VMEM((CONV_PAD + tt, CONV_CH), F32),
                        pltpu.VMEM((tt, CONV_CH), F32),
                        pltpu.VMEM((tt, SSM_INNER), F32),
                        pltpu.VMEM((tt, SSM_INNER), F32),
                        pltpu.VMEM((SSM_STATE, SSM_INNER), F32)],
        compiler_params=pltpu.CompilerParams(dimension_semantics=("arbitrary", "arbitrary"),
                                             vmem_limit_bytes=VMEM_LIMIT),
        name="ssd_mixer",
    )(x, conv0, ssm0, *consts)


def _hgrn_kernel(x_ref, hg0_ref, g_ref, wh_ref, wgb_ref, lbl_ref, ghg_ref, wbr_ref, tri_ref,
                 mb_ref, hgn_ref,
                 osc, st):
    tt = x_ref.shape[0]
    n_chunks = tt // CHUNK
    ti = pl.program_id(1)
    nt = pl.num_programs(1)

    @pl.when(ti == 0)
    def _():
        for h in range(HG_HEADS):
            st[h] = hg0_ref[h].T

    xn = _rmsnorm(x_ref[...], g_ref[...]).astype(BF16)
    proj = _dot(xn, wh_ref[...])
    l0 = lbl_ref[0:1, :]
    l1 = lbl_ref[1:2, :]
    lmax = jnp.maximum(l0, l1)
    e0 = jnp.exp(l0 - lmax)
    lb = e0 / (e0 + jnp.exp(l1 - lmax))
    fr = proj[:, D_MODEL:2 * D_MODEL]
    kk = (1.0 - lb) * _sigmoid(-fr)
    v = _silu(proj[:, 2 * D_MODEL:3 * D_MODEL]).astype(BF16)
    gc = _exact_dot_lhs01(tri_ref[...], jnp.log(lb + (1.0 - lb) * _sigmoid(fr)))
    q = proj[:, :D_MODEL]
    qh = (q * jnp.exp(gc)).astype(BF16)
    mid = []
    for m in range(tt // HG_SUB):
        r = m * HG_SUB + HG_SUB // 2 - 1
        mid.append(jnp.broadcast_to(gc[r:r + 1, :], (HG_SUB, D_MODEL)))
    d_mid = gc - jnp.concatenate(mid, axis=0)
    q_sub = (q * jnp.exp(d_mid)).astype(BF16)
    k_sub = (kk * jnp.exp(-d_mid)).astype(BF16)
    q_far, k_far, v_first = [], [], []
    for c in range(n_chunks):
        first = slice(c * CHUNK, c * CHUNK + HG_SUB)
        second = slice(c * CHUNK + HG_SUB, (c + 1) * CHUNK)
        g_half = gc[c * CHUNK + HG_SUB - 1:c * CHUNK + HG_SUB, :]
        q_far.append(q[second, :] * jnp.exp(gc[second, :] - g_half))
        k_far.append(kk[first, :] * jnp.exp(g_half - gc[first, :]))
        v_first.append(v[first, :])
    q_far = jnp.concatenate(q_far, axis=0).astype(BF16)
    k_far = jnp.concatenate(k_far, axis=0).astype(BF16)
    v_first = jnp.concatenate(v_first, axis=0)
    kt, decay = [], []
    for c in range(n_chunks):
        rows = slice(c * CHUNK, (c + 1) * CHUNK)
        glast = gc[(c + 1) * CHUNK - 1:(c + 1) * CHUNK, :]
        kt.append((kk[rows, :] * jnp.exp(glast - gc[rows, :])).astype(BF16))
        decay.append(jnp.exp(glast))

    row = lax.broadcasted_iota(jnp.int32, (tt, tt), 0)
    col = lax.broadcasted_iota(jnp.int32, (tt, tt), 1)
    near = jnp.logical_and(row >= col, row // HG_SUB == col // HG_SUB)
    same_chunk = (row // HG_SUB == col // HG_SUB)[:tt // 2, :tt // 2]
    no_far = jnp.zeros((HG_SUB, HG_K), F32)

    for h in range(HG_HEADS):
        hs = slice(h * HG_K, (h + 1) * HG_K)
        att = jnp.where(near, _dot_nt(q_sub[:, hs], k_sub[:, hs]), 0.0).astype(BF16)
        att_far = jnp.where(same_chunk, _dot_nt(q_far[:, hs], k_far[:, hs]), 0.0).astype(BF16)
        o_far = _dot(att_far, v_first[:, hs])
        far_rows = []
        for c in range(n_chunks):
            far_rows += [no_far, o_far[c * HG_SUB:(c + 1) * HG_SUB, :]]
        o_intra = _dot(att, v[:, hs]) + jnp.concatenate(far_rows, axis=0)
        s = st[h]
        o_inter = []
        for c in range(n_chunks):
            rows = slice(c * CHUNK, (c + 1) * CHUNK)
            o_inter.append(_dot_nt(qh[rows, hs], s.astype(BF16)))
            s = decay[c][:, hs] * s + _dot_tn(v[rows, hs], kt[c][:, hs])
        st[h] = s
        osc[:, hs] = o_intra + jnp.concatenate(o_inter, axis=0)

    parts = []
    for h in range(HG_HEADS):
        blk = osc[:, h * HG_K:(h + 1) * HG_K]
        parts.append(blk * lax.rsqrt(jnp.mean(blk * blk, axis=-1, keepdims=True) + EPS))
    on = (jnp.concatenate(parts, axis=1) * ghg_ref[...] * _silu(proj[:, 3 * D_MODEL:])).astype(BF16)
    yb = _dot(on, wbr_ref[...])
    mb_ref[...] = _sigmoid(_dot(xn, wgb_ref[...])) * yb

    @pl.when(ti == nt - 1)
    def _():
        for h in range(HG_HEADS):
            hgn_ref[h] = st[h].T


def _hgrn_mixer(x, hg0, p, tt):
    b, t, d = x.shape
    grid = (b, t // tt)
    row_spec = pl.BlockSpec((None, tt, d), lambda i, j: (i, j, 0))
    st_spec = pl.BlockSpec((None, HG_HEADS, HG_K, HG_K), lambda i, j: (i, 0, 0, 0))
    consts = [p['g_mix'], p['w_hg'], p['w_gb'], p['lb_logits'], p['g_hgrn'], p['w_hgrn_branch'], _chunk_tri(tt)]
    return pl.pallas_call(
        _hgrn_kernel,
        grid=grid,
        in_specs=[row_spec, pl.BlockSpec((None, None, HG_HEADS, HG_K, HG_K), lambda i, j: (0, i, 0, 0, 0))]
                 + [_const_spec(c.shape) for c in consts],
        out_specs=[row_spec, st_spec],
        out_shape=[jax.ShapeDtypeStruct((b, t, d), F32),
                   jax.ShapeDtypeStruct((b, HG_HEADS, HG_K, HG_K), F32)],
        scratch_shapes=[pltpu.VMEM((tt, d), F32), pltpu.VMEM((HG_HEADS, HG_K, HG_K), F32)],
        compiler_params=pltpu.CompilerParams(dimension_semantics=("arbitrary", "arbitrary"),
                                             vmem_limit_bytes=VMEM_LIMIT),
        name="hgrn_mixer",
    )(x, hg0, *consts)


def _memkv_kernel(m_ref, g_ref, wk_ref, wv_ref, k_ref, v_ref):
    mn = _rmsnorm(m_ref[...], g_ref[...]).astype(BF16)
    k_ref[...] = _dot(mn, wk_ref[...])
    v_ref[...] = _dot(mn, wv_ref[...])


def _memory_kv(mem, p):
    b, n, d = mem.shape
    spec = pl.BlockSpec((None, n, d), lambda i: (i, 0, 0))
    consts = [p['g_mem'], p['w_xk'], p['w_xv']]
    return pl.pallas_call(
        _memkv_kernel,
        grid=(b,),
        in_specs=[spec] + [_const_spec(c.shape) for c in consts],
        out_specs=[spec, spec],
        out_shape=[jax.ShapeDtypeStruct((b, n, d), F32)] * 2,
        compiler_params=pltpu.CompilerParams(dimension_semantics=("arbitrary",),
                                             vmem_limit_bytes=VMEM_LIMIT),
        name="memory_kv",
    )(mem, *consts)


def _post_kernel(x_ref, ma_ref, mb_ref, mk_ref, mv_ref, cnt0_ref, wo_ref, gx_ref, wq_ref, wxo_ref,
                 gmoe_ref, wr_ref, br_ref, triu_ref,
                 h_ref, xn_ref, meta_ref, gate_ref, cnt_ref,
                 base):
    tm = x_ref.shape[0]
    first = jnp.logical_and(pl.program_id(0) == 0, pl.program_id(1) == 0)

    @pl.when(first)
    def _():
        base[...] = cnt0_ref[...]

    m = (ma_ref[...] + mb_ref[...]).astype(BF16)
    h1 = x_ref[...] + _dot(m, wo_ref[...])

    hn = _rmsnorm(h1, gx_ref[...]).astype(BF16)
    q = _dot(hn, wq_ref[...])
    heads = []
    for hh in range(X_HEADS):
        hs = slice(hh * X_HEAD_DIM, (hh + 1) * X_HEAD_DIM)
        s = _dot_nt(q[:, hs].astype(BF16), mk_ref[:, hs].astype(BF16)) * (X_HEAD_DIM ** -0.5)
        s = s - jnp.max(s, axis=-1, keepdims=True)
        e = jnp.exp(s)
        pr = e / jnp.sum(e, axis=-1, keepdims=True)
        heads.append(_dot(pr.astype(BF16), mv_ref[:, hs].astype(BF16)))
    o = jnp.concatenate(heads, axis=1).astype(BF16)
    h2 = h1 + _dot(o, wxo_ref[...])
    h_ref[...] = h2

    xn3 = _rmsnorm(h2, gmoe_ref[...])
    xn_ref[...] = _pack_bf16_pairs(xn3)
    logits = _dot(xn3.astype(BF16), wr_ref[...]) + br_ref[...]

    run = logits.T[:N_EXPERTS, :]
    eid = lax.broadcasted_iota(jnp.int32, (N_EXPERTS, tm), 0).astype(F32)
    vals, ids, hots = [], [], []
    for _ in range(TOP_K):
        mx = jnp.max(run, axis=0, keepdims=True)
        idx = jnp.min(jnp.where(run == mx, eid, float(N_EXPERTS)), axis=0, keepdims=True)
        hot = eid == idx
        run = jnp.where(hot, -jnp.inf, run)
        vals.append(mx)
        ids.append(idx)
        hots.append(hot)
    es = [jnp.exp(v - vals[0]) for v in vals]
    den = es[0] + es[1] + es[2] + es[3]

    tot = jnp.zeros((N_EXPERTS, tm), F32)
    for hot in hots:
        tot = tot + hot.astype(F32)
    before = base[:, :1] + _dot(tot.astype(BF16), triu_ref[...])
    base[...] = base[...] + jnp.sum(tot, axis=1, keepdims=True)
    cnt_ref[...] = base[...]

    ranks = [jnp.sum(jnp.where(hot, before, 0.0), axis=0, keepdims=True) for hot in hots]
    meta_ref[...] = jnp.concatenate(ids + ranks, axis=0).astype(jnp.int32)
    gate_ref[...] = jnp.concatenate([e / den for e in es] + [jnp.zeros_like(den)] * TOP_K, axis=0)


def _post_mixer(x, ma, mb, mk, mv, cnt0, p, tm):
    b, t, d = x.shape
    grid = (b, t // tm)
    row_spec = pl.BlockSpec((None, tm, d), lambda i, j: (i, j, 0))
    small_spec = pl.BlockSpec((None, 2 * TOP_K, tm), lambda i, j: (i, 0, j))
    mem_spec = pl.BlockSpec((None, N_MEM, d), lambda i, j: (i, 0, 0))
    cnt_spec = pl.BlockSpec((N_EXPERTS, LANES), lambda i, j: (0, 0))
    triu = jnp.triu(jnp.ones((tm, tm), BF16), 1)
    consts = [p['w_out'], p['g_xattn'], p['w_xq'], p['w_xo'], p['g_moe'], p['w_router'], p['b_router'], triu]
    return pl.pallas_call(
        _post_kernel,
        grid=grid,
        in_specs=[row_spec, row_spec, row_spec, mem_spec, mem_spec, cnt_spec]
                 + [_const_spec(c.shape) for c in consts],
        out_specs=[row_spec, pl.BlockSpec((None, tm, d // 2), lambda i, j: (i, j, 0)), small_spec, small_spec,
                   cnt_spec],
        out_shape=[jax.ShapeDtypeStruct((b, t, d), F32),
                   jax.ShapeDtypeStruct((b, t, d // 2), jnp.int32),
                   jax.ShapeDtypeStruct((b, 2 * TOP_K, t), jnp.int32),
                   jax.ShapeDtypeStruct((b, 2 * TOP_K, t), F32),
                   jax.ShapeDtypeStruct((N_EXPERTS, LANES), F32)],
        scratch_shapes=[pltpu.VMEM((N_EXPERTS, LANES), F32)],
        compiler_params=pltpu.CompilerParams(dimension_semantics=("arbitrary", "arbitrary"),
                                             vmem_limit_bytes=VMEM_LIMIT),
        name="post_mixer",
    )(x, ma, mb, mk, mv, cnt0, *consts)


def _sc_chunk(rows_per_worker):
    for chunk in range(SC_CHUNK, 0, -8):
        if rows_per_worker % (SC_BUFFERS * chunk) == 0:
            return chunk
    raise ValueError(f"no SparseCore chunk size for {rows_per_worker} rows per worker")


def _sc_mesh():
    return plsc.VectorSubcoreMesh(core_axis_name="c", subcore_axis_name="s")


def _sc_worker():
    return lax.axis_index("s") * SC_CORES + lax.axis_index("c")


def _dispatch(xn, dest, n_slots):
    n, d = xn.shape
    per_w = n // SC_WORKERS
    chunk = _sc_chunk(per_w)
    n_chunks = per_w // chunk
    idx = dest.reshape(TOP_K, n // chunk, chunk).transpose(1, 0, 2)

    def body(x_hbm, idx_hbm, out_hbm, idx_v, rows_v, rsem, wsem):
        wid = _sc_worker()

        def read(i, b):
            blk = wid * n_chunks + i
            return pltpu.make_async_copy(x_hbm.at[pl.ds(blk * chunk, chunk)], rows_v.at[b], rsem.at[b])

        def read_start(i, b):
            pltpu.sync_copy(idx_hbm.at[wid * n_chunks + i], idx_v.at[b])
            read(i, b).start()

        def write(b, j):
            return pltpu.make_async_copy(rows_v.at[b], out_hbm.at[idx_v.at[b, j]], wsem.at[b])

        for b in range(SC_BUFFERS):
            read_start(b, b)

        @pl.loop(0, n_chunks, step=SC_BUFFERS)
        def _(i0):
            for b in range(SC_BUFFERS):
                i = i0 + b
                read(i, b).wait()
                for j in range(TOP_K):
                    write(b, j).start()
                for j in range(TOP_K):
                    write(b, j).wait()

                @pl.when(i + SC_BUFFERS < n_chunks)
                def _():
                    read_start(i + SC_BUFFERS, b)

    return pl.kernel(
        body, mesh=_sc_mesh(),
        out_type=jax.ShapeDtypeStruct((n_slots, d), xn.dtype),
        scratch_types=[pltpu.VMEM((SC_BUFFERS, TOP_K, chunk), jnp.int32),
                       pltpu.VMEM((SC_BUFFERS, chunk, d), xn.dtype),
                       pltpu.SemaphoreType.DMA((SC_BUFFERS,)),
                       pltpu.SemaphoreType.DMA((SC_BUFFERS,))],
        name="moe_dispatch",
    )(xn, idx)


def _gather_rows(table, idx):
    n_out = idx.shape[0]
    d = table.shape[1]
    per_w = n_out // SC_WORKERS
    chunk = _sc_chunk(per_w)
    n_chunks = per_w // chunk

    def body(table_hbm, idx_hbm, out_hbm, idx_v, rows_v, gsem, wsem):
        base = _sc_worker() * per_w

        def gather(b):
            return pltpu.make_async_copy(table_hbm.at[idx_v.at[b]], rows_v.at[b], gsem.at[b])

        def gather_start(i, b):
            pltpu.sync_copy(idx_hbm.at[pl.ds(base + i * chunk, chunk)], idx_v.at[b])
            gather(b).start()

        def write(i, b):
            return pltpu.make_async_copy(rows_v.at[b], out_hbm.at[pl.ds(base + i * chunk, chunk)], wsem.at[b])

        for b in range(SC_BUFFERS):
            gather_start(b, b)

        @pl.loop(0, n_chunks, step=SC_BUFFERS)
        def _(i0):
            for b in range(SC_BUFFERS):
                i = i0 + b
                gather(b).wait()
                write(i, b).start()
                write(i, b).wait()

                @pl.when(i + SC_BUFFERS < n_chunks)
                def _():
                    gather_start(i + SC_BUFFERS, b)

    return pl.kernel(
        body, mesh=_sc_mesh(),
        out_type=jax.ShapeDtypeStruct((n_out, d), table.dtype),
        scratch_types=[pltpu.VMEM((SC_BUFFERS, chunk), jnp.int32),
                       pltpu.VMEM((SC_BUFFERS, chunk, d), table.dtype),
                       pltpu.SemaphoreType.DMA((SC_BUFFERS,)),
                       pltpu.SemaphoreType.DMA((SC_BUFFERS,))],
        name="moe_gather",
    )(table, idx)


def _moe_kernel(be_ref, valid_ref, x_ref, wgu_ref, bgu_ref, wd_ref, bd_ref, y_ref, wgu_bf, wd_bf):
    i = pl.program_id(0)
    valid = valid_ref[i]

    @pl.when(jnp.logical_or(i == 0, be_ref[i] != be_ref[jnp.maximum(i - 1, 0)]))
    def _():
        wgu_bf[...] = wgu_ref[...].astype(BF16)
        wd_bf[...] = wd_ref[...].astype(BF16)

    @pl.when(valid > 0)
    def _():
        row = lax.broadcasted_iota(jnp.int32, x_ref.shape, 0)
        x_hi, x_lo = _unpack_bf16_pairs(jnp.where(row < valid, x_ref[...], 0))
        half = x_ref.shape[1]
        gu = (_dot(x_hi.astype(BF16), wgu_bf[:half, :]) + _dot(x_lo.astype(BF16), wgu_bf[half:, :])
              + bgu_ref[...])
        gate = jnp.minimum(gu[:, :D_EXPERT], SWIGLU_LIMIT)
        up = jnp.clip(gu[:, D_EXPERT:], -SWIGLU_LIMIT, SWIGLU_LIMIT)
        hmid = ((up + 1.0) * gate * _sigmoid(SWIGLU_ALPHA * gate)).astype(BF16)
        y_ref[...] = _pack_bf16_pairs(_dot(hmid, wd_bf[...]) + bd_ref[...])


def _moe_experts(slots, block_e, block_valid, p):
    n_slots, dh = slots.shape
    d = 2 * dh
    n_blocks = block_e.shape[0]
    block_rows = n_slots // n_blocks
    grid_spec = pltpu.PrefetchScalarGridSpec(
        num_scalar_prefetch=2,
        grid=(n_blocks,),
        in_specs=[pl.BlockSpec((block_rows, dh), lambda i, be, bv: (i, 0)),
                  pl.BlockSpec((None, None, d, 2 * D_EXPERT), lambda i, be, bv: (0, be[i], 0, 0)),
                  pl.BlockSpec((None, 1, 2 * D_EXPERT), lambda i, be, bv: (be[i], 0, 0)),
                  pl.BlockSpec((None, None, D_EXPERT, d), lambda i, be, bv: (0, be[i], 0, 0)),
                  pl.BlockSpec((None, 1, d), lambda i, be, bv: (be[i], 0, 0))],
        out_specs=pl.BlockSpec((block_rows, dh), lambda i, be, bv: (i, 0)),
        scratch_shapes=[pltpu.VMEM((d, 2 * D_EXPERT), BF16), pltpu.VMEM((D_EXPERT, d), BF16)],
    )
    return pl.pallas_call(
        _moe_kernel,
        grid_spec=grid_spec,
        out_shape=jax.ShapeDtypeStruct((n_slots, dh), jnp.int32),
        compiler_params=pltpu.CompilerParams(dimension_semantics=("arbitrary",),
                                             vmem_limit_bytes=VMEM_LIMIT),
        name="moe_experts",
    )(block_e, block_valid, slots, p['w_gate_up'], p['b_gate_up'], p['w_down'], p['b_down'])


def _combine_kernel(h_ref, gate_ref, y_ref, gfin_ref, out_ref):
    half = y_ref.shape[2]
    acc_hi = h_ref[:, :half]
    acc_lo = h_ref[:, half:]
    gates = gate_ref[...].T
    for j in range(TOP_K):
        y_hi, y_lo = _unpack_bf16_pairs(y_ref[j])
        acc_hi = acc_hi + gates[:, j:j + 1] * y_hi
        acc_lo = acc_lo + gates[:, j:j + 1] * y_lo
    out_ref[...] = _rmsnorm(jnp.concatenate([acc_hi, acc_lo], axis=1), gfin_ref[...])


def _combine(h, gates, y_tok, first_token, g_final, tm):
    n, d = h.shape
    first_block = first_token // tm
    assert first_token % tm == 0
    return pl.pallas_call(
        _combine_kernel,
        grid=(n // tm,),
        in_specs=[pl.BlockSpec((tm, d), lambda i: (i, 0)),
                  pl.BlockSpec((2 * TOP_K, tm), lambda i: (0, i)),
                  pl.BlockSpec((TOP_K, tm, d // 2), lambda i: (0, i + first_block, 0)),
                  _const_spec(g_final.shape)],
        out_specs=pl.BlockSpec((tm, d), lambda i: (i, 0)),
        out_shape=jax.ShapeDtypeStruct((n, d), F32),
        compiler_params=pltpu.CompilerParams(dimension_semantics=("arbitrary",),
                                             vmem_limit_bytes=VMEM_LIMIT),
        name="moe_combine",
    )(h, gates, y_tok, g_final)


def _moe_outputs(xn, meta, counts, p):
    n, d = xn.shape
    n_rows = n * TOP_K
    block_rows = max(MOE_ROWS_MIN, min(MOE_ROWS_MAX, n_rows // N_EXPERTS))
    n_blocks = n_rows // block_rows + N_EXPERTS
    n_slots = n_blocks * block_rows
    cnt = counts[:, 0].astype(jnp.int32)
    padded = (cnt + block_rows - 1) // block_rows * block_rows
    pad_end = jnp.cumsum(padded)
    pad_start = pad_end - padded
    block_row0 = jnp.arange(n_blocks, dtype=jnp.int32) * block_rows
    block_e = jnp.minimum(jnp.sum((pad_end[None, :] <= block_row0[:, None]).astype(jnp.int32), axis=1),
                          N_EXPERTS - 1)
    used_end = jnp.sum(jnp.where(block_e[:, None] == jnp.arange(N_EXPERTS)[None, :], (pad_start + cnt)[None, :], 0),
                       axis=1)
    block_valid = jnp.clip(used_end - block_row0, 0, block_rows)
    expert = meta[:TOP_K]
    start = jnp.zeros_like(expert)
    for e in range(N_EXPERTS):
        start = jnp.where(expert == e, pad_start[e], start)
    dest = (start + meta[TOP_K:]).astype(jnp.int32)
    slots = _dispatch(xn, dest, n_slots)
    y_slots = _moe_experts(slots, block_e, block_valid, p)
    return _gather_rows(y_slots, dest.reshape(-1)).reshape(TOP_K, n, d)


def _prepare(g_mix, w_in, conv_w, conv_b, dt_bias, a_log, d_skip, g_ssm, w_ssm_branch, lb_logits, g_hgrn,
             w_hgrn_branch, w_out, g_mem, w_xk, w_xv, g_xattn, w_xq, w_xo, g_moe, w_router, b_router,
             w_gate_up, b_gate_up, w_down, b_down):
    d = D_MODEL
    w = w_in[0]
    o_z, o_xbc = 0, SSM_INNER
    o_dt = o_xbc + CONV_CH
    o_q = o_dt + SSM_HEADS
    o_ga = o_q + 4 * d
    o_gb = o_ga + d

    def row(v):
        return v.reshape(1, -1).astype(F32)

    def lane_pad(v, fill=0.0):
        return jnp.pad(v.reshape(1, -1).astype(F32), ((0, 0), (0, LANES - v.shape[-1])), constant_values=fill)

    head_of_col = np.arange(SSM_INNER) // SSM_HEAD_DIM
    expand = (np.arange(LANES)[:, None] == head_of_col[None, :])
    return dict(
        g_mix=row(g_mix[0]),
        w_z=w[:, o_z:o_z + SSM_INNER].astype(BF16),
        w_xbc=w[:, o_xbc:o_xbc + CONV_CH].astype(BF16),
        w_dt=jnp.pad(w[:, o_dt:o_dt + SSM_HEADS], ((0, 0), (0, LANES - SSM_HEADS))).astype(BF16),
        w_hg=w[:, o_q:o_q + 4 * d].astype(BF16),
        w_ga=w[:, o_ga:o_ga + d].astype(BF16),
        w_gb=w[:, o_gb:o_gb + d].astype(BF16),
        conv_w=conv_w[0].astype(F32),
        conv_b=row(conv_b[0]),
        dt_bias=lane_pad(dt_bias[0]),
        a_log=lane_pad(a_log[0]),
        d_skip=row(jnp.repeat(d_skip[0], SSM_HEAD_DIM)),
        g_ssm=row(g_ssm[0]),
        w_ssm_branch=w_ssm_branch[0].astype(BF16),
        lb_logits=lb_logits.astype(F32),
        g_hgrn=row(g_hgrn[0]),
        w_hgrn_branch=w_hgrn_branch[0].astype(BF16),
        w_out=w_out[0].astype(BF16),
        g_mem=row(g_mem[0]),
        w_xk=w_xk[0].astype(BF16),
        w_xv=w_xv[0].astype(BF16),
        g_xattn=row(g_xattn[0]),
        w_xq=w_xq[0].astype(BF16),
        w_xo=w_xo[0].astype(BF16),
        g_moe=row(g_moe[0]),
        w_router=jnp.pad(w_router[0], ((0, 0), (0, LANES - N_EXPERTS))).astype(BF16),
        b_router=lane_pad(b_router[0], fill=-jnp.inf),
        w_gate_up=w_gate_up.astype(F32),
        b_gate_up=b_gate_up[0].reshape(N_EXPERTS, 1, 2 * D_EXPERT).astype(F32),
        w_down=w_down.astype(F32),
        b_down=b_down[0].reshape(N_EXPERTS, 1, d).astype(F32),
        expand=jnp.asarray(expand, BF16),
    )


def _group_to_router(x, conv0, ssm0, hg0, mk, mv, cnt0, p):
    b, t, d = x.shape
    ma, conv_n, ssm_n = _ssd_mixer(x, conv0, ssm0, p, min(SSD_TILE, t))
    mb, hg_n = _hgrn_mixer(x, hg0, p, min(HGRN_TILE, t))
    h2, xn3, meta, gates, counts = _post_mixer(x, ma, mb, mk, mv, cnt0, p, min(POST_TILE, t))
    n = b * t
    meta = meta.transpose(1, 0, 2).reshape(2 * TOP_K, n)
    gates = gates.transpose(1, 0, 2).reshape(2 * TOP_K, n)
    return (h2.reshape(n, d), xn3.reshape(n, d // 2), meta, gates), counts, (conv_n, ssm_n, hg_n)


def kernel(x_prompt, x_sample, mem_prompt, state_conv, state_ssm, state_hgrn, cache_mem_k, cache_mem_v, g_mix, w_in, conv_w, conv_b, dt_bias, a_log, d_skip, g_ssm, w_ssm_branch, lb_logits, g_hgrn, w_hgrn_branch, w_out, g_mem, w_xk, w_xv, g_xattn, w_xq, w_xo, g_moe, w_router, b_router, w_gate_up, b_gate_up, w_down, b_down, g_final):
    p = _prepare(g_mix, w_in, conv_w, conv_b, dt_bias, a_log, d_skip, g_ssm, w_ssm_branch, lb_logits, g_hgrn,
                 w_hgrn_branch, w_out, g_mem, w_xk, w_xv, g_xattn, w_xq, w_xo, g_moe, w_router, b_router,
                 w_gate_up, b_gate_up, w_down, b_down)
    g_fin = g_final.reshape(1, -1).astype(F32)
    bp = x_prompt.shape[0]
    bs = x_sample.shape[0]
    d = D_MODEL

    mk_p, mv_p = _memory_kv(mem_prompt, p)
    (h_p, xn_p, meta_p, gates_p), counts, (conv_p, ssm_p, hg_p) = _group_to_router(
        x_prompt,
        jnp.zeros((1, bp, CONV_W - 1, CONV_CH), F32),
        jnp.zeros((1, bp, SSM_HEADS, SSM_HEAD_DIM, SSM_STATE), F32),
        jnp.zeros((1, bp, HG_HEADS, HG_K, HG_K), F32),
        mk_p, mv_p, jnp.zeros((N_EXPERTS, LANES), F32), p)
    (h_s, xn_s, meta_s, gates_s), counts, (conv_s, ssm_s, hg_s) = _group_to_router(
        x_sample, state_conv, state_ssm, state_hgrn,
        cache_mem_k[0].reshape(bs, N_MEM, d), cache_mem_v[0].reshape(bs, N_MEM, d), counts, p)
    y_tok = _moe_outputs(jnp.concatenate([xn_p, xn_s], axis=0), jnp.concatenate([meta_p, meta_s], axis=1),
                         counts, p)
    n_p, n_s = h_p.shape[0], h_s.shape[0]
    y_p = _combine(h_p, gates_p, y_tok, 0, g_fin, min(COMBINE_TILE, n_p)).reshape(x_prompt.shape)
    y_s = _combine(h_s, gates_s, y_tok, n_p, g_fin, min(COMBINE_TILE, n_s)).reshape(x_sample.shape)

    kv_shape = (1, bp, N_MEM, X_HEADS, X_HEAD_DIM)
    return (y_p, y_s,
            conv_p[None], ssm_p[None], hg_p[None], mk_p.reshape(kv_shape), mv_p.reshape(kv_shape),
            conv_s[None], ssm_s[None], hg_s[None])
```

```python
import jax
import jax.numpy as jnp
import numpy as np
from jax import lax
from jax.experimental import pallas as pl
from jax.experimental.pallas import tpu as pltpu
from jax.experimental.pallas import tpu_sc as plsc

F32 = jnp.float32
BF16 = jnp.bfloat16

D_MODEL = 1024
CHUNK = 64
EPS = 1e-6
SSM_INNER = 2 * D_MODEL
SSM_HEAD_DIM = 64
SSM_HEADS = SSM_INNER // SSM_HEAD_DIM
SSM_GROUPS = 4
SSM_STATE = 128
GROUP_COLS = SSM_INNER // SSM_GROUPS
CONV_W = 4
BC_COLS = SSM_GROUPS * SSM_STATE
CONV_CH = SSM_INNER + 2 * BC_COLS
HG_HEADS = 8
HG_K = D_MODEL // HG_HEADS
HG_SUB = CHUNK // 2
N_MEM = 256
X_HEADS = 4
X_HEAD_DIM = D_MODEL // X_HEADS
N_EXPERTS = 32
TOP_K = 4
D_EXPERT = D_MODEL
SWIGLU_LIMIT = 7.0
SWIGLU_ALPHA = 1.702

LANES = 128
CONV_PAD = 8
MOE_ROWS_MAX = 512
MOE_ROWS_MIN = 128
SSD_TILE = 256
HGRN_TILE = 256
POST_TILE = 512
COMBINE_TILE = 512
VMEM_LIMIT = 56 * 1024 * 1024
SC_CORES = 2
SC_SUBCORES = 16
SC_WORKERS = SC_CORES * SC_SUBCORES
SC_CHUNK = 64
SC_BUFFERS = 2


def _const_spec(shape):
    nd = len(shape)
    return pl.BlockSpec(shape, lambda *_: (0,) * nd, pipeline_mode=pl.Buffered(1))


def _dot(a, b):
    return jnp.dot(a, b, preferred_element_type=F32)


def _dot_nt(a, b):
    return lax.dot_general(a, b, (((1,), (1,)), ((), ())), preferred_element_type=F32)


def _dot_tn(a, b):
    return lax.dot_general(a, b, (((0,), (0,)), ((), ())), preferred_element_type=F32)


def _split3(a):
    hi = a.astype(BF16)
    r1 = a - hi.astype(F32)
    mid = r1.astype(BF16)
    lo = (r1 - mid.astype(F32)).astype(BF16)
    return hi, mid, lo


def _exact_dot_lhs01(sel, a):
    hi, mid, lo = _split3(a)
    return _dot(sel, hi) + _dot(sel, mid) + _dot(sel, lo)


def _rmsnorm(x, g):
    return x * lax.rsqrt(jnp.mean(x * x, axis=-1, keepdims=True) + EPS) * g


def _sigmoid(x):
    return jax.nn.sigmoid(x)


def _silu(x):
    return x * jax.nn.sigmoid(x)


def _softplus(x):
    return jnp.maximum(x, 0.0) + jnp.log1p(jnp.exp(-jnp.abs(x)))


def _pack_bf16_pairs(x):
    c = x.shape[1] // 2
    hi = lax.bitcast_convert_type(x[:, :c].astype(BF16).astype(F32), jnp.int32)
    lo = lax.bitcast_convert_type(x[:, c:].astype(BF16).astype(F32), jnp.int32)
    return hi | lax.shift_right_logical(lo, 16)


def _unpack_bf16_pairs(u):
    hi = lax.bitcast_convert_type(u & jnp.int32(-65536), F32)
    lo = lax.bitcast_convert_type(lax.shift_left(u, 16), F32)
    return hi, lo


def _ssd_kernel(x_ref, conv0_ref, ssm0_ref, g_ref, wxbc_ref, wz_ref, wdt_ref, wga_ref, convw_ref,
                convb_ref, dtb_ref, alog_ref, dskip_ref, gssm_ref, wbr_ref, tri_ref, expand_ref,
                ma_ref, convn_ref, ssmn_ref,
                xpad, xact, xdt, ysc, st):
    tt = x_ref.shape[0]
    ti = pl.program_id(1)
    nt = pl.num_programs(1)

    @pl.when(ti == 0)
    def _():
        xpad[CONV_PAD - (CONV_W - 1):CONV_PAD, :] = conv0_ref[...]
        st[...] = ssm0_ref[...].reshape(SSM_INNER, SSM_STATE).T

    xn = _rmsnorm(x_ref[...], g_ref[...]).astype(BF16)
    xpad[CONV_PAD:CONV_PAD + tt, :] = _dot(xn, wxbc_ref[...])

    conv = convb_ref[...] + xpad[CONV_PAD:CONV_PAD + tt, :] * convw_ref[CONV_W - 1:CONV_W, :]
    for k in range(1, CONV_W):
        conv = conv + xpad[CONV_PAD - k:CONV_PAD - k + tt, :] * convw_ref[CONV_W - 1 - k:CONV_W - k, :]
    xact[...] = _silu(conv)
    tail = xpad[CONV_PAD + tt - (CONV_W - 1):CONV_PAD + tt, :]
    xpad[CONV_PAD - (CONV_W - 1):CONV_PAD, :] = tail

    dt = _softplus(_dot(xn, wdt_ref[...]) + dtb_ref[...])
    acum_all = _exact_dot_lhs01(tri_ref[...], dt * -jnp.exp(alog_ref[...]))
    dt_hi = dt.astype(BF16)
    dt_lo = (dt - dt_hi.astype(F32)).astype(BF16)
    dt_x = _dot(dt_hi, expand_ref[...]) + _dot(dt_lo, expand_ref[...])
    xdt[...] = xact[:, :SSM_INNER] * dt_x

    lane = lax.broadcasted_iota(jnp.int32, (CHUNK, LANES), 1)
    row = lax.broadcasted_iota(jnp.int32, (CHUNK, LANES), 0)
    causal2 = row >= (lane % CHUNK)
    diag2 = row == (lane % CHUNK)
    left = lane < CHUNK
    pair_of_lane = lane // CHUNK

    for c in range(tt // CHUNK):
        rows = slice(c * CHUNK, (c + 1) * CHUNK)
        acum = acum_all[rows, :]
        for g in range(SSM_GROUPS):
            gs = slice(g * GROUP_COLS, (g + 1) * GROUP_COLS)
            b_g = xact[rows, SSM_INNER + g * SSM_STATE:SSM_INNER + (g + 1) * SSM_STATE].astype(BF16)
            c_g = xact[rows, SSM_INNER + BC_COLS + g * SSM_STATE:
                       SSM_INNER + BC_COLS + (g + 1) * SSM_STATE].astype(BF16)
            cb2 = _dot_nt(c_g, jnp.concatenate([b_g, b_g], axis=0))
            st_g = st[:, gs]
            y_inter = _dot(c_g, st_g.astype(BF16))
            xw, decay = [], []
            for j in range(GROUP_COLS // LANES):
                pair = g * (GROUP_COLS // LANES) + j
                ps = slice(pair * LANES, (pair + 1) * LANES)
                a_col = jnp.take_along_axis(acum, pair_of_lane + 2 * pair, axis=1)
                a_row = jnp.sum(jnp.where(diag2, a_col, 0.0), axis=0, keepdims=True)
                a_last = a_col[CHUNK - 1:CHUNK, :]
                dec = jnp.where(causal2, jnp.exp(jnp.minimum(a_col - a_row, 0.0)), 0.0)
                wts = (cb2 * dec).astype(BF16)
                xp = xdt[rows, ps]
                xbd = jnp.concatenate([jnp.where(left, xp, 0.0), jnp.where(left, 0.0, xp)],
                                      axis=0).astype(BF16)
                ysc[rows, ps] = _dot(wts, xbd) + y_inter[:, j * LANES:(j + 1) * LANES] * jnp.exp(a_col)
                xw.append((jnp.exp(a_last - a_col) * xp).astype(BF16))
                decay.append(jnp.exp(a_last))
            st[:, gs] = jnp.concatenate(decay, axis=1) * st_g + _dot_tn(b_g, jnp.concatenate(xw, axis=1))

    xs = xact[:, :SSM_INNER]
    y = ysc[...] + dskip_ref[...] * xs
    yz = y * _silu(_dot(xn, wz_ref[...]))
    parts = []
    for g in range(SSM_GROUPS):
        blk = yz[:, g * GROUP_COLS:(g + 1) * GROUP_COLS]
        parts.append(blk * lax.rsqrt(jnp.mean(blk * blk, axis=-1, keepdims=True) + EPS))
    yn = (jnp.concatenate(parts, axis=1) * gssm_ref[...]).astype(BF16)
    ya = _dot(yn, wbr_ref[...])
    ma_ref[...] = _sigmoid(_dot(xn, wga_ref[...])) * ya

    @pl.when(ti == nt - 1)
    def _():
        convn_ref[...] = tail
        ssmn_ref[...] = st[...].T.reshape(SSM_HEADS, SSM_HEAD_DIM, SSM_STATE)


def _chunk_tri(tt):
    return jnp.asarray(np.kron(np.eye(tt // CHUNK), np.tril(np.ones((CHUNK, CHUNK)))), BF16)


def _ssd_mixer(x, conv0, ssm0, p, tt):
    b, t, d = x.shape
    grid = (b, t // tt)
    row_spec = pl.BlockSpec((None, tt, d), lambda i, j: (i, j, 0))
    consts = [p['g_mix'], p['w_xbc'], p['w_z'], p['w_dt'], p['w_ga'], p['conv_w'], p['conv_b'], p['dt_bias'],
              p['a_log'], p['d_skip'], p['g_ssm'], p['w_ssm_branch'], _chunk_tri(tt), p['expand']]
    return pl.pallas_call(
        _ssd_kernel,
        grid=grid,
        in_specs=[row_spec,
                  pl.BlockSpec((None, None, CONV_W - 1, CONV_CH), lambda i, j: (0, i, 0, 0)),
                  pl.BlockSpec((None, None, SSM_HEADS, SSM_HEAD_DIM, SSM_STATE), lambda i, j: (0, i, 0, 0, 0))]
                 + [_const_spec(c.shape) for c in consts],
        out_specs=[row_spec,
                   pl.BlockSpec((None, CONV_W - 1, CONV_CH), lambda i, j: (i, 0, 0)),
                   pl.BlockSpec((None, SSM_HEADS, SSM_HEAD_DIM, SSM_STATE), lambda i, j: (i, 0, 0, 0))],
        out_shape=[jax.ShapeDtypeStruct((b, t, d), F32),
                   jax.ShapeDtypeStruct((b, CONV_W - 1, CONV_CH), F32),
                   jax.ShapeDtypeStruct((b, SSM_HEADS, SSM_HEAD_DIM, SSM_STATE), F32)],
        scratch_shapes=[pltpu.VMEM((CONV_PAD + tt, CONV_CH), F32),
                        pltpu.VMEM((tt, CONV_CH), F32),
                        pltpu.VMEM((tt, SSM_INNER), F32),
                        pltpu.VMEM((tt, SSM_INNER), F32),
                        pltpu.VMEM((SSM_STATE, SSM_INNER), F32)],
        compiler_params=pltpu.CompilerParams(dimension_semantics=("arbitrary", "arbitrary"),
                                             vmem_limit_bytes=VMEM_LIMIT),
        name="ssd_mixer",
    )(x, conv0, ssm0, *consts)


def _hgrn_kernel(x_ref, hg0_ref, g_ref, wh_ref, wgb_ref, lbl_ref, ghg_ref, wbr_ref, tri_ref,
                 mb_ref, hgn_ref,
                 osc, st):
    tt = x_ref.shape[0]
    n_chunks = tt // CHUNK
    ti = pl.program_id(1)
    nt = pl.num_programs(1)

    @pl.when(ti == 0)
    def _():
        for h in range(HG_HEADS):
            st[h] = hg0_ref[h].T

    xn = _rmsnorm(x_ref[...], g_ref[...]).astype(BF16)
    proj = _dot(xn, wh_ref[...])
    l0 = lbl_ref[0:1, :]
    l1 = lbl_ref[1:2, :]
    lmax = jnp.maximum(l0, l1)
    e0 = jnp.exp(l0 - lmax)
    lb = e0 / (e0 + jnp.exp(l1 - lmax))
    fr = proj[:, D_MODEL:2 * D_MODEL]
    kk = (1.0 - lb) * _sigmoid(-fr)
    v = _silu(proj[:, 2 * D_MODEL:3 * D_MODEL]).astype(BF16)
    gc = _exact_dot_lhs01(tri_ref[...], jnp.log(lb + (1.0 - lb) * _sigmoid(fr)))
    q = proj[:, :D_MODEL]
    qh = (q * jnp.exp(gc)).astype(BF16)
    mid = []
    for m in range(tt // HG_SUB):
        r = m * HG_SUB + HG_SUB // 2 - 1
        mid.append(jnp.broadcast_to(gc[r:r + 1, :], (HG_SUB, D_MODEL)))
    d_mid = gc - jnp.concatenate(mid, axis=0)
    q_sub = (q * jnp.exp(d_mid)).astype(BF16)
    k_sub = (kk * jnp.exp(-d_mid)).astype(BF16)
    q_far, k_far, v_first = [], [], []
    for c in range(n_chunks):
        first = slice(c * CHUNK, c * CHUNK + HG_SUB)
        second = slice(c * CHUNK + HG_SUB, (c + 1) * CHUNK)
        g_half = gc[c * CHUNK + HG_SUB - 1:c * CHUNK + HG_SUB, :]
        q_far.append(q[second, :] * jnp.exp(gc[second, :] - g_half))
        k_far.append(kk[first, :] * jnp.exp(g_half - gc[first, :]))
        v_first.append(v[first, :])
    q_far = jnp.concatenate(q_far, axis=0).astype(BF16)
    k_far = jnp.concatenate(k_far, axis=0).astype(BF16)
    v_first = jnp.concatenate(v_first, axis=0)
    kt, decay = [], []
    for c in range(n_chunks):
        rows = slice(c * CHUNK, (c + 1) * CHUNK)
        glast = gc[(c + 1) * CHUNK - 1:(c + 1) * CHUNK, :]
        kt.append((kk[rows, :] * jnp.exp(glast - gc[rows, :])).astype(BF16))
        decay.append(jnp.exp(glast))

    row = lax.broadcasted_iota(jnp.int32, (tt, tt), 0)
    col = lax.broadcasted_iota(jnp.int32, (tt, tt), 1)
    near = jnp.logical_and(row >= col, row // HG_SUB == col // HG_SUB)
    same_chunk = (row // HG_SUB == col // HG_SUB)[:tt // 2, :tt // 2]
    no_far = jnp.zeros((HG_SUB, HG_K), F32)

    for h in range(HG_HEADS):
        hs = slice(h * HG_K, (h + 1) * HG_K)
        att = jnp.where(near, _dot_nt(q_sub[:, hs], k_sub[:, hs]), 0.0).astype(BF16)
        att_far = jnp.where(same_chunk, _dot_nt(q_far[:, hs], k_far[:, hs]), 0.0).astype(BF16)
        o_far = _dot(att_far, v_first[:, hs])
        far_rows = []
        for c in range(n_chunks):
            far_rows += [no_far, o_far[c * HG_SUB:(c + 1) * HG_SUB, :]]
        o_intra = _dot(att, v[:, hs]) + jnp.concatenate(far_rows, axis=0)
        s = st[h]
        o_inter = []
        for c in range(n_chunks):
            rows = slice(c * CHUNK, (c + 1) * CHUNK)
            o_inter.append(_dot_nt(qh[rows, hs], s.astype(BF16)))
            s = decay[c][:, hs] * s + _dot_tn(v[rows, hs], kt[c][:, hs])
        st[h] = s
        osc[:, hs] = o_intra + jnp.concatenate(o_inter, axis=0)

    parts = []
    for h in range(HG_HEADS):
        blk = osc[:, h * HG_K:(h + 1) * HG_K]
        parts.append(blk * lax.rsqrt(jnp.mean(blk * blk, axis=-1, keepdims=True) + EPS))
    on = (jnp.concatenate(parts, axis=1) * ghg_ref[...] * _silu(proj[:, 3 * D_MODEL:])).astype(BF16)
    yb = _dot(on, wbr_ref[...])
    mb_ref[...] = _sigmoid(_dot(xn, wgb_ref[...])) * yb

    @pl.when(ti == nt - 1)
    def _():
        for h in range(HG_HEADS):
            hgn_ref[h] = st[h].T


def _hgrn_mixer(x, hg0, p, tt):
    b, t, d = x.shape
    grid = (b, t // tt)
    row_spec = pl.BlockSpec((None, tt, d), lambda i, j: (i, j, 0))
    st_spec = pl.BlockSpec((None, HG_HEADS, HG_K, HG_K), lambda i, j: (i, 0, 0, 0))
    consts = [p['g_mix'], p['w_hg'], p['w_gb'], p['lb_logits'], p['g_hgrn'], p['w_hgrn_branch'], _chunk_tri(tt)]
    return pl.pallas_call(
        _hgrn_kernel,
        grid=grid,
        in_specs=[row_spec, pl.BlockSpec((None, None, HG_HEADS, HG_K, HG_K), lambda i, j: (0, i, 0, 0, 0))]
                 + [_const_spec(c.shape) for c in consts],
        out_specs=[row_spec, st_spec],
        out_shape=[jax.ShapeDtypeStruct((b, t, d), F32),
                   jax.ShapeDtypeStruct((b, HG_HEADS, HG_K, HG_K), F32)],
        scratch_shapes=[pltpu.VMEM((tt, d), F32), pltpu.VMEM((HG_HEADS, HG_K, HG_K), F32)],
        compiler_params=pltpu.CompilerParams(dimension_semantics=("arbitrary", "arbitrary"),
                                             vmem_limit_bytes=VMEM_LIMIT),
        name="hgrn_mixer",
    )(x, hg0, *consts)


def _memkv_kernel(m_ref, g_ref, wk_ref, wv_ref, k_ref, v_ref):
    mn = _rmsnorm(m_ref[...], g_ref[...]).astype(BF16)
    k_ref[...] = _dot(mn, wk_ref[...])
    v_ref[...] = _dot(mn, wv_ref[...])


def _memory_kv(mem, p):
    b, n, d = mem.shape
    spec = pl.BlockSpec((None, n, d), lambda i: (i, 0, 0))
    consts = [p['g_mem'], p['w_xk'], p['w_xv']]
    return pl.pallas_call(
        _memkv_kernel,
        grid=(b,),
        in_specs=[spec] + [_const_spec(c.shape) for c in consts],
        out_specs=[spec, spec],
        out_shape=[jax.ShapeDtypeStruct((b, n, d), F32)] * 2,
        compiler_params=pltpu.CompilerParams(dimension_semantics=("arbitrary",),
                                             vmem_limit_bytes=VMEM_LIMIT),
        name="memory_kv",
    )(mem, *consts)


def _post_kernel(x_ref, ma_ref, mb_ref, mk_ref, mv_ref, cnt0_ref, wo_ref, gx_ref, wq_ref, wxo_ref,
                 gmoe_ref, wr_ref, br_ref, triu_ref,
                 h_ref, xn_ref, meta_ref, gate_ref, cnt_ref,
                 base):
    tm = x_ref.shape[0]
    first = jnp.logical_and(pl.program_id(0) == 0, pl.program_id(1) == 0)

    @pl.when(first)
    def _():
        base[...] = cnt0_ref[...]

    m = (ma_ref[...] + mb_ref[...]).astype(BF16)
    h1 = x_ref[...] + _dot(m, wo_ref[...])

    hn = _rmsnorm(h1, gx_ref[...]).astype(BF16)
    q = _dot(hn, wq_ref[...])
    heads = []
    for hh in range(X_HEADS):
        hs = slice(hh * X_HEAD_DIM, (hh + 1) * X_HEAD_DIM)
        s = _dot_nt(q[:, hs].astype(BF16), mk_ref[:, hs].astype(BF16)) * (X_HEAD_DIM ** -0.5)
        s = s - jnp.max(s, axis=-1, keepdims=True)
        e = jnp.exp(s)
        pr = e / jnp.sum(e, axis=-1, keepdims=True)
        heads.append(_dot(pr.astype(BF16), mv_ref[:, hs].astype(BF16)))
    o = jnp.concatenate(heads, axis=1).astype(BF16)
    h2 = h1 + _dot(o, wxo_ref[...])
    h_ref[...] = h2

    xn3 = _rmsnorm(h2, gmoe_ref[...])
    xn_ref[...] = _pack_bf16_pairs(xn3)
    logits = _dot(xn3.astype(BF16), wr_ref[...]) + br_ref[...]

    run = logits.T[:N_EXPERTS, :]
    eid = lax.broadcasted_iota(jnp.int32, (N_EXPERTS, tm), 0).astype(F32)
    vals, ids, hots = [], [], []
    for _ in range(TOP_K):
        mx = jnp.max(run, axis=0, keepdims=True)
        idx = jnp.min(jnp.where(run == mx, eid, float(N_EXPERTS)), axis=0, keepdims=True)
        hot = eid == idx
        run = jnp.where(hot, -jnp.inf, run)
        vals.append(mx)
        ids.append(idx)
        hots.append(hot)
    es = [jnp.exp(v - vals[0]) for v in vals]
    den = es[0] + es[1] + es[2] + es[3]

    tot = jnp.zeros((N_EXPERTS, tm), F32)
    for hot in hots:
        tot = tot + hot.astype(F32)
    before = base[:, :1] + _dot(tot.astype(BF16), triu_ref[...])
    base[...] = base[...] + jnp.sum(tot, axis=1, keepdims=True)
    cnt_ref[...] = base[...]

    ranks = [jnp.sum(jnp.where(hot, before, 0.0), axis=0, keepdims=True) for hot in hots]
    meta_ref[...] = jnp.concatenate(ids + ranks, axis=0).astype(jnp.int32)
    gate_ref[...] = jnp.concatenate([e / den for e in es] + [jnp.zeros_like(den)] * TOP_K, axis=0)


def _post_kernel_into(xn_all_ref, *refs):
    del xn_all_ref
    _post_kernel(*refs)


def _post_mixer(x, ma, mb, mk, mv, cnt0, p, tm, n_all, first_token, xn_all=None):
    b, t, d = x.shape
    grid = (b, t // tm)
    first_block = first_token // tm
    assert first_token % tm == 0
    row_spec = pl.BlockSpec((None, tm, d), lambda i, j: (i, j, 0))
    small_spec = pl.BlockSpec((None, 2 * TOP_K, tm), lambda i, j: (i, 0, j))
    mem_spec = pl.BlockSpec((None, N_MEM, d), lambda i, j: (i, 0, 0))
    cnt_spec = pl.BlockSpec((N_EXPERTS, LANES), lambda i, j: (0, 0))
    xn_spec = pl.BlockSpec((tm, d // 2), lambda i, j: (first_block + i * (t // tm) + j, 0))
    triu = jnp.triu(jnp.ones((tm, tm), BF16), 1)
    consts = [p['w_out'], p['g_xattn'], p['w_xq'], p['w_xo'], p['g_moe'], p['w_router'], p['b_router'], triu]
    in_specs = [row_spec, row_spec, row_spec, mem_spec, mem_spec, cnt_spec] + [_const_spec(c.shape) for c in consts]
    args = (x, ma, mb, mk, mv, cnt0, *consts)
    if xn_all is not None:
        in_specs = [pl.BlockSpec(memory_space=pl.ANY)] + in_specs
        args = (xn_all,) + args
    return pl.pallas_call(
        _post_kernel if xn_all is None else _post_kernel_into,
        grid=grid,
        in_specs=in_specs,
        out_specs=[row_spec, xn_spec, small_spec, small_spec, cnt_spec],
        out_shape=[jax.ShapeDtypeStruct((b, t, d), F32),
                   jax.ShapeDtypeStruct((n_all, d // 2), jnp.int32),
                   jax.ShapeDtypeStruct((b, 2 * TOP_K, t), jnp.int32),
                   jax.ShapeDtypeStruct((b, 2 * TOP_K, t), F32),
                   jax.ShapeDtypeStruct((N_EXPERTS, LANES), F32)],
        scratch_shapes=[pltpu.VMEM((N_EXPERTS, LANES), F32)],
        input_output_aliases={} if xn_all is None else {0: 1},
        compiler_params=pltpu.CompilerParams(dimension_semantics=("arbitrary", "arbitrary"),
                                             vmem_limit_bytes=VMEM_LIMIT),
        name="post_mixer",
    )(*args)


def _sc_chunk(rows_per_worker):
    for chunk in range(SC_CHUNK, 0, -8):
        if rows_per_worker % (SC_BUFFERS * chunk) == 0:
            return chunk
    raise ValueError(f"no SparseCore chunk size for {rows_per_worker} rows per worker")


def _sc_mesh():
    return plsc.VectorSubcoreMesh(core_axis_name="c", subcore_axis_name="s")


def _sc_worker():
    return lax.axis_index("s") * SC_CORES + lax.axis_index("c")


def _dispatch(xn, dest, n_slots):
    n, d = xn.shape
    per_w = n // SC_WORKERS
    chunk = _sc_chunk(per_w)
    n_chunks = per_w // chunk
    idx = dest.reshape(TOP_K, n // chunk, chunk).transpose(1, 0, 2)

    def body(x_hbm, idx_hbm, out_hbm, idx_v, rows_v, rsem, wsem):
        wid = _sc_worker()

        def read(i, b):
            blk = wid * n_chunks + i
            return pltpu.make_async_copy(x_hbm.at[pl.ds(blk * chunk, chunk)], rows_v.at[b], rsem.at[b])

        def read_start(i, b):
            pltpu.sync_copy(idx_hbm.at[wid * n_chunks + i], idx_v.at[b])
            read(i, b).start()

        def write(b, j):
            return pltpu.make_async_copy(rows_v.at[b], out_hbm.at[idx_v.at[b, j]], wsem.at[b])

        for b in range(SC_BUFFERS):
            read_start(b, b)

        @pl.loop(0, n_chunks, step=SC_BUFFERS)
        def _(i0):
            for b in range(SC_BUFFERS):
                i = i0 + b
                read(i, b).wait()
                for j in range(TOP_K):
                    write(b, j).start()
                for j in range(TOP_K):
                    write(b, j).wait()

                @pl.when(i + SC_BUFFERS < n_chunks)
                def _():
                    read_start(i + SC_BUFFERS, b)

    return pl.kernel(
        body, mesh=_sc_mesh(),
        out_type=jax.ShapeDtypeStruct((n_slots, d), xn.dtype),
        scratch_types=[pltpu.VMEM((SC_BUFFERS, TOP_K, chunk), jnp.int32),
                       pltpu.VMEM((SC_BUFFERS, chunk, d), xn.dtype),
                       pltpu.SemaphoreType.DMA((SC_BUFFERS,)),
                       pltpu.SemaphoreType.DMA((SC_BUFFERS,))],
        name="moe_dispatch",
    )(xn, idx)


def _gather_rows(table, idx):
    n_out = idx.shape[0]
    d = table.shape[1]
    per_w = n_out // SC_WORKERS
    chunk = _sc_chunk(per_w)
    n_chunks = per_w // chunk

    def body(table_hbm, idx_hbm, out_hbm, idx_v, rows_v, gsem, wsem):
        base = _sc_worker() * per_w

        def gather(b):
            return pltpu.make_async_copy(table_hbm.at[idx_v.at[b]], rows_v.at[b], gsem.at[b])

        def gather_start(i, b):
            pltpu.sync_copy(idx_hbm.at[pl.ds(base + i * chunk, chunk)], idx_v.at[b])
            gather(b).start()

        def write(i, b):
            return pltpu.make_async_copy(rows_v.at[b], out_hbm.at[pl.ds(base + i * chunk, chunk)], wsem.at[b])

        for b in range(SC_BUFFERS):
            gather_start(b, b)

        @pl.loop(0, n_chunks, step=SC_BUFFERS)
        def _(i0):
            for b in range(SC_BUFFERS):
                i = i0 + b
                gather(b).wait()
                write(i, b).start()
                write(i, b).wait()

                @pl.when(i + SC_BUFFERS < n_chunks)
                def _():
                    gather_start(i + SC_BUFFERS, b)

    return pl.kernel(
        body, mesh=_sc_mesh(),
        out_type=jax.ShapeDtypeStruct((n_out, d), table.dtype),
        scratch_types=[pltpu.VMEM((SC_BUFFERS, chunk), jnp.int32),
                       pltpu.VMEM((SC_BUFFERS, chunk, d), table.dtype),
                       pltpu.SemaphoreType.DMA((SC_BUFFERS,)),
                       pltpu.SemaphoreType.DMA((SC_BUFFERS,))],
        name="moe_gather",
    )(table, idx)


def _moe_kernel(be_ref, valid_ref, x_ref, wgu_ref, bgu_ref, wd_ref, bd_ref, y_ref, wgu_bf, wd_bf):
    i = pl.program_id(0)
    valid = valid_ref[i]

    @pl.when(jnp.logical_or(i == 0, be_ref[i] != be_ref[jnp.maximum(i - 1, 0)]))
    def _():
        wgu_bf[...] = wgu_ref[...].astype(BF16)
        wd_bf[...] = wd_ref[...].astype(BF16)

    @pl.when(valid > 0)
    def _():
        row = lax.broadcasted_iota(jnp.int32, x_ref.shape, 0)
        x_hi, x_lo = _unpack_bf16_pairs(jnp.where(row < valid, x_ref[...], 0))
        half = x_ref.shape[1]
        gu = (_dot(x_hi.astype(BF16), wgu_bf[:half, :]) + _dot(x_lo.astype(BF16), wgu_bf[half:, :])
              + bgu_ref[...])
        gate = jnp.minimum(gu[:, :D_EXPERT], SWIGLU_LIMIT)
        up = jnp.clip(gu[:, D_EXPERT:], -SWIGLU_LIMIT, SWIGLU_LIMIT)
        hmid = ((up + 1.0) * gate * _sigmoid(SWIGLU_ALPHA * gate)).astype(BF16)
        y_ref[...] = _pack_bf16_pairs(_dot(hmid, wd_bf[...]) + bd_ref[...])


def _moe_experts(slots, block_e, block_valid, p):
    n_slots, dh = slots.shape
    d = 2 * dh
    n_blocks = block_e.shape[0]
    block_rows = n_slots // n_blocks
    grid_spec = pltpu.PrefetchScalarGridSpec(
        num_scalar_prefetch=2,
        grid=(n_blocks,),
        in_specs=[pl.BlockSpec((block_rows, dh), lambda i, be, bv: (i, 0)),
                  pl.BlockSpec((None, None, d, 2 * D_EXPERT), lambda i, be, bv: (0, be[i], 0, 0)),
                  pl.BlockSpec((None, 1, 2 * D_EXPERT), lambda i, be, bv: (be[i], 0, 0)),
                  pl.BlockSpec((None, None, D_EXPERT, d), lambda i, be, bv: (0, be[i], 0, 0)),
                  pl.BlockSpec((None, 1, d), lambda i, be, bv: (be[i], 0, 0))],
        out_specs=pl.BlockSpec((block_rows, dh), lambda i, be, bv: (i, 0)),
        scratch_shapes=[pltpu.VMEM((d, 2 * D_EXPERT), BF16), pltpu.VMEM((D_EXPERT, d), BF16)],
    )
    return pl.pallas_call(
        _moe_kernel,
        grid_spec=grid_spec,
        out_shape=jax.ShapeDtypeStruct((n_slots, dh), jnp.int32),
        compiler_params=pltpu.CompilerParams(dimension_semantics=("arbitrary",),
                                             vmem_limit_bytes=VMEM_LIMIT),
        name="moe_experts",
    )(block_e, block_valid, slots, p['w_gate_up'], p['b_gate_up'], p['w_down'], p['b_down'])


def _combine_kernel(h_ref, gate_ref, y_ref, gfin_ref, out_ref):
    half = y_ref.shape[2]
    acc_hi = h_ref[:, :half]
    acc_lo = h_ref[:, half:]
    gates = gate_ref[...].T
    for j in range(TOP_K):
        y_hi, y_lo = _unpack_bf16_pairs(y_ref[j])
        acc_hi = acc_hi + gates[:, j:j + 1] * y_hi
        acc_lo = acc_lo + gates[:, j:j + 1] * y_lo
    out_ref[...] = _rmsnorm(jnp.concatenate([acc_hi, acc_lo], axis=1), gfin_ref[...])


def _combine(h, gates, y_tok, first_token, g_final, tm):
    n, d = h.shape
    first_block = first_token // tm
    assert first_token % tm == 0
    return pl.pallas_call(
        _combine_kernel,
        grid=(n // tm,),
        in_specs=[pl.BlockSpec((tm, d), lambda i: (i, 0)),
                  pl.BlockSpec((2 * TOP_K, tm), lambda i: (0, i)),
                  pl.BlockSpec((TOP_K, tm, d // 2), lambda i: (0, i + first_block, 0)),
                  _const_spec(g_final.shape)],
        out_specs=pl.BlockSpec((tm, d), lambda i: (i, 0)),
        out_shape=jax.ShapeDtypeStruct((n, d), F32),
        compiler_params=pltpu.CompilerParams(dimension_semantics=("arbitrary",),
                                             vmem_limit_bytes=VMEM_LIMIT),
        name="moe_combine",
    )(h, gates, y_tok, g_final)


def _moe_outputs(xn, meta, counts, p):
    n, d = xn.shape
    n_rows = n * TOP_K
    block_rows = max(MOE_ROWS_MIN, min(MOE_ROWS_MAX, n_rows // N_EXPERTS))
    n_blocks = n_rows // block_rows + N_EXPERTS
    n_slots = n_blocks * block_rows
    cnt = counts[:, 0].astype(jnp.int32)
    padded = (cnt + block_rows - 1) // block_rows * block_rows
    pad_end = jnp.cumsum(padded)
    pad_start = pad_end - padded
    block_row0 = jnp.arange(n_blocks, dtype=jnp.int32) * block_rows
    block_e = jnp.minimum(jnp.sum((pad_end[None, :] <= block_row0[:, None]).astype(jnp.int32), axis=1),
                          N_EXPERTS - 1)
    used_end = jnp.sum(jnp.where(block_e[:, None] == jnp.arange(N_EXPERTS)[None, :], (pad_start + cnt)[None, :], 0),
                       axis=1)
    block_valid = jnp.clip(used_end - block_row0, 0, block_rows)
    expert = meta[:TOP_K]
    start = jnp.zeros_like(expert)
    for e in range(N_EXPERTS):
        start = jnp.where(expert == e, pad_start[e], start)
    dest = (start + meta[TOP_K:]).astype(jnp.int32)
    slots = _dispatch(xn, dest, n_slots)
    y_slots = _moe_experts(slots, block_e, block_valid, p)
    return _gather_rows(y_slots, dest.reshape(-1)).reshape(TOP_K, n, d)


def _prepare(g_mix, w_in, conv_w, conv_b, dt_bias, a_log, d_skip, g_ssm, w_ssm_branch, lb_logits, g_hgrn,
             w_hgrn_branch, w_out, g_mem, w_xk, w_xv, g_xattn, w_xq, w_xo, g_moe, w_router, b_router,
             w_gate_up, b_gate_up, w_down, b_down):
    d = D_MODEL
    w = w_in[0]
    o_z, o_xbc = 0, SSM_INNER
    o_dt = o_xbc + CONV_CH
    o_q = o_dt + SSM_HEADS
    o_ga = o_q + 4 * d
    o_gb = o_ga + d

    def row(v):
        return v.reshape(1, -1).astype(F32)

    def lane_pad(v, fill=0.0):
        return jnp.pad(v.reshape(1, -1).astype(F32), ((0, 0), (0, LANES - v.shape[-1])), constant_values=fill)

    head_of_col = np.arange(SSM_INNER) // SSM_HEAD_DIM
    expand = (np.arange(LANES)[:, None] == head_of_col[None, :])
    return dict(
        g_mix=row(g_mix[0]),
        w_z=w[:, o_z:o_z + SSM_INNER].astype(BF16),
        w_xbc=w[:, o_xbc:o_xbc + CONV_CH].astype(BF16),
        w_dt=jnp.pad(w[:, o_dt:o_dt + SSM_HEADS], ((0, 0), (0, LANES - SSM_HEADS))).astype(BF16),
        w_hg=w[:, o_q:o_q + 4 * d].astype(BF16),
        w_ga=w[:, o_ga:o_ga + d].astype(BF16),
        w_gb=w[:, o_gb:o_gb + d].astype(BF16),
        conv_w=conv_w[0].astype(F32),
        conv_b=row(conv_b[0]),
        dt_bias=lane_pad(dt_bias[0]),
        a_log=lane_pad(a_log[0]),
        d_skip=row(jnp.repeat(d_skip[0], SSM_HEAD_DIM)),
        g_ssm=row(g_ssm[0]),
        w_ssm_branch=w_ssm_branch[0].astype(BF16),
        lb_logits=lb_logits.astype(F32),
        g_hgrn=row(g_hgrn[0]),
        w_hgrn_branch=w_hgrn_branch[0].astype(BF16),
        w_out=w_out[0].astype(BF16),
        g_mem=row(g_mem[0]),
        w_xk=w_xk[0].astype(BF16),
        w_xv=w_xv[0].astype(BF16),
        g_xattn=row(g_xattn[0]),
        w_xq=w_xq[0].astype(BF16),
        w_xo=w_xo[0].astype(BF16),
        g_moe=row(g_moe[0]),
        w_router=jnp.pad(w_router[0], ((0, 0), (0, LANES - N_EXPERTS))).astype(BF16),
        b_router=lane_pad(b_router[0], fill=-jnp.inf),
        w_gate_up=w_gate_up.astype(F32),
        b_gate_up=b_gate_up[0].reshape(N_EXPERTS, 1, 2 * D_EXPERT).astype(F32),
        w_down=w_down.astype(F32),
        b_down=b_down[0].reshape(N_EXPERTS, 1, d).astype(F32),
        expand=jnp.asarray(expand, BF16),
    )


def _group_to_router(x, conv0, ssm0, hg0, mk, mv, cnt0, p, n_all, first_token, xn_all):
    b, t, d = x.shape
    ma, conv_n, ssm_n = _ssd_mixer(x, conv0, ssm0, p, min(SSD_TILE, t))
    mb, hg_n = _hgrn_mixer(x, hg0, p, min(HGRN_TILE, t))
    h2, xn_all, meta, gates, counts = _post_mixer(x, ma, mb, mk, mv, cnt0, p, min(POST_TILE, t), n_all,
                                                  first_token, xn_all)
    n = b * t
    meta = meta.transpose(1, 0, 2).reshape(2 * TOP_K, n)
    gates = gates.transpose(1, 0, 2).reshape(2 * TOP_K, n)
    return (h2.reshape(n, d), meta, gates), xn_all, counts, (conv_n, ssm_n, hg_n)


def kernel(x_prompt, x_sample, mem_prompt, state_conv, state_ssm, state_hgrn, cache_mem_k, cache_mem_v, g_mix, w_in, conv_w, conv_b, dt_bias, a_log, d_skip, g_ssm, w_ssm_branch, lb_logits, g_hgrn, w_hgrn_branch, w_out, g_mem, w_xk, w_xv, g_xattn, w_xq, w_xo, g_moe, w_router, b_router, w_gate_up, b_gate_up, w_down, b_down, g_final):
    p = _prepare(g_mix, w_in, conv_w, conv_b, dt_bias, a_log, d_skip, g_ssm, w_ssm_branch, lb_logits, g_hgrn,
                 w_hgrn_branch, w_out, g_mem, w_xk, w_xv, g_xattn, w_xq, w_xo, g_moe, w_router, b_router,
                 w_gate_up, b_gate_up, w_down, b_down)
    g_fin = g_final.reshape(1, -1).astype(F32)
    if w_in.shape[0] != 1:
        raise NotImplementedError("a single layer (DEPTH == 1) is implemented")
    bp = x_prompt.shape[0]
    bs = x_sample.shape[0]
    d = D_MODEL

    mk_p, mv_p = _memory_kv(mem_prompt, p)
    n_p = bp * x_prompt.shape[1]
    n_s = bs * x_sample.shape[1]
    (h_p, meta_p, gates_p), xn_all, counts, (conv_p, ssm_p, hg_p) = _group_to_router(
        x_prompt,
        jnp.zeros((1, bp, CONV_W - 1, CONV_CH), F32),
        jnp.zeros((1, bp, SSM_HEADS, SSM_HEAD_DIM, SSM_STATE), F32),
        jnp.zeros((1, bp, HG_HEADS, HG_K, HG_K), F32),
        mk_p, mv_p, jnp.zeros((N_EXPERTS, LANES), F32), p, n_p + n_s, 0, None)
    (h_s, meta_s, gates_s), xn_all, counts, (conv_s, ssm_s, hg_s) = _group_to_router(
        x_sample, state_conv, state_ssm, state_hgrn,
        cache_mem_k[0].reshape(bs, N_MEM, d), cache_mem_v[0].reshape(bs, N_MEM, d), counts, p, n_p + n_s, n_p,
        xn_all)
    y_tok = _moe_outputs(xn_all, jnp.concatenate([meta_p, meta_s], axis=1), counts, p)
    y_p = _combine(h_p, gates_p, y_tok, 0, g_fin, min(COMBINE_TILE, n_p)).reshape(x_prompt.shape)
    y_s = _combine(h_s, gates_s, y_tok, n_p, g_fin, min(COMBINE_TILE, n_s)).reshape(x_sample.shape)

    kv_shape = (1, bp, N_MEM, X_HEADS, X_HEAD_DIM)
    return (y_p, y_s,
            conv_p[None], ssm_p[None], hg_p[None], mk_p.reshape(kv_shape), mv_p.reshape(kv_shape),
            conv_s[None], ssm_s[None], hg_s[None])
```

```python
import jax
import jax.numpy as jnp
import numpy as np
from jax import lax
from jax.experimental import pallas as pl
from jax.experimental.pallas import tpu as pltpu
from jax.experimental.pallas import tpu_sc as plsc

F32 = jnp.float32
BF16 = jnp.bfloat16

D_MODEL = 1024
CHUNK = 64
EPS = 1e-6
SSM_INNER = 2 * D_MODEL
SSM_HEAD_DIM = 64
SSM_HEADS = SSM_INNER // SSM_HEAD_DIM
SSM_GROUPS = 4
SSM_STATE = 128
GROUP_COLS = SSM_INNER // SSM_GROUPS
CONV_W = 4
BC_COLS = SSM_GROUPS * SSM_STATE
CONV_CH = SSM_INNER + 2 * BC_COLS
HG_HEADS = 8
HG_K = D_MODEL // HG_HEADS
HG_SUB = CHUNK // 2
N_MEM = 256
X_HEADS = 4
X_HEAD_DIM = D_MODEL // X_HEADS
N_EXPERTS = 32
TOP_K = 4
D_EXPERT = D_MODEL
SWIGLU_LIMIT = 7.0
SWIGLU_ALPHA = 1.702

LANES = 128
CONV_PAD = 8
MOE_ROWS_MAX = 512
MOE_ROWS_MIN = 128
SSD_TILE = 256
HGRN_TILE = 256
POST_TILE = 512
COMBINE_TILE = 512
VMEM_LIMIT = 56 * 1024 * 1024
SC_CORES = 2
SC_SUBCORES = 16
SC_WORKERS = SC_CORES * SC_SUBCORES
SC_CHUNK = 104
SC_BUFFERS = 2


def _const_spec(shape):
    nd = len(shape)
    return pl.BlockSpec(shape, lambda *_: (0,) * nd, pipeline_mode=pl.Buffered(1))


def _dot(a, b):
    return jnp.dot(a, b, preferred_element_type=F32)


def _dot_nt(a, b):
    return lax.dot_general(a, b, (((1,), (1,)), ((), ())), preferred_element_type=F32)


def _dot_tn(a, b):
    return lax.dot_general(a, b, (((0,), (0,)), ((), ())), preferred_element_type=F32)


def _split3(a):
    hi = a.astype(BF16)
    r1 = a - hi.astype(F32)
    mid = r1.astype(BF16)
    lo = (r1 - mid.astype(F32)).astype(BF16)
    return hi, mid, lo


def _exact_dot_lhs01(sel, a):
    hi, mid, lo = _split3(a)
    return _dot(sel, hi) + _dot(sel, mid) + _dot(sel, lo)


def _rmsnorm(x, g):
    return x * lax.rsqrt(jnp.mean(x * x, axis=-1, keepdims=True) + EPS) * g


def _sigmoid(x):
    return jax.nn.sigmoid(x)


def _silu(x):
    return x * jax.nn.sigmoid(x)


def _softplus(x):
    return jnp.maximum(x, 0.0) + jnp.log1p(jnp.exp(-jnp.abs(x)))


def _pack_bf16_pairs(x):
    c = x.shape[1] // 2
    hi = lax.bitcast_convert_type(x[:, :c].astype(BF16).astype(F32), jnp.int32)
    lo = lax.bitcast_convert_type(x[:, c:].astype(BF16).astype(F32), jnp.int32)
    return hi | lax.shift_right_logical(lo, 16)


def _unpack_bf16_pairs(u):
    hi = lax.bitcast_convert_type(u & jnp.int32(-65536), F32)
    lo = lax.bitcast_convert_type(lax.shift_left(u, 16), F32)
    return hi, lo


def _ssd_kernel(x_ref, conv0_ref, ssm0_ref, g_ref, wxbc_ref, wz_ref, wdt_ref, wga_ref, convw_ref,
                convb_ref, dtb_ref, alog_ref, dskip_ref, gssm_ref, wbr_ref, tri_ref, expand_ref,
                ma_ref, convn_ref, ssmn_ref,
                xpad, xact, xdt, ysc, st):
    tt = x_ref.shape[0]
    ti = pl.program_id(1)
    nt = pl.num_programs(1)

    @pl.when(ti == 0)
    def _():
        xpad[CONV_PAD - (CONV_W - 1):CONV_PAD, :] = conv0_ref[...]
        st[...] = ssm0_ref[...].reshape(SSM_INNER, SSM_STATE).T

    xn = _rmsnorm(x_ref[...], g_ref[...]).astype(BF16)
    xpad[CONV_PAD:CONV_PAD + tt, :] = _dot(xn, wxbc_ref[...])

    conv = convb_ref[...] + xpad[CONV_PAD:CONV_PAD + tt, :] * convw_ref[CONV_W - 1:CONV_W, :]
    for k in range(1, CONV_W):
        conv = conv + xpad[CONV_PAD - k:CONV_PAD - k + tt, :] * convw_ref[CONV_W - 1 - k:CONV_W - k, :]
    xact[...] = _silu(conv)
    tail = xpad[CONV_PAD + tt - (CONV_W - 1):CONV_PAD + tt, :]
    xpad[CONV_PAD - (CONV_W - 1):CONV_PAD, :] = tail

    dt = _softplus(_dot(xn, wdt_ref[...]) + dtb_ref[...])
    acum_all = _exact_dot_lhs01(tri_ref[...], dt * -jnp.exp(alog_ref[...]))
    dt_hi = dt.astype(BF16)
    dt_lo = (dt - dt_hi.astype(F32)).astype(BF16)
    dt_x = _dot(dt_hi, expand_ref[...]) + _dot(dt_lo, expand_ref[...])
    xdt[...] = xact[:, :SSM_INNER] * dt_x

    lane = lax.broadcasted_iota(jnp.int32, (CHUNK, LANES), 1)
    row = lax.broadcasted_iota(jnp.int32, (CHUNK, LANES), 0)
    causal2 = row >= (lane % CHUNK)
    diag2 = row == (lane % CHUNK)
    left = lane < CHUNK
    pair_of_lane = lane // CHUNK

    for c in range(tt // CHUNK):
        rows = slice(c * CHUNK, (c + 1) * CHUNK)
        acum = acum_all[rows, :]
        for g in range(SSM_GROUPS):
            gs = slice(g * GROUP_COLS, (g + 1) * GROUP_COLS)
            b_g = xact[rows, SSM_INNER + g * SSM_STATE:SSM_INNER + (g + 1) * SSM_STATE].astype(BF16)
            c_g = xact[rows, SSM_INNER + BC_COLS + g * SSM_STATE:
                       SSM_INNER + BC_COLS + (g + 1) * SSM_STATE].astype(BF16)
            cb2 = _dot_nt(c_g, jnp.concatenate([b_g, b_g], axis=0))
            st_g = st[:, gs]
            y_inter = _dot(c_g, st_g.astype(BF16))
            xw, decay = [], []
            for j in range(GROUP_COLS // LANES):
                pair = g * (GROUP_COLS // LANES) + j
                ps = slice(pair * LANES, (pair + 1) * LANES)
                a_col = jnp.take_along_axis(acum, pair_of_lane + 2 * pair, axis=1)
                a_row = jnp.sum(jnp.where(diag2, a_col, 0.0), axis=0, keepdims=True)
                a_last = a_col[CHUNK - 1:CHUNK, :]
                dec = jnp.where(causal2, jnp.exp(jnp.minimum(a_col - a_row, 0.0)), 0.0)
                wts = (cb2 * dec).astype(BF16)
                xp = xdt[rows, ps]
                xbd = jnp.concatenate([jnp.where(left, xp, 0.0), jnp.where(left, 0.0, xp)],
                                      axis=0).astype(BF16)
                ysc[rows, ps] = _dot(wts, xbd) + y_inter[:, j * LANES:(j + 1) * LANES] * jnp.exp(a_col)
                xw.append((jnp.exp(a_last - a_col) * xp).astype(BF16))
                decay.append(jnp.exp(a_last))
            st[:, gs] = jnp.concatenate(decay, axis=1) * st_g + _dot_tn(b_g, jnp.concatenate(xw, axis=1))

    xs = xact[:, :SSM_INNER]
    y = ysc[...] + dskip_ref[...] * xs
    yz = y * _silu(_dot(xn, wz_ref[...]))
    parts = []
    for g in range(SSM_GROUPS):
        blk = yz[:, g * GROUP_COLS:(g + 1) * GROUP_COLS]
        parts.append(blk * lax.rsqrt(jnp.mean(blk * blk, axis=-1, keepdims=True) + EPS))
    yn = (jnp.concatenate(parts, axis=1) * gssm_ref[...]).astype(BF16)
    ya = _dot(yn, wbr_ref[...])
    ma_ref[...] = _sigmoid(_dot(xn, wga_ref[...])) * ya

    @pl.when(ti == nt - 1)
    def _():
        convn_ref[...] = tail
        ssmn_ref[...] = st[...].T.reshape(SSM_HEADS, SSM_HEAD_DIM, SSM_STATE)


def _chunk_tri(tt):
    return jnp.asarray(np.kron(np.eye(tt // CHUNK), np.tril(np.ones((CHUNK, CHUNK)))), BF16)


def _ssd_mixer(x, conv0, ssm0, p, tt):
    b, t, d = x.shape
    grid = (b, t // tt)
    row_spec = pl.BlockSpec((None, tt, d), lambda i, j: (i, j, 0))
    consts = [p['g_mix'], p['w_xbc'], p['w_z'], p['w_dt'], p['w_ga'], p['conv_w'], p['conv_b'], p['dt_bias'],
              p['a_log'], p['d_skip'], p['g_ssm'], p['w_ssm_branch'], _chunk_tri(tt), p['expand']]
    return pl.pallas_call(
        _ssd_kernel,
        grid=grid,
        in_specs=[row_spec,
                  pl.BlockSpec((None, None, CONV_W - 1, CONV_CH), lambda i, j: (0, i, 0, 0)),
                  pl.BlockSpec((None, None, SSM_HEADS, SSM_HEAD_DIM, SSM_STATE), lambda i, j: (0, i, 0, 0, 0))]
                 + [_const_spec(c.shape) for c in consts],
        out_specs=[row_spec,
                   pl.BlockSpec((None, CONV_W - 1, CONV_CH), lambda i, j: (i, 0, 0)),
                   pl.BlockSpec((None, SSM_HEADS, SSM_HEAD_DIM, SSM_STATE), lambda i, j: (i, 0, 0, 0))],
        out_shape=[jax.ShapeDtypeStruct((b, t, d), F32),
                   jax.ShapeDtypeStruct((b, CONV_W - 1, CONV_CH), F32),
                   jax.ShapeDtypeStruct((b, SSM_HEADS, SSM_HEAD_DIM, SSM_STATE), F32)],
        scratch_shapes=[pltpu.VMEM((CONV_PAD + tt, CONV_CH), F32),
                        pltpu.VMEM((tt, CONV_CH), F32),
                        pltpu.VMEM((tt, SSM_INNER), F32),
                        pltpu.VMEM((tt, SSM_INNER), F32),
                        pltpu.VMEM((SSM_STATE, SSM_INNER), F32)],
        compiler_params=pltpu.CompilerParams(dimension_semantics=("arbitrary", "arbitrary"),
                                             vmem_limit_bytes=VMEM_LIMIT),
        name="ssd_mixer",
    )(x, conv0, ssm0, *consts)


def _hgrn_kernel(x_ref, hg0_ref, g_ref, wh_ref, wgb_ref, lbl_ref, ghg_ref, wbr_ref, tri_ref,
                 mb_ref, hgn_ref,
                 osc, st):
    tt = x_ref.shape[0]
    n_chunks = tt // CHUNK
    ti = pl.program_id(1)
    nt = pl.num_programs(1)

    @pl.when(ti == 0)
    def _():
        for h in range(HG_HEADS):
            st[h] = hg0_ref[h].T

    xn = _rmsnorm(x_ref[...], g_ref[...]).astype(BF16)
    proj = _dot(xn, wh_ref[...])
    l0 = lbl_ref[0:1, :]
    l1 = lbl_ref[1:2, :]
    lmax = jnp.maximum(l0, l1)
    e0 = jnp.exp(l0 - lmax)
    lb = e0 / (e0 + jnp.exp(l1 - lmax))
    fr = proj[:, D_MODEL:2 * D_MODEL]
    kk = (1.0 - lb) * _sigmoid(-fr)
    v = _silu(proj[:, 2 * D_MODEL:3 * D_MODEL]).astype(BF16)
    gc = _exact_dot_lhs01(tri_ref[...], jnp.log(lb + (1.0 - lb) * _sigmoid(fr)))
    q = proj[:, :D_MODEL]
    qh = (q * jnp.exp(gc)).astype(BF16)
    mid = []
    for m in range(tt // HG_SUB):
        r = m * HG_SUB + HG_SUB // 2 - 1
        mid.append(jnp.broadcast_to(gc[r:r + 1, :], (HG_SUB, D_MODEL)))
    d_mid = gc - jnp.concatenate(mid, axis=0)
    q_sub = (q * jnp.exp(d_mid)).astype(BF16)
    k_sub = (kk * jnp.exp(-d_mid)).astype(BF16)
    q_far, k_far, v_first = [], [], []
    for c in range(n_chunks):
        first = slice(c * CHUNK, c * CHUNK + HG_SUB)
        second = slice(c * CHUNK + HG_SUB, (c + 1) * CHUNK)
        g_half = gc[c * CHUNK + HG_SUB - 1:c * CHUNK + HG_SUB, :]
        q_far.append(q[second, :] * jnp.exp(gc[second, :] - g_half))
        k_far.append(kk[first, :] * jnp.exp(g_half - gc[first, :]))
        v_first.append(v[first, :])
    q_far = jnp.concatenate(q_far, axis=0).astype(BF16)
    k_far = jnp.concatenate(k_far, axis=0).astype(BF16)
    v_first = jnp.concatenate(v_first, axis=0)
    kt, decay = [], []
    for c in range(n_chunks):
        rows = slice(c * CHUNK, (c + 1) * CHUNK)
        glast = gc[(c + 1) * CHUNK - 1:(c + 1) * CHUNK, :]
        kt.append((kk[rows, :] * jnp.exp(glast - gc[rows, :])).astype(BF16))
        decay.append(jnp.exp(glast))

    row = lax.broadcasted_iota(jnp.int32, (tt, tt), 0)
    col = lax.broadcasted_iota(jnp.int32, (tt, tt), 1)
    near = jnp.logical_and(row >= col, row // HG_SUB == col // HG_SUB)
    same_chunk = (row // HG_SUB == col // HG_SUB)[:tt // 2, :tt // 2]
    no_far = jnp.zeros((HG_SUB, HG_K), F32)

    for h in range(HG_HEADS):
        hs = slice(h * HG_K, (h + 1) * HG_K)
        att = jnp.where(near, _dot_nt(q_sub[:, hs], k_sub[:, hs]), 0.0).astype(BF16)
        att_far = jnp.where(same_chunk, _dot_nt(q_far[:, hs], k_far[:, hs]), 0.0).astype(BF16)
        o_far = _dot(att_far, v_first[:, hs])
        far_rows = []
        for c in range(n_chunks):
            far_rows += [no_far, o_far[c * HG_SUB:(c + 1) * HG_SUB, :]]
        o_intra = _dot(att, v[:, hs]) + jnp.concatenate(far_rows, axis=0)
        s = st[h]
        o_inter = []
        for c in range(n_chunks):
            rows = slice(c * CHUNK, (c + 1) * CHUNK)
            o_inter.append(_dot_nt(qh[rows, hs], s.astype(BF16)))
            s = decay[c][:, hs] * s + _dot_tn(v[rows, hs], kt[c][:, hs])
        st[h] = s
        osc[:, hs] = o_intra + jnp.concatenate(o_inter, axis=0)

    parts = []
    for h in range(HG_HEADS):
        blk = osc[:, h * HG_K:(h + 1) * HG_K]
        parts.append(blk * lax.rsqrt(jnp.mean(blk * blk, axis=-1, keepdims=True) + EPS))
    on = (jnp.concatenate(parts, axis=1) * ghg_ref[...] * _silu(proj[:, 3 * D_MODEL:])).astype(BF16)
    yb = _dot(on, wbr_ref[...])
    mb_ref[...] = _sigmoid(_dot(xn, wgb_ref[...])) * yb

    @pl.when(ti == nt - 1)
    def _():
        for h in range(HG_HEADS):
            hgn_ref[h] = st[h].T


def _hgrn_mixer(x, hg0, p, tt):
    b, t, d = x.shape
    grid = (b, t // tt)
    row_spec = pl.BlockSpec((None, tt, d), lambda i, j: (i, j, 0))
    st_spec = pl.BlockSpec((None, HG_HEADS, HG_K, HG_K), lambda i, j: (i, 0, 0, 0))
    consts = [p['g_mix'], p['w_hg'], p['w_gb'], p['lb_logits'], p['g_hgrn'], p['w_hgrn_branch'], _chunk_tri(tt)]
    return pl.pallas_call(
        _hgrn_kernel,
        grid=grid,
        in_specs=[row_spec, pl.BlockSpec((None, None, HG_HEADS, HG_K, HG_K), lambda i, j: (0, i, 0, 0, 0))]
                 + [_const_spec(c.shape) for c in consts],
        out_specs=[row_spec, st_spec],
        out_shape=[jax.ShapeDtypeStruct((b, t, d), F32),
                   jax.ShapeDtypeStruct((b, HG_HEADS, HG_K, HG_K), F32)],
        scratch_shapes=[pltpu.VMEM((tt, d), F32), pltpu.VMEM((HG_HEADS, HG_K, HG_K), F32)],
        compiler_params=pltpu.CompilerParams(dimension_semantics=("arbitrary", "arbitrary"),
                                             vmem_limit_bytes=VMEM_LIMIT),
        name="hgrn_mixer",
    )(x, hg0, *consts)


def _memkv_kernel(m_ref, g_ref, wk_ref, wv_ref, k_ref, v_ref):
    mn = _rmsnorm(m_ref[...], g_ref[...]).astype(BF16)
    k_ref[...] = _dot(mn, wk_ref[...])
    v_ref[...] = _dot(mn, wv_ref[...])


def _memory_kv(mem, p):
    b, n, d = mem.shape
    spec = pl.BlockSpec((None, n, d), lambda i: (i, 0, 0))
    consts = [p['g_mem'], p['w_xk'], p['w_xv']]
    return pl.pallas_call(
        _memkv_kernel,
        grid=(b,),
        in_specs=[spec] + [_const_spec(c.shape) for c in consts],
        out_specs=[spec, spec],
        out_shape=[jax.ShapeDtypeStruct((b, n, d), F32)] * 2,
        compiler_params=pltpu.CompilerParams(dimension_semantics=("arbitrary",),
                                             vmem_limit_bytes=VMEM_LIMIT),
        name="memory_kv",
    )(mem, *consts)


def _post_kernel(x_ref, ma_ref, mb_ref, mk_ref, mv_ref, cnt0_ref, wo_ref, gx_ref, wq_ref, wxo_ref,
                 gmoe_ref, wr_ref, br_ref, triu_ref,
                 h_ref, xn_ref, meta_ref, gate_ref, cnt_ref,
                 base):
    tm = x_ref.shape[0]
    first = jnp.logical_and(pl.program_id(0) == 0, pl.program_id(1) == 0)

    @pl.when(first)
    def _():
        base[...] = cnt0_ref[...]

    m = (ma_ref[...] + mb_ref[...]).astype(BF16)
    h1 = x_ref[...] + _dot(m, wo_ref[...])

    hn = _rmsnorm(h1, gx_ref[...]).astype(BF16)
    q = _dot(hn, wq_ref[...])
    heads = []
    for hh in range(X_HEADS):
        hs = slice(hh * X_HEAD_DIM, (hh + 1) * X_HEAD_DIM)
        s = _dot_nt(q[:, hs].astype(BF16), mk_ref[:, hs].astype(BF16)) * (X_HEAD_DIM ** -0.5)
        s = s - jnp.max(s, axis=-1, keepdims=True)
        e = jnp.exp(s)
        pr = e / jnp.sum(e, axis=-1, keepdims=True)
        heads.append(_dot(pr.astype(BF16), mv_ref[:, hs].astype(BF16)))
    o = jnp.concatenate(heads, axis=1).astype(BF16)
    h2 = h1 + _dot(o, wxo_ref[...])
    h_ref[...] = h2

    xn3 = _rmsnorm(h2, gmoe_ref[...])
    xn_ref[...] = _pack_bf16_pairs(xn3)
    logits = _dot(xn3.astype(BF16), wr_ref[...]) + br_ref[...]

    run = logits.T[:N_EXPERTS, :]
    eid = lax.broadcasted_iota(jnp.int32, (N_EXPERTS, tm), 0).astype(F32)
    vals, ids, hots = [], [], []
    for _ in range(TOP_K):
        mx = jnp.max(run, axis=0, keepdims=True)
        idx = jnp.min(jnp.where(run == mx, eid, float(N_EXPERTS)), axis=0, keepdims=True)
        hot = eid == idx
        run = jnp.where(hot, -jnp.inf, run)
        vals.append(mx)
        ids.append(idx)
        hots.append(hot)
    es = [jnp.exp(v - vals[0]) for v in vals]
    den = es[0] + es[1] + es[2] + es[3]

    tot = jnp.zeros((N_EXPERTS, tm), F32)
    for hot in hots:
        tot = tot + hot.astype(F32)
    before = base[:, :1] + _dot(tot.astype(BF16), triu_ref[...])
    base[...] = base[...] + jnp.sum(tot, axis=1, keepdims=True)
    cnt_ref[...] = base[...]

    ranks = [jnp.sum(jnp.where(hot, before, 0.0), axis=0, keepdims=True) for hot in hots]
    meta_ref[...] = jnp.concatenate(ids + ranks, axis=0).astype(jnp.int32)
    gate_ref[...] = jnp.concatenate([e / den for e in es] + [jnp.zeros_like(den)] * TOP_K, axis=0)


def _post_kernel_into(xn_all_ref, *refs):
    del xn_all_ref
    _post_kernel(*refs)


def _post_mixer(x, ma, mb, mk, mv, cnt0, p, tm, n_all, first_token, xn_all=None):
    b, t, d = x.shape
    grid = (b, t // tm)
    first_block = first_token // tm
    assert first_token % tm == 0
    row_spec = pl.BlockSpec((None, tm, d), lambda i, j: (i, j, 0))
    small_spec = pl.BlockSpec((None, 2 * TOP_K, tm), lambda i, j: (i, 0, j))
    mem_spec = pl.BlockSpec((None, N_MEM, d), lambda i, j: (i, 0, 0))
    cnt_spec = pl.BlockSpec((N_EXPERTS, LANES), lambda i, j: (0, 0))
    xn_spec = pl.BlockSpec((tm, d // 2), lambda i, j: (first_block + i * (t // tm) + j, 0))
    triu = jnp.triu(jnp.ones((tm, tm), BF16), 1)
    consts = [p['w_out'], p['g_xattn'], p['w_xq'], p['w_xo'], p['g_moe'], p['w_router'], p['b_router'], triu]
    in_specs = [row_spec, row_spec, row_spec, mem_spec, mem_spec, cnt_spec] + [_const_spec(c.shape) for c in consts]
    args = (x, ma, mb, mk, mv, cnt0, *consts)
    if xn_all is not None:
        in_specs = [pl.BlockSpec(memory_space=pl.ANY)] + in_specs
        args = (xn_all,) + args
    return pl.pallas_call(
        _post_kernel if xn_all is None else _post_kernel_into,
        grid=grid,
        in_specs=in_specs,
        out_specs=[row_spec, xn_spec, small_spec, small_spec, cnt_spec],
        out_shape=[jax.ShapeDtypeStruct((b, t, d), F32),
                   jax.ShapeDtypeStruct((n_all, d // 2), jnp.int32),
                   jax.ShapeDtypeStruct((b, 2 * TOP_K, t), jnp.int32),
                   jax.ShapeDtypeStruct((b, 2 * TOP_K, t), F32),
                   jax.ShapeDtypeStruct((N_EXPERTS, LANES), F32)],
        scratch_shapes=[pltpu.VMEM((N_EXPERTS, LANES), F32)],
        input_output_aliases={} if xn_all is None else {0: 1},
        compiler_params=pltpu.CompilerParams(dimension_semantics=("arbitrary", "arbitrary"),
                                             vmem_limit_bytes=VMEM_LIMIT),
        name="post_mixer",
    )(*args)


def _sc_chunk(rows_per_worker):
    for chunk in range(SC_CHUNK, 0, -8):
        if rows_per_worker % (SC_BUFFERS * chunk) == 0:
            return chunk
    raise ValueError(f"no SparseCore chunk size for {rows_per_worker} rows per worker")


def _sc_mesh():
    return plsc.VectorSubcoreMesh(core_axis_name="c", subcore_axis_name="s")


def _sc_worker():
    return lax.axis_index("s") * SC_CORES + lax.axis_index("c")


def _dispatch(xn, dest, n_slots):
    n, d = xn.shape
    per_w = n // SC_WORKERS
    chunk = _sc_chunk(per_w)
    n_chunks = per_w // chunk
    idx = dest.reshape(TOP_K, n // chunk, chunk).transpose(1, 0, 2)

    def body(x_hbm, idx_hbm, out_hbm, idx_v, rows_v, rsem, wsem):
        wid = _sc_worker()

        def read(i, b):
            blk = wid * n_chunks + i
            return pltpu.make_async_copy(x_hbm.at[pl.ds(blk * chunk, chunk)], rows_v.at[b], rsem.at[b])

        def read_start(i, b):
            pltpu.sync_copy(idx_hbm.at[wid * n_chunks + i], idx_v.at[b])
            read(i, b).start()

        def write(b, j):
            return pltpu.make_async_copy(rows_v.at[b], out_hbm.at[idx_v.at[b, j]], wsem.at[b])

        for b in range(SC_BUFFERS):
            read_start(b, b)

        @pl.loop(0, n_chunks, step=SC_BUFFERS)
        def _(i0):
            for b in range(SC_BUFFERS):
                i = i0 + b
                read(i, b).wait()
                for j in range(TOP_K):
                    write(b, j).start()
                for j in range(TOP_K):
                    write(b, j).wait()

                @pl.when(i + SC_BUFFERS < n_chunks)
                def _():
                    read_start(i + SC_BUFFERS, b)

    return pl.kernel(
        body, mesh=_sc_mesh(),
        out_type=jax.ShapeDtypeStruct((n_slots, d), xn.dtype),
        scratch_types=[pltpu.VMEM((SC_BUFFERS, TOP_K, chunk), jnp.int32),
                       pltpu.VMEM((SC_BUFFERS, chunk, d), xn.dtype),
                       pltpu.SemaphoreType.DMA((SC_BUFFERS,)),
                       pltpu.SemaphoreType.DMA((SC_BUFFERS,))],
        name="moe_dispatch",
    )(xn, idx)


def _gather_rows(table, idx):
    n_out = idx.shape[0]
    d = table.shape[1]
    per_w = n_out // SC_WORKERS
    chunk = _sc_chunk(per_w)
    n_chunks = per_w // chunk

    def body(table_hbm, idx_hbm, out_hbm, idx_v, rows_v, gsem, wsem):
        base = _sc_worker() * per_w

        def gather(b):
            return pltpu.make_async_copy(table_hbm.at[idx_v.at[b]], rows_v.at[b], gsem.at[b])

        def gather_start(i, b):
            pltpu.sync_copy(idx_hbm.at[pl.ds(base + i * chunk, chunk)], idx_v.at[b])
            gather(b).start()

        def write(i, b):
            return pltpu.make_async_copy(rows_v.at[b], out_hbm.at[pl.ds(base + i * chunk, chunk)], wsem.at[b])

        for b in range(SC_BUFFERS):
            gather_start(b, b)

        @pl.loop(0, n_chunks, step=SC_BUFFERS)
        def _(i0):
            for b in range(SC_BUFFERS):
                i = i0 + b
                gather(b).wait()
                write(i, b).start()
                write(i, b).wait()

                @pl.when(i + SC_BUFFERS < n_chunks)
                def _():
                    gather_start(i + SC_BUFFERS, b)

    return pl.kernel(
        body, mesh=_sc_mesh(),
        out_type=jax.ShapeDtypeStruct((n_out, d), table.dtype),
        scratch_types=[pltpu.VMEM((SC_BUFFERS, chunk), jnp.int32),
                       pltpu.VMEM((SC_BUFFERS, chunk, d), table.dtype),
                       pltpu.SemaphoreType.DMA((SC_BUFFERS,)),
                       pltpu.SemaphoreType.DMA((SC_BUFFERS,))],
        name="moe_gather",
    )(table, idx)


def _moe_kernel(be_ref, valid_ref, x_ref, wgu_ref, bgu_ref, wd_ref, bd_ref, y_ref, wgu_bf, wd_bf):
    i = pl.program_id(0)
    valid = valid_ref[i]

    @pl.when(jnp.logical_or(i == 0, be_ref[i] != be_ref[jnp.maximum(i - 1, 0)]))
    def _():
        wgu_bf[...] = wgu_ref[...].astype(BF16)
        wd_bf[...] = wd_ref[...].astype(BF16)

    @pl.when(valid > 0)
    def _():
        row = lax.broadcasted_iota(jnp.int32, x_ref.shape, 0)
        x_hi, x_lo = _unpack_bf16_pairs(jnp.where(row < valid, x_ref[...], 0))
        half = x_ref.shape[1]
        gu = (_dot(x_hi.astype(BF16), wgu_bf[:half, :]) + _dot(x_lo.astype(BF16), wgu_bf[half:, :])
              + bgu_ref[...])
        gate = jnp.minimum(gu[:, :D_EXPERT], SWIGLU_LIMIT)
        up = jnp.clip(gu[:, D_EXPERT:], -SWIGLU_LIMIT, SWIGLU_LIMIT)
        hmid = ((up + 1.0) * gate * _sigmoid(SWIGLU_ALPHA * gate)).astype(BF16)
        y_ref[...] = _pack_bf16_pairs(_dot(hmid, wd_bf[...]) + bd_ref[...])


def _moe_experts(slots, block_e, block_valid, p):
    n_slots, dh = slots.shape
    d = 2 * dh
    n_blocks = block_e.shape[0]
    block_rows = n_slots // n_blocks
    grid_spec = pltpu.PrefetchScalarGridSpec(
        num_scalar_prefetch=2,
        grid=(n_blocks,),
        in_specs=[pl.BlockSpec((block_rows, dh), lambda i, be, bv: (i, 0)),
                  pl.BlockSpec((None, None, d, 2 * D_EXPERT), lambda i, be, bv: (0, be[i], 0, 0)),
                  pl.BlockSpec((None, 1, 2 * D_EXPERT), lambda i, be, bv: (be[i], 0, 0)),
                  pl.BlockSpec((None, None, D_EXPERT, d), lambda i, be, bv: (0, be[i], 0, 0)),
                  pl.BlockSpec((None, 1, d), lambda i, be, bv: (be[i], 0, 0))],
        out_specs=pl.BlockSpec((block_rows, dh), lambda i, be, bv: (i, 0)),
        scratch_shapes=[pltpu.VMEM((d, 2 * D_EXPERT), BF16), pltpu.VMEM((D_EXPERT, d), BF16)],
    )
    return pl.pallas_call(
        _moe_kernel,
        grid_spec=grid_spec,
        out_shape=jax.ShapeDtypeStruct((n_slots, dh), jnp.int32),
        compiler_params=pltpu.CompilerParams(dimension_semantics=("arbitrary",),
                                             vmem_limit_bytes=VMEM_LIMIT),
        name="moe_experts",
    )(block_e, block_valid, slots, p['w_gate_up'], p['b_gate_up'], p['w_down'], p['b_down'])


def _combine_kernel(h_ref, gate_ref, y_ref, gfin_ref, out_ref):
    half = y_ref.shape[2]
    acc_hi = h_ref[:, :half]
    acc_lo = h_ref[:, half:]
    gates = gate_ref[...].T
    for j in range(TOP_K):
        y_hi, y_lo = _unpack_bf16_pairs(y_ref[j])
        acc_hi = acc_hi + gates[:, j:j + 1] * y_hi
        acc_lo = acc_lo + gates[:, j:j + 1] * y_lo
    out_ref[...] = _rmsnorm(jnp.concatenate([acc_hi, acc_lo], axis=1), gfin_ref[...])


def _combine(h, gates, y_tok, first_token, g_final, tm):
    n, d = h.shape
    first_block = first_token // tm
    assert first_token % tm == 0
    return pl.pallas_call(
        _combine_kernel,
        grid=(n // tm,),
        in_specs=[pl.BlockSpec((tm, d), lambda i: (i, 0)),
                  pl.BlockSpec((2 * TOP_K, tm), lambda i: (0, i)),
                  pl.BlockSpec((TOP_K, tm, d // 2), lambda i: (0, i + first_block, 0)),
                  _const_spec(g_final.shape)],
        out_specs=pl.BlockSpec((tm, d), lambda i: (i, 0)),
        out_shape=jax.ShapeDtypeStruct((n, d), F32),
        compiler_params=pltpu.CompilerParams(dimension_semantics=("arbitrary",),
                                             vmem_limit_bytes=VMEM_LIMIT),
        name="moe_combine",
    )(h, gates, y_tok, g_final)


def _moe_outputs(xn, meta, counts, p):
    n, d = xn.shape
    n_rows = n * TOP_K
    block_rows = max(MOE_ROWS_MIN, min(MOE_ROWS_MAX, n_rows // N_EXPERTS))
    n_blocks = n_rows // block_rows + N_EXPERTS
    n_slots = n_blocks * block_rows
    cnt = counts[:, 0].astype(jnp.int32)
    padded = (cnt + block_rows - 1) // block_rows * block_rows
    pad_end = jnp.cumsum(padded)
    pad_start = pad_end - padded
    block_row0 = jnp.arange(n_blocks, dtype=jnp.int32) * block_rows
    block_e = jnp.minimum(jnp.sum((pad_end[None, :] <= block_row0[:, None]).astype(jnp.int32), axis=1),
                          N_EXPERTS - 1)
    used_end = jnp.sum(jnp.where(block_e[:, None] == jnp.arange(N_EXPERTS)[None, :], (pad_start + cnt)[None, :], 0),
                       axis=1)
    block_valid = jnp.clip(used_end - block_row0, 0, block_rows)
    expert = meta[:TOP_K]
    start = jnp.zeros_like(expert)
    for e in range(N_EXPERTS):
        start = jnp.where(expert == e, pad_start[e], start)
    dest = (start + meta[TOP_K:]).astype(jnp.int32)
    slots = _dispatch(xn, dest, n_slots)
    y_slots = _moe_experts(slots, block_e, block_valid, p)
    return _gather_rows(y_slots, dest.reshape(-1)).reshape(TOP_K, n, d)


def _prepare(g_mix, w_in, conv_w, conv_b, dt_bias, a_log, d_skip, g_ssm, w_ssm_branch, lb_logits, g_hgrn,
             w_hgrn_branch, w_out, g_mem, w_xk, w_xv, g_xattn, w_xq, w_xo, g_moe, w_router, b_router,
             w_gate_up, b_gate_up, w_down, b_down):
    d = D_MODEL
    w = w_in[0]
    o_z, o_xbc = 0, SSM_INNER
    o_dt = o_xbc + CONV_CH
    o_q = o_dt + SSM_HEADS
    o_ga = o_q + 4 * d
    o_gb = o_ga + d

    def row(v):
        return v.reshape(1, -1).astype(F32)

    def lane_pad(v, fill=0.0):
        return jnp.pad(v.reshape(1, -1).astype(F32), ((0, 0), (0, LANES - v.shape[-1])), constant_values=fill)

    head_of_col = np.arange(SSM_INNER) // SSM_HEAD_DIM
    expand = (np.arange(LANES)[:, None] == head_of_col[None, :])
    return dict(
        g_mix=row(g_mix[0]),
        w_z=w[:, o_z:o_z + SSM_INNER].astype(BF16),
        w_xbc=w[:, o_xbc:o_xbc + CONV_CH].astype(BF16),
        w_dt=jnp.pad(w[:, o_dt:o_dt + SSM_HEADS], ((0, 0), (0, LANES - SSM_HEADS))).astype(BF16),
        w_hg=w[:, o_q:o_q + 4 * d].astype(BF16),
        w_ga=w[:, o_ga:o_ga + d].astype(BF16),
        w_gb=w[:, o_gb:o_gb + d].astype(BF16),
        conv_w=conv_w[0].astype(F32),
        conv_b=row(conv_b[0]),
        dt_bias=lane_pad(dt_bias[0]),
        a_log=lane_pad(a_log[0]),
        d_skip=row(jnp.repeat(d_skip[0], SSM_HEAD_DIM)),
        g_ssm=row(g_ssm[0]),
        w_ssm_branch=w_ssm_branch[0].astype(BF16),
        lb_logits=lb_logits.astype(F32),
        g_hgrn=row(g_hgrn[0]),
        w_hgrn_branch=w_hgrn_branch[0].astype(BF16),
        w_out=w_out[0].astype(BF16),
        g_mem=row(g_mem[0]),
        w_xk=w_xk[0].astype(BF16),
        w_xv=w_xv[0].astype(BF16),
        g_xattn=row(g_xattn[0]),
        w_xq=w_xq[0].astype(BF16),
        w_xo=w_xo[0].astype(BF16),
        g_moe=row(g_moe[0]),
        w_router=jnp.pad(w_router[0], ((0, 0), (0, LANES - N_EXPERTS))).astype(BF16),
        b_router=lane_pad(b_router[0], fill=-jnp.inf),
        w_gate_up=w_gate_up.astype(F32),
        b_gate_up=b_gate_up[0].reshape(N_EXPERTS, 1, 2 * D_EXPERT).astype(F32),
        w_down=w_down.astype(F32),
        b_down=b_down[0].reshape(N_EXPERTS, 1, d).astype(F32),
        expand=jnp.asarray(expand, BF16),
    )


def _group_to_router(x, conv0, ssm0, hg0, mk, mv, cnt0, p, n_all, first_token, xn_all):
    b, t, d = x.shape
    ma, conv_n, ssm_n = _ssd_mixer(x, conv0, ssm0, p, min(SSD_TILE, t))
    mb, hg_n = _hgrn_mixer(x, hg0, p, min(HGRN_TILE, t))
    h2, xn_all, meta, gates, counts = _post_mixer(x, ma, mb, mk, mv, cnt0, p, min(POST_TILE, t), n_all,
                                                  first_token, xn_all)
    n = b * t
    meta = meta.transpose(1, 0, 2).reshape(2 * TOP_K, n)
    gates = gates.transpose(1, 0, 2).reshape(2 * TOP_K, n)
    return (h2.reshape(n, d), meta, gates), xn_all, counts, (conv_n, ssm_n, hg_n)


def kernel(x_prompt, x_sample, mem_prompt, state_conv, state_ssm, state_hgrn, cache_mem_k, cache_mem_v, g_mix, w_in, conv_w, conv_b, dt_bias, a_log, d_skip, g_ssm, w_ssm_branch, lb_logits, g_hgrn, w_hgrn_branch, w_out, g_mem, w_xk, w_xv, g_xattn, w_xq, w_xo, g_moe, w_router, b_router, w_gate_up, b_gate_up, w_down, b_down, g_final):
    p = _prepare(g_mix, w_in, conv_w, conv_b, dt_bias, a_log, d_skip, g_ssm, w_ssm_branch, lb_logits, g_hgrn,
                 w_hgrn_branch, w_out, g_mem, w_xk, w_xv, g_xattn, w_xq, w_xo, g_moe, w_router, b_router,
                 w_gate_up, b_gate_up, w_down, b_down)
    g_fin = g_final.reshape(1, -1).astype(F32)
    if w_in.shape[0] != 1:
        raise NotImplementedError("a single layer (DEPTH == 1) is implemented")
    bp = x_prompt.shape[0]
    bs = x_sample.shape[0]
    d = D_MODEL

    mk_p, mv_p = _memory_kv(mem_prompt, p)
    n_p = bp * x_prompt.shape[1]
    n_s = bs * x_sample.shape[1]
    (h_p, meta_p, gates_p), xn_all, counts, (conv_p, ssm_p, hg_p) = _group_to_router(
        x_prompt,
        jnp.zeros((1, bp, CONV_W - 1, CONV_CH), F32),
        jnp.zeros((1, bp, SSM_HEADS, SSM_HEAD_DIM, SSM_STATE), F32),
        jnp.zeros((1, bp, HG_HEADS, HG_K, HG_K), F32),
        mk_p, mv_p, jnp.zeros((N_EXPERTS, LANES), F32), p, n_p + n_s, 0, None)
    (h_s, meta_s, gates_s), xn_all, counts, (conv_s, ssm_s, hg_s) = _group_to_router(
        x_sample, state_conv, state_ssm, state_hgrn,
        cache_mem_k[0].reshape(bs, N_MEM, d), cache_mem_v[0].reshape(bs, N_MEM, d), counts, p, n_p + n_s, n_p,
        xn_all)
    y_tok = _moe_outputs(xn_all, jnp.concatenate([meta_p, meta_s], axis=1), counts, p)
    y_p = _combine(h_p, gates_p, y_tok, 0, g_fin, min(COMBINE_TILE, n_p)).reshape(x_prompt.shape)
    y_s = _combine(h_s, gates_s, y_tok, n_p, g_fin, min(COMBINE_TILE, n_s)).reshape(x_sample.shape)

    kv_shape = (1, bp, N_MEM, X_HEADS, X_HEAD_DIM)
    return (y_p, y_s,
            conv_p[None], ssm_p[None], hg_p[None], mk_p.reshape(kv_shape), mv_p.reshape(kv_shape),
            conv_s[None], ssm_s[None], hg_s[None])
```

```python
import jax
import jax.numpy as jnp
import numpy as np
from jax import lax
from jax.experimental import pallas as pl
from jax.experimental.pallas import tpu as pltpu
from jax.experimental.pallas import tpu_sc as plsc

F32 = jnp.float32
BF16 = jnp.bfloat16

D_MODEL = 1024
CHUNK = 64
EPS = 1e-6
SSM_INNER = 2 * D_MODEL
SSM_HEAD_DIM = 64
SSM_HEADS = SSM_INNER // SSM_HEAD_DIM
SSM_GROUPS = 4
SSM_STATE = 128
GROUP_COLS = SSM_INNER // SSM_GROUPS
CONV_W = 4
BC_COLS = SSM_GROUPS * SSM_STATE
CONV_CH = SSM_INNER + 2 * BC_COLS
HG_HEADS = 8
HG_K = D_MODEL // HG_HEADS
HG_SUB = CHUNK // 2
N_MEM = 256
X_HEADS = 4
X_HEAD_DIM = D_MODEL // X_HEADS
N_EXPERTS = 32
TOP_K = 4
D_EXPERT = D_MODEL
SWIGLU_LIMIT = 7.0
SWIGLU_ALPHA = 1.702

LANES = 128
CONV_PAD = 8
MOE_ROWS_MAX = 512
MOE_ROWS_MIN = 128
SSD_TILE = 256
HGRN_TILE = 256
POST_TILE = 512
COMBINE_TILE = 512
VMEM_LIMIT = 56 * 1024 * 1024
SC_CORES = 2
SC_SUBCORES = 16
SC_WORKERS = SC_CORES * SC_SUBCORES
SC_CHUNK = 104
SC_BUFFERS = 2


def _const_spec(shape):
    nd = len(shape)
    return pl.BlockSpec(shape, lambda *_: (0,) * nd, pipeline_mode=pl.Buffered(1))


def _dot(a, b):
    return jnp.dot(a, b, preferred_element_type=F32)


def _dot_nt(a, b):
    return lax.dot_general(a, b, (((1,), (1,)), ((), ())), preferred_element_type=F32)


def _dot_tn(a, b):
    return lax.dot_general(a, b, (((0,), (0,)), ((), ())), preferred_element_type=F32)


def _split3(a):
    hi = a.astype(BF16)
    r1 = a - hi.astype(F32)
    mid = r1.astype(BF16)
    lo = (r1 - mid.astype(F32)).astype(BF16)
    return hi, mid, lo


def _exact_dot_lhs01(sel, a):
    hi, mid, lo = _split3(a)
    return _dot(sel, hi) + _dot(sel, mid) + _dot(sel, lo)


def _rmsnorm(x, g):
    return x * lax.rsqrt(jnp.mean(x * x, axis=-1, keepdims=True) + EPS) * g


def _sigmoid(x):
    return jax.nn.sigmoid(x)


def _silu(x):
    return x * jax.nn.sigmoid(x)


def _softplus(x):
    return jnp.maximum(x, 0.0) + jnp.log1p(jnp.exp(-jnp.abs(x)))


def _pack_bf16_pairs(x):
    c = x.shape[1] // 2
    hi = lax.bitcast_convert_type(x[:, :c].astype(BF16).astype(F32), jnp.int32)
    lo = lax.bitcast_convert_type(x[:, c:].astype(BF16).astype(F32), jnp.int32)
    return hi | lax.shift_right_logical(lo, 16)


def _unpack_bf16_pairs(u):
    hi = lax.bitcast_convert_type(u & jnp.int32(-65536), F32)
    lo = lax.bitcast_convert_type(lax.shift_left(u, 16), F32)
    return hi, lo


def _ssd_kernel(x_ref, conv0_ref, ssm0_ref, g_ref, wxbc_ref, wz_ref, wdt_ref, wga_ref, convw_ref,
                convb_ref, dtb_ref, alog_ref, dskip_ref, gssm_ref, wbr_ref, tri_ref, expand_ref,
                ma_ref, convn_ref, ssmn_ref,
                xpad, xact, xdt, ysc, st):
    tt = x_ref.shape[0]
    ti = pl.program_id(1)
    nt = pl.num_programs(1)

    @pl.when(ti == 0)
    def _():
        xpad[CONV_PAD - (CONV_W - 1):CONV_PAD, :] = conv0_ref[...]
        st[...] = ssm0_ref[...].reshape(SSM_INNER, SSM_STATE).T

    xn = _rmsnorm(x_ref[...], g_ref[...]).astype(BF16)
    xpad[CONV_PAD:CONV_PAD + tt, :] = _dot(xn, wxbc_ref[...])

    conv = convb_ref[...] + xpad[CONV_PAD:CONV_PAD + tt, :] * convw_ref[CONV_W - 1:CONV_W, :]
    for k in range(1, CONV_W):
        conv = conv + xpad[CONV_PAD - k:CONV_PAD - k + tt, :] * convw_ref[CONV_W - 1 - k:CONV_W - k, :]
    xact[...] = _silu(conv)
    tail = xpad[CONV_PAD + tt - (CONV_W - 1):CONV_PAD + tt, :]
    xpad[CONV_PAD - (CONV_W - 1):CONV_PAD, :] = tail

    dt = _softplus(_dot(xn, wdt_ref[...]) + dtb_ref[...])
    acum_all = _exact_dot_lhs01(tri_ref[...], dt * -jnp.exp(alog_ref[...]))
    dt_hi = dt.astype(BF16)
    dt_lo = (dt - dt_hi.astype(F32)).astype(BF16)
    dt_x = _dot(dt_hi, expand_ref[...]) + _dot(dt_lo, expand_ref[...])
    xdt[...] = xact[:, :SSM_INNER] * dt_x

    lane = lax.broadcasted_iota(jnp.int32, (CHUNK, LANES), 1)
    row = lax.broadcasted_iota(jnp.int32, (CHUNK, LANES), 0)
    causal2 = row >= (lane % CHUNK)
    diag2 = row == (lane % CHUNK)
    left = lane < CHUNK
    pair_of_lane = lane // CHUNK

    for c in range(tt // CHUNK):
        rows = slice(c * CHUNK, (c + 1) * CHUNK)
        acum = acum_all[rows, :]
        for g in range(SSM_GROUPS):
            gs = slice(g * GROUP_COLS, (g + 1) * GROUP_COLS)
            b_g = xact[rows, SSM_INNER + g * SSM_STATE:SSM_INNER + (g + 1) * SSM_STATE].astype(BF16)
            c_g = xact[rows, SSM_INNER + BC_COLS + g * SSM_STATE:
                       SSM_INNER + BC_COLS + (g + 1) * SSM_STATE].astype(BF16)
            cb2 = _dot_nt(c_g, jnp.concatenate([b_g, b_g], axis=0))
            st_g = st[:, gs]
            y_inter = _dot(c_g, st_g.astype(BF16))
            xw, decay = [], []
            for j in range(GROUP_COLS // LANES):
                pair = g * (GROUP_COLS // LANES) + j
                ps = slice(pair * LANES, (pair + 1) * LANES)
                a_col = jnp.take_along_axis(acum, pair_of_lane + 2 * pair, axis=1)
                a_row = jnp.sum(jnp.where(diag2, a_col, 0.0), axis=0, keepdims=True)
                a_last = a_col[CHUNK - 1:CHUNK, :]
                dec = jnp.where(causal2, jnp.exp(jnp.minimum(a_col - a_row, 0.0)), 0.0)
                wts = (cb2 * dec).astype(BF16)
                xp = xdt[rows, ps]
                xbd = jnp.concatenate([jnp.where(left, xp, 0.0), jnp.where(left, 0.0, xp)],
                                      axis=0).astype(BF16)
                ysc[rows, ps] = _dot(wts, xbd) + y_inter[:, j * LANES:(j + 1) * LANES] * jnp.exp(a_col)
                xw.append((jnp.exp(a_last - a_col) * xp).astype(BF16))
                decay.append(jnp.exp(a_last))
            st[:, gs] = jnp.concatenate(decay, axis=1) * st_g + _dot_tn(b_g, jnp.concatenate(xw, axis=1))

    xs = xact[:, :SSM_INNER]
    y = ysc[...] + dskip_ref[...] * xs
    yz = y * _silu(_dot(xn, wz_ref[...]))
    parts = []
    for g in range(SSM_GROUPS):
        blk = yz[:, g * GROUP_COLS:(g + 1) * GROUP_COLS]
        parts.append(blk * lax.rsqrt(jnp.mean(blk * blk, axis=-1, keepdims=True) + EPS))
    yn = (jnp.concatenate(parts, axis=1) * gssm_ref[...]).astype(BF16)
    ya = _dot(yn, wbr_ref[...])
    ma_ref[...] = _sigmoid(_dot(xn, wga_ref[...])) * ya

    @pl.when(ti == nt - 1)
    def _():
        convn_ref[...] = tail
        ssmn_ref[...] = st[...].T.reshape(SSM_HEADS, SSM_HEAD_DIM, SSM_STATE)


def _chunk_tri(tt):
    return jnp.asarray(np.kron(np.eye(tt // CHUNK), np.tril(np.ones((CHUNK, CHUNK)))), BF16)


def _ssd_mixer(x, conv0, ssm0, p, tt):
    b, t, d = x.shape
    grid = (b, t // tt)
    row_spec = pl.BlockSpec((None, tt, d), lambda i, j: (i, j, 0))
    consts = [p['g_mix'], p['w_xbc'], p['w_z'], p['w_dt'], p['w_ga'], p['conv_w'], p['conv_b'], p['dt_bias'],
              p['a_log'], p['d_skip'], p['g_ssm'], p['w_ssm_branch'], _chunk_tri(tt), p['expand']]
    return pl.pallas_call(
        _ssd_kernel,
        grid=grid,
        in_specs=[row_spec,
                  pl.BlockSpec((None, None, CONV_W - 1, CONV_CH), lambda i, j: (0, i, 0, 0)),
                  pl.BlockSpec((None, None, SSM_HEADS, SSM_HEAD_DIM, SSM_STATE), lambda i, j: (0, i, 0, 0, 0))]
                 + [_const_spec(c.shape) for c in consts],
        out_specs=[row_spec,
                   pl.BlockSpec((None, CONV_W - 1, CONV_CH), lambda i, j: (i, 0, 0)),
                   pl.BlockSpec((None, SSM_HEADS, SSM_HEAD_DIM, SSM_STATE), lambda i, j: (i, 0, 0, 0))],
        out_shape=[jax.ShapeDtypeStruct((b, t, d), F32),
                   jax.ShapeDtypeStruct((b, CONV_W - 1, CONV_CH), F32),
                   jax.ShapeDtypeStruct((b, SSM_HEADS, SSM_HEAD_DIM, SSM_STATE), F32)],
        scratch_shapes=[pltpu.VMEM((CONV_PAD + tt, CONV_CH), F32),
                        pltpu.VMEM((tt, CONV_CH), F32),
                        pltpu.VMEM((tt, SSM_INNER), F32),
                        pltpu.VMEM((tt, SSM_INNER), F32),
                        pltpu.VMEM((SSM_STATE, SSM_INNER), F32)],
        compiler_params=pltpu.CompilerParams(dimension_semantics=("arbitrary", "arbitrary"),
                                             vmem_limit_bytes=VMEM_LIMIT),
        name="ssd_mixer",
    )(x, conv0, ssm0, *consts)


def _hgrn_kernel(x_ref, hg0_ref, g_ref, wh_ref, wgb_ref, lbl_ref, ghg_ref, wbr_ref, tri_ref,
                 mb_ref, hgn_ref,
                 osc, st):
    tt = x_ref.shape[0]
    n_chunks = tt // CHUNK
    ti = pl.program_id(1)
    nt = pl.num_programs(1)

    @pl.when(ti == 0)
    def _():
        for h in range(HG_HEADS):
            st[h] = hg0_ref[h].T

    xn = _rmsnorm(x_ref[...], g_ref[...]).astype(BF16)
    proj = _dot(xn, wh_ref[...])
    l0 = lbl_ref[0:1, :]
    l1 = lbl_ref[1:2, :]
    lmax = jnp.maximum(l0, l1)
    e0 = jnp.exp(l0 - lmax)
    lb = e0 / (e0 + jnp.exp(l1 - lmax))
    fr = proj[:, D_MODEL:2 * D_MODEL]
    kk = (1.0 - lb) * _sigmoid(-fr)
    v = _silu(proj[:, 2 * D_MODEL:3 * D_MODEL]).astype(BF16)
    gc = _exact_dot_lhs01(tri_ref[...], jnp.log(lb + (1.0 - lb) * _sigmoid(fr)))
    q = proj[:, :D_MODEL]
    qh = (q * jnp.exp(gc)).astype(BF16)
    mid = []
    for m in range(tt // HG_SUB):
        r = m * HG_SUB + HG_SUB // 2 - 1
        mid.append(jnp.broadcast_to(gc[r:r + 1, :], (HG_SUB, D_MODEL)))
    d_mid = gc - jnp.concatenate(mid, axis=0)
    q_sub = (q * jnp.exp(d_mid)).astype(BF16)
    k_sub = (kk * jnp.exp(-d_mid)).astype(BF16)
    q_far, k_far, v_first = [], [], []
    for c in range(n_chunks):
        first = slice(c * CHUNK, c * CHUNK + HG_SUB)
        second = slice(c * CHUNK + HG_SUB, (c + 1) * CHUNK)
        g_half = gc[c * CHUNK + HG_SUB - 1:c * CHUNK + HG_SUB, :]
        q_far.append(q[second, :] * jnp.exp(gc[second, :] - g_half))
        k_far.append(kk[first, :] * jnp.exp(g_half - gc[first, :]))
        v_first.append(v[first, :])
    q_far = jnp.concatenate(q_far, axis=0).astype(BF16)
    k_far = jnp.concatenate(k_far, axis=0).astype(BF16)
    v_first = jnp.concatenate(v_first, axis=0)
    kt, decay = [], []
    for c in range(n_chunks):
        rows = slice(c * CHUNK, (c + 1) * CHUNK)
        glast = gc[(c + 1) * CHUNK - 1:(c + 1) * CHUNK, :]
        kt.append((kk[rows, :] * jnp.exp(glast - gc[rows, :])).astype(BF16))
        decay.append(jnp.exp(glast))

    row = lax.broadcasted_iota(jnp.int32, (tt, tt), 0)
    col = lax.broadcasted_iota(jnp.int32, (tt, tt), 1)
    near = jnp.logical_and(row >= col, row // HG_SUB == col // HG_SUB)
    same_chunk = (row // HG_SUB == col // HG_SUB)[:tt // 2, :tt // 2]
    no_far = jnp.zeros((HG_SUB, HG_K), F32)

    for h in range(HG_HEADS):
        hs = slice(h * HG_K, (h + 1) * HG_K)
        att = jnp.where(near, _dot_nt(q_sub[:, hs], k_sub[:, hs]), 0.0).astype(BF16)
        att_far = jnp.where(same_chunk, _dot_nt(q_far[:, hs], k_far[:, hs]), 0.0).astype(BF16)
        o_far = _dot(att_far, v_first[:, hs])
        far_rows = []
        for c in range(n_chunks):
            far_rows += [no_far, o_far[c * HG_SUB:(c + 1) * HG_SUB, :]]
        o_intra = _dot(att, v[:, hs]) + jnp.concatenate(far_rows, axis=0)
        s = st[h]
        o_inter = []
        for c in range(n_chunks):
            rows = slice(c * CHUNK, (c + 1) * CHUNK)
            o_inter.append(_dot_nt(qh[rows, hs], s.astype(BF16)))
            s = decay[c][:, hs] * s + _dot_tn(v[rows, hs], kt[c][:, hs])
        st[h] = s
        osc[:, hs] = o_intra + jnp.concatenate(o_inter, axis=0)

    parts = []
    for h in range(HG_HEADS):
        blk = osc[:, h * HG_K:(h + 1) * HG_K]
        parts.append(blk * lax.rsqrt(jnp.mean(blk * blk, axis=-1, keepdims=True) + EPS))
    on = (jnp.concatenate(parts, axis=1) * ghg_ref[...] * _silu(proj[:, 3 * D_MODEL:])).astype(BF16)
    yb = _dot(on, wbr_ref[...])
    mb_ref[...] = _sigmoid(_dot(xn, wgb_ref[...])) * yb

    @pl.when(ti == nt - 1)
    def _():
        for h in range(HG_HEADS):
            hgn_ref[h] = st[h].T


def _hgrn_mixer(x, hg0, p, tt):
    b, t, d = x.shape
    grid = (b, t // tt)
    row_spec = pl.BlockSpec((None, tt, d), lambda i, j: (i, j, 0))
    st_spec = pl.BlockSpec((None, HG_HEADS, HG_K, HG_K), lambda i, j: (i, 0, 0, 0))
    consts = [p['g_mix'], p['w_hg'], p['w_gb'], p['lb_logits'], p['g_hgrn'], p['w_hgrn_branch'], _chunk_tri(tt)]
    return pl.pallas_call(
        _hgrn_kernel,
        grid=grid,
        in_specs=[row_spec, pl.BlockSpec((None, None, HG_HEADS, HG_K, HG_K), lambda i, j: (0, i, 0, 0, 0))]
                 + [_const_spec(c.shape) for c in consts],
        out_specs=[row_spec, st_spec],
        out_shape=[jax.ShapeDtypeStruct((b, t, d), F32),
                   jax.ShapeDtypeStruct((b, HG_HEADS, HG_K, HG_K), F32)],
        scratch_shapes=[pltpu.VMEM((tt, d), F32), pltpu.VMEM((HG_HEADS, HG_K, HG_K), F32)],
        compiler_params=pltpu.CompilerParams(dimension_semantics=("arbitrary", "arbitrary"),
                                             vmem_limit_bytes=VMEM_LIMIT),
        name="hgrn_mixer",
    )(x, hg0, *consts)


def _memkv_kernel(m_ref, g_ref, wk_ref, wv_ref, k_ref, v_ref):
    mn = _rmsnorm(m_ref[...], g_ref[...]).astype(BF16)
    k = _dot(mn, wk_ref[...])
    v = _dot(mn, wv_ref[...])
    for hh in range(X_HEADS):
        hs = slice(hh * X_HEAD_DIM, (hh + 1) * X_HEAD_DIM)
        k_ref[:, hh, :] = k[:, hs]
        v_ref[:, hh, :] = v[:, hs]


def _memory_kv(mem, p):
    b, n, d = mem.shape
    spec = pl.BlockSpec((None, n, d), lambda i: (i, 0, 0))
    kv_spec = pl.BlockSpec((None, None, n, X_HEADS, X_HEAD_DIM), lambda i: (0, i, 0, 0, 0))
    consts = [p['g_mem'], p['w_xk'], p['w_xv']]
    return pl.pallas_call(
        _memkv_kernel,
        grid=(b,),
        in_specs=[spec] + [_const_spec(c.shape) for c in consts],
        out_specs=[kv_spec, kv_spec],
        out_shape=[jax.ShapeDtypeStruct((1, b, n, X_HEADS, X_HEAD_DIM), F32)] * 2,
        compiler_params=pltpu.CompilerParams(dimension_semantics=("arbitrary",),
                                             vmem_limit_bytes=VMEM_LIMIT),
        name="memory_kv",
    )(mem, *consts)


def _post_kernel(x_ref, ma_ref, mb_ref, mk_ref, mv_ref, cnt0_ref, wo_ref, gx_ref, wq_ref, wxo_ref,
                 gmoe_ref, wr_ref, br_ref, triu_ref,
                 h_ref, xn_ref, meta_ref, gate_ref, cnt_ref,
                 base):
    tm = x_ref.shape[0]
    first = jnp.logical_and(pl.program_id(0) == 0, pl.program_id(1) == 0)

    @pl.when(first)
    def _():
        base[...] = cnt0_ref[...]

    m = (ma_ref[...] + mb_ref[...]).astype(BF16)
    h1 = x_ref[...] + _dot(m, wo_ref[...])

    hn = _rmsnorm(h1, gx_ref[...]).astype(BF16)
    q = _dot(hn, wq_ref[...])
    heads = []
    for hh in range(X_HEADS):
        hs = slice(hh * X_HEAD_DIM, (hh + 1) * X_HEAD_DIM)
        s = _dot_nt(q[:, hs].astype(BF16), mk_ref[:, hh, :].astype(BF16)) * (X_HEAD_DIM ** -0.5)
        s = s - jnp.max(s, axis=-1, keepdims=True)
        e = jnp.exp(s)
        pr = e / jnp.sum(e, axis=-1, keepdims=True)
        heads.append(_dot(pr.astype(BF16), mv_ref[:, hh, :].astype(BF16)))
    o = jnp.concatenate(heads, axis=1).astype(BF16)
    h2 = h1 + _dot(o, wxo_ref[...])
    h_ref[...] = h2

    xn3 = _rmsnorm(h2, gmoe_ref[...])
    xn_ref[...] = _pack_bf16_pairs(xn3)
    logits = _dot(xn3.astype(BF16), wr_ref[...]) + br_ref[...]

    run = logits.T[:N_EXPERTS, :]
    eid = lax.broadcasted_iota(jnp.int32, (N_EXPERTS, tm), 0).astype(F32)
    vals, ids, hots = [], [], []
    for _ in range(TOP_K):
        mx = jnp.max(run, axis=0, keepdims=True)
        idx = jnp.min(jnp.where(run == mx, eid, float(N_EXPERTS)), axis=0, keepdims=True)
        hot = eid == idx
        run = jnp.where(hot, -jnp.inf, run)
        vals.append(mx)
        ids.append(idx)
        hots.append(hot)
    es = [jnp.exp(v - vals[0]) for v in vals]
    den = es[0] + es[1] + es[2] + es[3]

    tot = jnp.zeros((N_EXPERTS, tm), F32)
    for hot in hots:
        tot = tot + hot.astype(F32)
    before = base[:, :1] + _dot(tot.astype(BF16), triu_ref[...])
    base[...] = base[...] + jnp.sum(tot, axis=1, keepdims=True)
    cnt_ref[...] = base[...]

    ranks = [jnp.sum(jnp.where(hot, before, 0.0), axis=0, keepdims=True) for hot in hots]
    meta_ref[...] = jnp.concatenate(ids + ranks, axis=0).astype(jnp.int32)
    gate_ref[...] = jnp.concatenate([e / den for e in es] + [jnp.zeros_like(den)] * TOP_K, axis=0)


def _post_kernel_into(xn_all_ref, *refs):
    del xn_all_ref
    _post_kernel(*refs)


def _post_mixer(x, ma, mb, mk, mv, cnt0, p, tm, n_all, first_token, xn_all=None):
    b, t, d = x.shape
    grid = (b, t // tm)
    first_block = first_token // tm
    assert first_token % tm == 0
    row_spec = pl.BlockSpec((None, tm, d), lambda i, j: (i, j, 0))
    small_spec = pl.BlockSpec((None, 2 * TOP_K, tm), lambda i, j: (i, 0, j))
    mem_spec = pl.BlockSpec((None, None, N_MEM, X_HEADS, X_HEAD_DIM), lambda i, j: (0, i, 0, 0, 0))
    cnt_spec = pl.BlockSpec((N_EXPERTS, LANES), lambda i, j: (0, 0))
    xn_spec = pl.BlockSpec((tm, d // 2), lambda i, j: (first_block + i * (t // tm) + j, 0))
    triu = jnp.triu(jnp.ones((tm, tm), BF16), 1)
    consts = [p['w_out'], p['g_xattn'], p['w_xq'], p['w_xo'], p['g_moe'], p['w_router'], p['b_router'], triu]
    in_specs = [row_spec, row_spec, row_spec, mem_spec, mem_spec, cnt_spec] + [_const_spec(c.shape) for c in consts]
    args = (x, ma, mb, mk, mv, cnt0, *consts)
    if xn_all is not None:
        in_specs = [pl.BlockSpec(memory_space=pl.ANY)] + in_specs
        args = (xn_all,) + args
    return pl.pallas_call(
        _post_kernel if xn_all is None else _post_kernel_into,
        grid=grid,
        in_specs=in_specs,
        out_specs=[row_spec, xn_spec, small_spec, small_spec, cnt_spec],
        out_shape=[jax.ShapeDtypeStruct((b, t, d), F32),
                   jax.ShapeDtypeStruct((n_all, d // 2), jnp.int32),
                   jax.ShapeDtypeStruct((b, 2 * TOP_K, t), jnp.int32),
                   jax.ShapeDtypeStruct((b, 2 * TOP_K, t), F32),
                   jax.ShapeDtypeStruct((N_EXPERTS, LANES), F32)],
        scratch_shapes=[pltpu.VMEM((N_EXPERTS, LANES), F32)],
        input_output_aliases={} if xn_all is None else {0: 1},
        compiler_params=pltpu.CompilerParams(dimension_semantics=("arbitrary", "arbitrary"),
                                             vmem_limit_bytes=VMEM_LIMIT),
        name="post_mixer",
    )(*args)


def _sc_chunk(rows_per_worker):
    for chunk in range(SC_CHUNK, 0, -8):
        if rows_per_worker % (SC_BUFFERS * chunk) == 0:
            return chunk
    raise ValueError(f"no SparseCore chunk size for {rows_per_worker} rows per worker")


def _sc_mesh():
    return plsc.VectorSubcoreMesh(core_axis_name="c", subcore_axis_name="s")


def _sc_worker():
    return lax.axis_index("s") * SC_CORES + lax.axis_index("c")


def _dispatch(xn, dest, n_slots):
    n, d = xn.shape
    per_w = n // SC_WORKERS
    chunk = _sc_chunk(per_w)
    n_chunks = per_w // chunk
    idx = dest.reshape(TOP_K, n // chunk, chunk).transpose(1, 0, 2)

    def body(x_hbm, idx_hbm, out_hbm, idx_v, rows_v, rsem, wsem):
        wid = _sc_worker()

        def read(i, b):
            blk = wid * n_chunks + i
            return pltpu.make_async_copy(x_hbm.at[pl.ds(blk * chunk, chunk)], rows_v.at[b], rsem.at[b])

        def read_start(i, b):
            pltpu.sync_copy(idx_hbm.at[wid * n_chunks + i], idx_v.at[b])
            read(i, b).start()

        def write(b, j):
            return pltpu.make_async_copy(rows_v.at[b], out_hbm.at[idx_v.at[b, j]], wsem.at[b])

        for b in range(SC_BUFFERS):
            read_start(b, b)

        @pl.loop(0, n_chunks, step=SC_BUFFERS)
        def _(i0):
            for b in range(SC_BUFFERS):
                i = i0 + b
                read(i, b).wait()
                for j in range(TOP_K):
                    write(b, j).start()
                for j in range(TOP_K):
                    write(b, j).wait()

                @pl.when(i + SC_BUFFERS < n_chunks)
                def _():
                    read_start(i + SC_BUFFERS, b)

    return pl.kernel(
        body, mesh=_sc_mesh(),
        out_type=jax.ShapeDtypeStruct((n_slots, d), xn.dtype),
        scratch_types=[pltpu.VMEM((SC_BUFFERS, TOP_K, chunk), jnp.int32),
                       pltpu.VMEM((SC_BUFFERS, chunk, d), xn.dtype),
                       pltpu.SemaphoreType.DMA((SC_BUFFERS,)),
                       pltpu.SemaphoreType.DMA((SC_BUFFERS,))],
        name="moe_dispatch",
    )(xn, idx)


def _gather_rows(table, idx):
    n_out = idx.shape[0]
    d = table.shape[1]
    per_w = n_out // SC_WORKERS
    chunk = _sc_chunk(per_w)
    n_chunks = per_w // chunk

    def body(table_hbm, idx_hbm, out_hbm, idx_v, rows_v, gsem, wsem):
        base = _sc_worker() * per_w

        def gather(b):
            return pltpu.make_async_copy(table_hbm.at[idx_v.at[b]], rows_v.at[b], gsem.at[b])

        def gather_start(i, b):
            pltpu.sync_copy(idx_hbm.at[pl.ds(base + i * chunk, chunk)], idx_v.at[b])
            gather(b).start()

        def write(i, b):
            return pltpu.make_async_copy(rows_v.at[b], out_hbm.at[pl.ds(base + i * chunk, chunk)], wsem.at[b])

        for b in range(SC_BUFFERS):
            gather_start(b, b)

        @pl.loop(0, n_chunks, step=SC_BUFFERS)
        def _(i0):
            for b in range(SC_BUFFERS):
                i = i0 + b
                gather(b).wait()
                write(i, b).start()
                write(i, b).wait()

                @pl.when(i + SC_BUFFERS < n_chunks)
                def _():
                    gather_start(i + SC_BUFFERS, b)

    return pl.kernel(
        body, mesh=_sc_mesh(),
        out_type=jax.ShapeDtypeStruct((n_out, d), table.dtype),
        scratch_types=[pltpu.VMEM((SC_BUFFERS, chunk), jnp.int32),
                       pltpu.VMEM((SC_BUFFERS, chunk, d), table.dtype),
                       pltpu.SemaphoreType.DMA((SC_BUFFERS,)),
                       pltpu.SemaphoreType.DMA((SC_BUFFERS,))],
        name="moe_gather",
    )(table, idx)


def _moe_kernel(be_ref, valid_ref, x_ref, wgu_ref, bgu_ref, wd_ref, bd_ref, y_ref, wgu_bf, wd_bf):
    i = pl.program_id(0)
    valid = valid_ref[i]

    @pl.when(jnp.logical_or(i == 0, be_ref[i] != be_ref[jnp.maximum(i - 1, 0)]))
    def _():
        wgu_bf[...] = wgu_ref[...].astype(BF16)
        wd_bf[...] = wd_ref[...].astype(BF16)

    @pl.when(valid > 0)
    def _():
        row = lax.broadcasted_iota(jnp.int32, x_ref.shape, 0)
        x_hi, x_lo = _unpack_bf16_pairs(jnp.where(row < valid, x_ref[...], 0))
        half = x_ref.shape[1]
        gu = (_dot(x_hi.astype(BF16), wgu_bf[:half, :]) + _dot(x_lo.astype(BF16), wgu_bf[half:, :])
              + bgu_ref[...])
        gate = jnp.minimum(gu[:, :D_EXPERT], SWIGLU_LIMIT)
        up = jnp.clip(gu[:, D_EXPERT:], -SWIGLU_LIMIT, SWIGLU_LIMIT)
        hmid = ((up + 1.0) * gate * _sigmoid(SWIGLU_ALPHA * gate)).astype(BF16)
        y_ref[...] = _pack_bf16_pairs(_dot(hmid, wd_bf[...]) + bd_ref[...])


def _moe_experts(slots, block_e, block_valid, p):
    n_slots, dh = slots.shape
    d = 2 * dh
    n_blocks = block_e.shape[0]
    block_rows = n_slots // n_blocks
    grid_spec = pltpu.PrefetchScalarGridSpec(
        num_scalar_prefetch=2,
        grid=(n_blocks,),
        in_specs=[pl.BlockSpec((block_rows, dh), lambda i, be, bv: (i, 0)),
                  pl.BlockSpec((None, None, d, 2 * D_EXPERT), lambda i, be, bv: (0, be[i], 0, 0)),
                  pl.BlockSpec((None, 1, 2 * D_EXPERT), lambda i, be, bv: (be[i], 0, 0)),
                  pl.BlockSpec((None, None, D_EXPERT, d), lambda i, be, bv: (0, be[i], 0, 0)),
                  pl.BlockSpec((None, 1, d), lambda i, be, bv: (be[i], 0, 0))],
        out_specs=pl.BlockSpec((block_rows, dh), lambda i, be, bv: (i, 0)),
        scratch_shapes=[pltpu.VMEM((d, 2 * D_EXPERT), BF16), pltpu.VMEM((D_EXPERT, d), BF16)],
    )
    return pl.pallas_call(
        _moe_kernel,
        grid_spec=grid_spec,
        out_shape=jax.ShapeDtypeStruct((n_slots, dh), jnp.int32),
        compiler_params=pltpu.CompilerParams(dimension_semantics=("arbitrary",),
                                             vmem_limit_bytes=VMEM_LIMIT),
        name="moe_experts",
    )(block_e, block_valid, slots, p['w_gate_up'], p['b_gate_up'], p['w_down'], p['b_down'])


def _combine_kernel(h_ref, gate_ref, y_ref, gfin_ref, out_ref):
    half = y_ref.shape[2]
    acc_hi = h_ref[:, :half]
    acc_lo = h_ref[:, half:]
    gates = gate_ref[...].T
    for j in range(TOP_K):
        y_hi, y_lo = _unpack_bf16_pairs(y_ref[j])
        acc_hi = acc_hi + gates[:, j:j + 1] * y_hi
        acc_lo = acc_lo + gates[:, j:j + 1] * y_lo
    out_ref[...] = _rmsnorm(jnp.concatenate([acc_hi, acc_lo], axis=1), gfin_ref[...])


def _combine(h, gates, y_tok, first_token, g_final, tm):
    n, d = h.shape
    first_block = first_token // tm
    assert first_token % tm == 0
    return pl.pallas_call(
        _combine_kernel,
        grid=(n // tm,),
        in_specs=[pl.BlockSpec((tm, d), lambda i: (i, 0)),
                  pl.BlockSpec((2 * TOP_K, tm), lambda i: (0, i)),
                  pl.BlockSpec((TOP_K, tm, d // 2), lambda i: (0, i + first_block, 0)),
                  _const_spec(g_final.shape)],
        out_specs=pl.BlockSpec((tm, d), lambda i: (i, 0)),
        out_shape=jax.ShapeDtypeStruct((n, d), F32),
        compiler_params=pltpu.CompilerParams(dimension_semantics=("arbitrary",),
                                             vmem_limit_bytes=VMEM_LIMIT),
        name="moe_combine",
    )(h, gates, y_tok, g_final)


def _moe_outputs(xn, meta, counts, p):
    n, d = xn.shape
    n_rows = n * TOP_K
    block_rows = max(MOE_ROWS_MIN, min(MOE_ROWS_MAX, n_rows // N_EXPERTS))
    n_blocks = n_rows // block_rows + N_EXPERTS
    n_slots = n_blocks * block_rows
    cnt = counts[:, 0].astype(jnp.int32)
    padded = (cnt + block_rows - 1) // block_rows * block_rows
    pad_end = jnp.cumsum(padded)
    pad_start = pad_end - padded
    block_row0 = jnp.arange(n_blocks, dtype=jnp.int32) * block_rows
    block_e = jnp.minimum(jnp.sum((pad_end[None, :] <= block_row0[:, None]).astype(jnp.int32), axis=1),
                          N_EXPERTS - 1)
    used_end = jnp.sum(jnp.where(block_e[:, None] == jnp.arange(N_EXPERTS)[None, :], (pad_start + cnt)[None, :], 0),
                       axis=1)
    block_valid = jnp.clip(used_end - block_row0, 0, block_rows)
    expert = meta[:TOP_K]
    start = jnp.zeros_like(expert)
    for e in range(N_EXPERTS):
        start = jnp.where(expert == e, pad_start[e], start)
    dest = (start + meta[TOP_K:]).astype(jnp.int32)
    slots = _dispatch(xn, dest, n_slots)
    y_slots = _moe_experts(slots, block_e, block_valid, p)
    return _gather_rows(y_slots, dest.reshape(-1)).reshape(TOP_K, n, d)


def _prepare(g_mix, w_in, conv_w, conv_b, dt_bias, a_log, d_skip, g_ssm, w_ssm_branch, lb_logits, g_hgrn,
             w_hgrn_branch, w_out, g_mem, w_xk, w_xv, g_xattn, w_xq, w_xo, g_moe, w_router, b_router,
             w_gate_up, b_gate_up, w_down, b_down):
    d = D_MODEL
    w = w_in[0]
    o_z, o_xbc = 0, SSM_INNER
    o_dt = o_xbc + CONV_CH
    o_q = o_dt + SSM_HEADS
    o_ga = o_q + 4 * d
    o_gb = o_ga + d

    def row(v):
        return v.reshape(1, -1).astype(F32)

    def lane_pad(v, fill=0.0):
        return jnp.pad(v.reshape(1, -1).astype(F32), ((0, 0), (0, LANES - v.shape[-1])), constant_values=fill)

    head_of_col = np.arange(SSM_INNER) // SSM_HEAD_DIM
    expand = (np.arange(LANES)[:, None] == head_of_col[None, :])
    return dict(
        g_mix=row(g_mix[0]),
        w_z=w[:, o_z:o_z + SSM_INNER].astype(BF16),
        w_xbc=w[:, o_xbc:o_xbc + CONV_CH].astype(BF16),
        w_dt=jnp.pad(w[:, o_dt:o_dt + SSM_HEADS], ((0, 0), (0, LANES - SSM_HEADS))).astype(BF16),
        w_hg=w[:, o_q:o_q + 4 * d].astype(BF16),
        w_ga=w[:, o_ga:o_ga + d].astype(BF16),
        w_gb=w[:, o_gb:o_gb + d].astype(BF16),
        conv_w=conv_w[0].astype(F32),
        conv_b=row(conv_b[0]),
        dt_bias=lane_pad(dt_bias[0]),
        a_log=lane_pad(a_log[0]),
        d_skip=row(jnp.repeat(d_skip[0], SSM_HEAD_DIM)),
        g_ssm=row(g_ssm[0]),
        w_ssm_branch=w_ssm_branch[0].astype(BF16),
        lb_logits=lb_logits.astype(F32),
        g_hgrn=row(g_hgrn[0]),
        w_hgrn_branch=w_hgrn_branch[0].astype(BF16),
        w_out=w_out[0].astype(BF16),
        g_mem=row(g_mem[0]),
        w_xk=w_xk[0].astype(BF16),
        w_xv=w_xv[0].astype(BF16),
        g_xattn=row(g_xattn[0]),
        w_xq=w_xq[0].astype(BF16),
        w_xo=w_xo[0].astype(BF16),
        g_moe=row(g_moe[0]),
        w_router=jnp.pad(w_router[0], ((0, 0), (0, LANES - N_EXPERTS))).astype(BF16),
        b_router=lane_pad(b_router[0], fill=-jnp.inf),
        w_gate_up=w_gate_up.astype(F32),
        b_gate_up=b_gate_up[0].reshape(N_EXPERTS, 1, 2 * D_EXPERT).astype(F32),
        w_down=w_down.astype(F32),
        b_down=b_down[0].reshape(N_EXPERTS, 1, d).astype(F32),
        expand=jnp.asarray(expand, BF16),
    )


def _group_to_router(x, conv0, ssm0, hg0, mk, mv, cnt0, p, n_all, first_token, xn_all):
    b, t, d = x.shape
    ma, conv_n, ssm_n = _ssd_mixer(x, conv0, ssm0, p, min(SSD_TILE, t))
    mb, hg_n = _hgrn_mixer(x, hg0, p, min(HGRN_TILE, t))
    h2, xn_all, meta, gates, counts = _post_mixer(x, ma, mb, mk, mv, cnt0, p, min(POST_TILE, t), n_all,
                                                  first_token, xn_all)
    n = b * t
    meta = meta.transpose(1, 0, 2).reshape(2 * TOP_K, n)
    gates = gates.transpose(1, 0, 2).reshape(2 * TOP_K, n)
    return (h2.reshape(n, d), meta, gates), xn_all, counts, (conv_n, ssm_n, hg_n)


def kernel(x_prompt, x_sample, mem_prompt, state_conv, state_ssm, state_hgrn, cache_mem_k, cache_mem_v, g_mix, w_in, conv_w, conv_b, dt_bias, a_log, d_skip, g_ssm, w_ssm_branch, lb_logits, g_hgrn, w_hgrn_branch, w_out, g_mem, w_xk, w_xv, g_xattn, w_xq, w_xo, g_moe, w_router, b_router, w_gate_up, b_gate_up, w_down, b_down, g_final):
    p = _prepare(g_mix, w_in, conv_w, conv_b, dt_bias, a_log, d_skip, g_ssm, w_ssm_branch, lb_logits, g_hgrn,
                 w_hgrn_branch, w_out, g_mem, w_xk, w_xv, g_xattn, w_xq, w_xo, g_moe, w_router, b_router,
                 w_gate_up, b_gate_up, w_down, b_down)
    g_fin = g_final.reshape(1, -1).astype(F32)
    if w_in.shape[0] != 1:
        raise NotImplementedError("a single layer (DEPTH == 1) is implemented")
    bp = x_prompt.shape[0]
    bs = x_sample.shape[0]

    mk_p, mv_p = _memory_kv(mem_prompt, p)
    n_p = bp * x_prompt.shape[1]
    n_s = bs * x_sample.shape[1]
    (h_p, meta_p, gates_p), xn_all, counts, (conv_p, ssm_p, hg_p) = _group_to_router(
        x_prompt,
        jnp.zeros((1, bp, CONV_W - 1, CONV_CH), F32),
        jnp.zeros((1, bp, SSM_HEADS, SSM_HEAD_DIM, SSM_STATE), F32),
        jnp.zeros((1, bp, HG_HEADS, HG_K, HG_K), F32),
        mk_p, mv_p, jnp.zeros((N_EXPERTS, LANES), F32), p, n_p + n_s, 0, None)
    (h_s, meta_s, gates_s), xn_all, counts, (conv_s, ssm_s, hg_s) = _group_to_router(
        x_sample, state_conv, state_ssm, state_hgrn,
        cache_mem_k, cache_mem_v, counts, p, n_p + n_s, n_p, xn_all)
    y_tok = _moe_outputs(xn_all, jnp.concatenate([meta_p, meta_s], axis=1), counts, p)
    y_p = _combine(h_p, gates_p, y_tok, 0, g_fin, min(COMBINE_TILE, n_p)).reshape(x_prompt.shape)
    y_s = _combine(h_s, gates_s, y_tok, n_p, g_fin, min(COMBINE_TILE, n_s)).reshape(x_sample.shape)

    return (y_p, y_s,
            conv_p[None], ssm_p[None], hg_p[None], mk_p, mv_p,
            conv_s[None], ssm_s[None], hg_s[None])
```
